```python
import jax, jax.numpy as jnp
from jax import lax
import numpy as np

D_MODEL = 2048
BATCH = 32
SEQ = 256
DEPTH = 2
DEC_BATCH = 4
DEC_SEQ = 2048
PAST_LEN = 256

GRID_W = 64
CHUNK = 128
EPS = 1e-6
M_HEADS = 4
M_DK = 256
M_DV = 256
M_WIDTH = M_HEADS * M_DV
R_HEADS = 8
R_DK = 128
R_DV = 128
R_WIDTH = R_HEADS * R_DV
ROPE_BASE = 10000.0
ROPE_FREQS = R_DK // 4
S_HEADS = 16
S_P = 64
S_GROUPS = 2
S_N = 128
S_WIDTH = S_HEADS * S_P
CONV_W = 4
CONV_CH = S_WIDTH + 2 * S_GROUPS * S_N
D_FF = 4 * D_MODEL
N_BRANCH = 3
IN_SIZES = (M_HEADS * M_DK, M_HEADS * M_DK, M_WIDTH, M_WIDTH, 2 * M_HEADS, 2 * M_HEADS,
            R_HEADS * R_DK, R_HEADS * R_DK, R_WIDTH, R_WIDTH,
            S_WIDTH, CONV_CH, 2 * S_HEADS,
            N_BRANCH * D_MODEL)
IN_COLS = sum(IN_SIZES)

kernel_name = "hybrid_mlstm_retnet_ssd_diffusion_step"


def rmsnorm(x, g):
    xf = x.astype(jnp.float32)
    y = xf * lax.rsqrt(jnp.mean(xf * xf, -1, keepdims=True) + EPS)
    return (y * g.astype(jnp.float32)).astype(x.dtype)


def head_rmsnorm(x, g):
    H, d = x.shape[-2:]
    y = x * lax.rsqrt(jnp.mean(x * x, -1, keepdims=True) + EPS)
    return y * g.astype(jnp.float32).reshape(H, d)


def flip(a):
    return jnp.flip(a, axis=1)


def rope_tables(L):
    n_rows = L // GRID_W
    rows = jnp.repeat(jnp.arange(n_rows, dtype=jnp.float32), GRID_W)
    cols = jnp.tile(jnp.arange(GRID_W, dtype=jnp.float32), n_rows)
    inv = ROPE_BASE ** (-jnp.arange(ROPE_FREQS, dtype=jnp.float32) / ROPE_FREQS)
    ang = jnp.concatenate([rows[:, None] * inv, cols[:, None] * inv], -1)
    return jnp.cos(ang), jnp.sin(ang)


def apply_rope(x, cos, sin):
    x1, x2 = jnp.split(x, 2, axis=-1)
    c = cos[None, :, None, :]
    s = sin[None, :, None, :]
    return jnp.concatenate([x1 * c - x2 * s, x2 * c + x1 * s], -1)


def conv_centred(x, w, b):
    left = (CONV_W - 1) // 2
    y = lax.conv_general_dilated(x, w[:, None, :], window_strides=(1,),
                                 padding=[(left, CONV_W - 1 - left)],
                                 dimension_numbers=('NWC', 'WIO', 'NWC'),
                                 feature_group_count=x.shape[-1])
    return y + b


def mlstm_chunked(q, k, v, i_pre, log_f, state0):
    C0, n0, m0 = state0
    B, L, H, dk = q.shape
    nc = L // CHUNK
    qc = q.reshape(B, nc, CHUNK, H, dk)
    kc = k.reshape(B, nc, CHUNK, H, dk)
    vc = v.reshape(B, nc, CHUNK, H, -1)
    ih = jnp.swapaxes(i_pre.reshape(B, nc, CHUNK, H), 2, 3)
    bh = jnp.cumsum(jnp.swapaxes(log_f.reshape(B, nc, CHUNK, H), 2, 3), -1)
    b_last = bh[..., -1]
    causal = jnp.tril(jnp.ones((CHUNK, CHUNK), dtype=bool))
    logw = jnp.where(causal, bh[..., :, None] - bh[..., None, :] + ih[..., None, :], -jnp.inf)
    m_loc = jnp.max(logw, -1)
    s = jnp.einsum('bcihd,bcjhd->bchij', qc, kc) * jnp.exp(logw - m_loc[..., None])
    a_num = jnp.einsum('bchij,bcjhe->bcihe', s, vc)
    a_den = jnp.swapaxes(jnp.sum(s, -1), 2, 3)
    logw_end = b_last[..., None] - bh + ih
    m_chunk = jnp.max(logw_end, -1)
    w_end = jnp.exp(logw_end - m_chunk[..., None])
    u_C = jnp.einsum('bchj,bcjhd,bcjhe->bchde', w_end, kc, vc)
    u_n = jnp.einsum('bchj,bcjhd->bchd', w_end, kc)

    def step(carry, xs):
        C, n, m = carry
        bl, mc, uC, un = xs
        m_new = jnp.maximum(bl + m, mc)
        a = jnp.exp(bl + m - m_new)
        e = jnp.exp(mc - m_new)
        C_new = a[..., None, None] * C + e[..., None, None] * uC
        n_new = a[..., None] * n + e[..., None] * un
        return (C_new, n_new, m_new), (C, n, m)

    xs = (jnp.moveaxis(b_last, 1, 0), jnp.moveaxis(m_chunk, 1, 0),
          jnp.moveaxis(u_C, 1, 0), jnp.moveaxis(u_n, 1, 0))
    fin, prev = lax.scan(step, (C0, n0, m0), xs)
    C_prev = jnp.moveaxis(prev[0], 0, 1)
    n_prev = jnp.moveaxis(prev[1], 0, 1)
    m_prev = jnp.moveaxis(prev[2], 0, 1)
    g = bh + m_prev[..., None]
    m_tot = jnp.maximum(m_loc, g)
    e_intra = jnp.swapaxes(jnp.exp(m_loc - m_tot), 2, 3)
    e_inter = jnp.swapaxes(jnp.exp(g - m_tot), 2, 3)
    num = (e_intra[..., None] * a_num
           + e_inter[..., None] * jnp.einsum('bcihd,bchde->bcihe', qc, C_prev))
    den = e_intra * a_den + e_inter * jnp.einsum('bcihd,bchd->bcih', qc, n_prev)
    floor = jnp.exp(-jnp.swapaxes(m_tot, 2, 3))
    h = num / jnp.maximum(jnp.abs(den), floor)[..., None]
    return h.reshape(B, L, H, -1), fin


def retention_chunked(q, k, v, log_gamma, s0):
    B, L, H, dk = q.shape
    nc = L // CHUNK
    qc = q.reshape(B, nc, CHUNK, H, dk)
    kc = k.reshape(B, nc, CHUNK, H, dk)
    vc = v.reshape(B, nc, CHUNK, H, -1)
    pos = jnp.arange(CHUNK, dtype=jnp.float32)
    rel = pos[:, None] - pos[None, :]
    decay = jnp.exp(jnp.where(rel >= 0, rel[None] * log_gamma[:, None, None], -jnp.inf))
    scores = jnp.einsum('bcihd,bcjhd->bchij', qc, kc) * decay
    o_intra = jnp.einsum('bchij,bcjhe->bcihe', scores, vc)
    w_k = jnp.exp((CHUNK - 1 - pos)[:, None] * log_gamma[None, :])
    u = jnp.einsum('bcjhd,jh,bcjhe->bchde', kc, w_k, vc)
    chunk_decay = jnp.exp(CHUNK * log_gamma)[:, None, None]

    def step(s, u_c):
        return chunk_decay * s + u_c, s

    s_fin, s_prev = lax.scan(step, s0, jnp.moveaxis(u, 1, 0))
    s_prev = jnp.moveaxis(s_prev, 0, 1)
    w_q = jnp.exp((pos + 1)[:, None] * log_gamma[None, :])
    o_inter = jnp.einsum('bcihd,ih,bchde->bcihe', qc, w_q, s_prev)
    return (o_intra + o_inter).reshape(B, L, H, -1), s_fin


def ssd_chunked(x, dt, A, Bm, Cm, h0):
    B, L, H, P = x.shape
    G, N = Bm.shape[-2:]
    K = H // G
    nc = L // CHUNK
    xc = x.reshape(B, nc, CHUNK, G, K, P)
    dtc = dt.reshape(B, nc, CHUNK, G, K)
    Bc = Bm.reshape(B, nc, CHUNK, G, N)
    Cc = Cm.reshape(B, nc, CHUNK, G, N)
    cs = jnp.cumsum(jnp.moveaxis(dtc * A.reshape(G, K), 2, -1), -1)
    causal = jnp.tril(jnp.ones((CHUNK, CHUNK), dtype=bool))
    decay = jnp.exp(jnp.where(causal, cs[..., :, None] - cs[..., None, :], -jnp.inf))
    cb = jnp.einsum('bcigs,bcjgs->bcgij', Cc, Bc)
    y_intra = jnp.einsum('bcgij,bcgkij,bcjgk,bcjgkp->bcigkp', cb, decay, dtc, xc)
    w_end = jnp.exp(cs[..., -1:] - cs)
    u = jnp.einsum('bcgkj,bcjgk,bcjgkp,bcjgs->bcgkps', w_end, dtc, xc, Bc)
    chunk_decay = jnp.exp(cs[..., -1])

    def step(h, xs):
        a, u_c = xs
        return a[..., None, None] * h + u_c, h

    fin, h_prev = lax.scan(step, h0.reshape(B, G, K, P, N),
                           (jnp.moveaxis(chunk_decay, 1, 0), jnp.moveaxis(u, 1, 0)))
    h_prev = jnp.moveaxis(h_prev, 0, 1)
    y_inter = jnp.einsum('bcigs,bcgkps,bcgki->bcigkp', Cc, h_prev, jnp.exp(cs))
    return (y_intra + y_inter).reshape(B, L, H, P), fin.reshape(B, H, P, N)


def token_mixers(h, lp, st0, rope):
    B, L, _ = h.shape
    f32 = lambda a: a.astype(jnp.float32)
    split_at = [int(s) for s in np.cumsum(IN_SIZES)[:-1]]
    (mq, mk, mv, mo, mi, mf, rq, rk, rv, rg, sz, sxbc, sdt, gl) = jnp.split(h @ lp['w_in'], split_at, axis=-1)
    mC0, mn0, mm0, r0, s0 = (f32(a) for a in st0)

    q = f32(mq).reshape(B, L, M_HEADS, M_DK)
    k = f32(mk).reshape(B, L, M_HEADS, M_DK) * (M_DK ** -0.5)
    v = f32(mv).reshape(B, L, M_HEADS, M_DV)
    ig = f32(mi).reshape(B, L, 2, M_HEADS) + f32(lp['m_igate_b'])
    lf = jax.nn.log_sigmoid(f32(mf).reshape(B, L, 2, M_HEADS) + f32(lp['m_fgate_b']))
    hm_f, ms_f = mlstm_chunked(q, k, v, ig[:, :, 0], lf[:, :, 0], (mC0[:, 0], mn0[:, 0], mm0[:, 0]))
    hm_b, ms_b = mlstm_chunked(flip(q), flip(k), flip(v), flip(ig[:, :, 1]), flip(lf[:, :, 1]),
                               (mC0[:, 1], mn0[:, 1], mm0[:, 1]))
    y_m = head_rmsnorm(hm_f + flip(hm_b), lp['m_norm_g']).reshape(B, L, M_WIDTH) * jax.nn.sigmoid(f32(mo))

    q = f32(rq).reshape(B, L, R_HEADS, R_DK)
    k = f32(rk).reshape(B, L, R_HEADS, R_DK) * (R_DK ** -0.5)
    if rope is not None:
        q = apply_rope(q, *rope)
        k = apply_rope(k, *rope)
    v = f32(rv).reshape(B, L, R_HEADS, R_DV)
    log_gamma = -jnp.exp(f32(lp['r_decay']))
    or_f, rs_f = retention_chunked(q, k, v, log_gamma[0], r0[:, 0])
    or_b, rs_b = retention_chunked(flip(q), flip(k), flip(v), log_gamma[1], r0[:, 1])
    y_r = head_rmsnorm(or_f + flip(or_b), lp['r_norm_g']).reshape(B, L, R_WIDTH) * jax.nn.silu(f32(rg))

    xbc = jax.nn.silu(conv_centred(f32(sxbc), f32(lp['s_conv_w']), f32(lp['s_conv_b'])))
    xs, bs, cs = jnp.split(xbc, [S_WIDTH, S_WIDTH + S_GROUPS * S_N], axis=-1)
    xs = xs.reshape(B, L, S_HEADS, S_P)
    bs = bs.reshape(B, L, S_GROUPS, S_N)
    cs = cs.reshape(B, L, S_GROUPS, S_N)
    dt = jax.nn.softplus(f32(sdt).reshape(B, L, 2, S_HEADS) + f32(lp['s_dt_bias']))
    A = -jnp.exp(f32(lp['s_a_log']))
    ys_f, ss_f = ssd_chunked(xs, dt[:, :, 0], A[0], bs, cs, s0[:, 0])
    ys_b, ss_b = ssd_chunked(flip(xs), flip(dt[:, :, 1]), A[1], flip(bs), flip(cs), s0[:, 1])
    y_s = (ys_f + flip(ys_b) + f32(lp['s_d'])[:, None] * xs).reshape(B, L, S_WIDTH)
    y_s = rmsnorm(y_s * jax.nn.silu(f32(sz)), lp['s_norm_g'])

    dtp = h.dtype
    gates = jax.nn.sigmoid(f32(gl).reshape(B, L, N_BRANCH, D_MODEL)).astype(dtp)
    merged = (gates[:, :, 0] * (y_m.astype(dtp) @ lp['w_br_m'])
              + gates[:, :, 1] * (y_r.astype(dtp) @ lp['w_br_r'])
              + gates[:, :, 2] * (y_s.astype(dtp) @ lp['w_br_s']))
    out = merged @ lp['w_out']
    new_st = (jnp.stack([ms_f[0], ms_b[0]], 1), jnp.stack([ms_f[1], ms_b[1]], 1),
              jnp.stack([ms_f[2], ms_b[2]], 1), jnp.stack([rs_f, rs_b], 1),
              jnp.stack([ss_f, ss_b], 1))
    return out, new_st


def trunk_layer(x, mod, st0, rope, lp):
    sh_a, sc_a, g_a, sh_f, sc_f, g_f = jnp.split(mod[:, None, :], 6, axis=-1)
    h = rmsnorm(x, lp['norm_mix_g']) * (1 + sc_a) + sh_a
    y, st = token_mixers(h, lp, st0, rope)
    x = x + g_a * y
    h = rmsnorm(x, lp['norm_mlp_g']) * (1 + sc_f) + sh_f
    x = x + g_f * (jnp.square(jax.nn.relu(h @ lp['w_ff1'])) @ lp['w_ff2'])
    return x, st


def setup_inputs(seed: int = 0) -> dict:
    key = jax.random.key(seed)
    ks = iter(jax.random.split(key, 48))
    f = jnp.float32
    D = D_MODEL

    def nrm(shape, scale):
        return jax.random.normal(next(ks), shape, f) * scale

    def unif(shape, lo, hi):
        return jax.random.uniform(next(ks), shape, f, lo, hi)

    ret_base = jnp.log(-jnp.log1p(-(2.0 ** (-5.0 - jnp.arange(R_HEADS, dtype=f)))))
    dt0 = jnp.exp(unif((DEPTH, 2, S_HEADS), float(np.log(1e-3)), float(np.log(1e-1))))
    return {
        "x_prompt": nrm((BATCH, SEQ, D), 1.0),
        "x_sample": nrm((DEC_BATCH, DEC_SEQ, D), 1.0),
        "c": nrm((DEC_BATCH, D), 1.0),
        "state_mlstm_C": nrm((DEC_BATCH, DEPTH, 2, M_HEADS, M_DK, M_DV), 0.1),
        "state_mlstm_n": nrm((DEC_BATCH, DEPTH, 2, M_HEADS, M_DK), 0.1),
        "state_mlstm_m": nrm((DEC_BATCH, DEPTH, 2, M_HEADS), 0.5),
        "state_ret": nrm((DEC_BATCH, DEPTH, 2, R_HEADS, R_DK, R_DV), 0.5),
        "state_ssd": nrm((DEC_BATCH, DEPTH, 2, S_HEADS, S_P, S_N), 0.3),
        "c_ctx": nrm((D,), 1.0),
        "w_mod": nrm((DEPTH, D, 6 * D), 0.5 * D ** -0.5),
        "b_mod": nrm((DEPTH, 6 * D), 0.02),
        "norm_mix_g": 1.0 + nrm((DEPTH, D), 0.02),
        "norm_mlp_g": 1.0 + nrm((DEPTH, D), 0.02),
        "w_in": nrm((DEPTH, D, IN_COLS), D ** -0.5),
        "m_igate_b": nrm((DEPTH, 2, M_HEADS), 0.1),
        "m_fgate_b": jnp.linspace(3.0, 6.0, M_HEADS, dtype=f)[None, None] + nrm((DEPTH, 2, M_HEADS), 0.1),
        "m_norm_g": 1.0 + nrm((DEPTH, M_WIDTH), 0.02),
        "r_decay": ret_base[None, None] + nrm((DEPTH, 2, R_HEADS), 0.05),
        "r_norm_g": 1.0 + nrm((DEPTH, R_WIDTH), 0.02),
        "s_conv_w": nrm((DEPTH, CONV_W, CONV_CH), CONV_W ** -0.5),
        "s_conv_b": nrm((DEPTH, CONV_CH), 0.02),
        "s_dt_bias": dt0 + jnp.log(-jnp.expm1(-dt0)),
        "s_a_log": jnp.log(unif((DEPTH, 2, S_HEADS), 1.0, 16.0)),
        "s_d": 1.0 + nrm((DEPTH, S_HEADS), 0.1),
        "s_norm_g": 1.0 + nrm((DEPTH, S_WIDTH), 0.02),
        "w_br_m": nrm((DEPTH, M_WIDTH, D), M_WIDTH ** -0.5),
        "w_br_r": nrm((DEPTH, R_WIDTH, D), R_WIDTH ** -0.5),
        "w_br_s": nrm((DEPTH, S_WIDTH, D), S_WIDTH ** -0.5),
        "w_out": nrm((DEPTH, D, D), D ** -0.5),
        "w_ff1": nrm((DEPTH, D, D_FF), D ** -0.5),
        "w_ff2": nrm((DEPTH, D_FF, D), D_FF ** -0.5),
        "final_norm_g": 1.0 + nrm((D,), 0.02),
    }


def reference(x_prompt, x_sample, c, state_mlstm_C, state_mlstm_n, state_mlstm_m, state_ret, state_ssd,
              c_ctx, w_mod, b_mod, norm_mix_g, norm_mlp_g, w_in, m_igate_b, m_fgate_b, m_norm_g,
              r_decay, r_norm_g, s_conv_w, s_conv_b, s_dt_bias, s_a_log, s_d, s_norm_g,
              w_br_m, w_br_r, w_br_s, w_out, w_ff1, w_ff2, final_norm_g):
    Bp = x_prompt.shape[0]
    Ls = x_sample.shape[1]
    rope = rope_tables(Ls)
    f = jnp.float32
    zero_state = (jnp.zeros((Bp, 2, M_HEADS, M_DK, M_DV), f), jnp.zeros((Bp, 2, M_HEADS, M_DK), f),
                  jnp.zeros((Bp, 2, M_HEADS), f), jnp.zeros((Bp, 2, R_HEADS, R_DK, R_DV), f),
                  jnp.zeros((Bp, 2, S_HEADS, S_P, S_N), f))
    xp, xs = x_prompt, x_sample
    st_mC, st_mn, st_mm, st_r, st_s = [], [], [], [], []
    for l in range(DEPTH):
        lp = dict(w_in=w_in[l], m_igate_b=m_igate_b[l], m_fgate_b=m_fgate_b[l], m_norm_g=m_norm_g[l],
                  r_decay=r_decay[l], r_norm_g=r_norm_g[l], s_conv_w=s_conv_w[l], s_conv_b=s_conv_b[l],
                  s_dt_bias=s_dt_bias[l], s_a_log=s_a_log[l], s_d=s_d[l], s_norm_g=s_norm_g[l],
                  w_br_m=w_br_m[l], w_br_r=w_br_r[l], w_br_s=w_br_s[l], w_out=w_out[l],
                  w_ff1=w_ff1[l], w_ff2=w_ff2[l], norm_mix_g=norm_mix_g[l], norm_mlp_g=norm_mlp_g[l])
        mod_ctx = jax.nn.silu(c_ctx)[None, :] @ w_mod[l] + b_mod[l]
        mod_lat = jax.nn.silu(c) @ w_mod[l] + b_mod[l]
        xp, st = trunk_layer(xp, mod_ctx, zero_state, None, lp)
        st_mC.append(st[0]); st_mn.append(st[1]); st_mm.append(st[2]); st_r.append(st[3]); st_s.append(st[4])
        cache_l = (state_mlstm_C[:, l], state_mlstm_n[:, l], state_mlstm_m[:, l], state_ret[:, l], state_ssd[:, l])
        xs, _ = trunk_layer(xs, mod_lat, cache_l, rope, lp)
    y_prompt = rmsnorm(xp, final_norm_g)
    y_sample = rmsnorm(xs, final_norm_g)
    new_mlstm_C = jnp.stack(st_mC, 1)
    new_mlstm_n = jnp.stack(st_mn, 1)
    new_mlstm_m = jnp.stack(st_mm, 1)
    new_ret = jnp.stack(st_r, 1)
    new_ssd = jnp.stack(st_s, 1)
    return (y_prompt, y_sample, new_mlstm_C, new_mlstm_n, new_mlstm_m, new_ret, new_ssd)
```

```python
import functools

import numpy as np
import jax
import jax.numpy as jnp
from jax import lax
from jax.experimental import pallas as pl
from jax.experimental.pallas import tpu as pltpu

F32 = jnp.float32
BF16 = jnp.bfloat16

D_MODEL = 2048
DEPTH = 2
CHUNK = 128
EPS = 1e-6
M_HEADS, M_DK = 4, 256
R_HEADS, R_DK = 8, 128
S_HEADS, S_P, S_GROUPS, S_N = 16, 64, 2, 128
S_HPG = S_HEADS // S_GROUPS
S_GW = S_HPG * S_P
GRID_W = 64
ROPE_BASE = 10000.0
ROPE_FREQS = R_DK // 4
D_FF = 4 * D_MODEL
LANES = 128
VMEM_LIMIT = 56 * 1024 * 1024

OFF_MQ, OFF_MK, OFF_MV, OFF_MO = 0, 1024, 2048, 3072
OFF_RQ, OFF_RK, OFF_RV, OFF_RG = 4096, 5120, 6144, 7168
OFF_SZ, OFF_SX, OFF_SB, OFF_SC = 8192, 9216, 10240, 10496
OFF_GL = 10752
N_PROJ = 16896
N_GATES = (M_HEADS + S_GROUPS) * LANES


def _cparams(n_axes):
    return pltpu.CompilerParams(dimension_semantics=("arbitrary",) * n_axes,
                                vmem_limit_bytes=VMEM_LIMIT)


def _dot(a, b):
    return jnp.dot(a, b, preferred_element_type=F32)


def _dot_nt(a, b):
    return lax.dot_general(a, b, (((1,), (1,)), ((), ())), preferred_element_type=F32)


def _split2(x):
    hi = x.astype(BF16)
    lo = (x - hi.astype(F32)).astype(BF16)
    return hi, lo


def _split3(x):
    hi = x.astype(BF16)
    r = x - hi.astype(F32)
    mid = r.astype(BF16)
    lo = (r - mid.astype(F32)).astype(BF16)
    return hi, mid, lo


def _cumsum_rows(x, tri):
    hi, mid, lo = _split3(x)
    return _dot(tri, hi) + _dot(tri, mid) + _dot(tri, lo)


def _expand(x, e):
    hi, lo = _split2(x)
    return _dot(hi, e) + _dot(lo, e)


def _softplus(x):
    return jnp.maximum(x, 0.0) + jnp.log1p(jnp.exp(-jnp.abs(x)))


def _sigmoid(x):
    return 1.0 / (1.0 + jnp.exp(-x))


def _silu(x):
    return x * _sigmoid(x)


def _mod_kernel(c_ref, w_ref, b_ref, o_ref):
    a = _silu(c_ref[...]).astype(BF16)
    o_ref[...] = _dot(a, w_ref[...].astype(BF16)) + b_ref[...]


def _mod_call(c_rows, w_mod, b_mod):
    tn = 1024
    n = w_mod.shape[-1]
    return pl.pallas_call(
        _mod_kernel,
        grid=(DEPTH, n // tn),
        in_specs=[pl.BlockSpec((8, D_MODEL), lambda l, j: (0, 0)),
                  pl.BlockSpec((None, D_MODEL, tn), lambda l, j: (l, 0, j)),
                  pl.BlockSpec((None, 1, tn), lambda l, j: (l, 0, j))],
        out_specs=pl.BlockSpec((None, 8, tn), lambda l, j: (l, 0, j)),
        out_shape=jax.ShapeDtypeStruct((DEPTH, 8, n), F32),
        compiler_params=_cparams(2),
        name="mod_proj",
    )(c_rows, w_mod, b_mod.reshape(DEPTH, 1, n))


def _norm_mm_kernel(*refs, with_gates, relu2):
    if with_gates:
        x_ref, g_ref, sc_ref, sh_ref, w_ref, wg_ref, bg_ref, o_ref, og_ref, h_scr = refs
    else:
        x_ref, g_ref, sc_ref, sh_ref, w_ref, o_ref, h_scr = refs

    @pl.when(pl.program_id(1) == 0)
    def _():
        x = x_ref[...]
        ms = jnp.mean(x * x, axis=-1, keepdims=True)
        y = x * lax.rsqrt(ms + EPS) * g_ref[...]
        hb = (y * (1.0 + sc_ref[...]) + sh_ref[...]).astype(BF16)
        h_scr[...] = hb
        if with_gates:
            pre = _dot(hb, wg_ref[...]) + bg_ref[...]
            lane = lax.broadcasted_iota(jnp.int32, (pre.shape[0], LANES), 1)
            for blk in range(N_GATES // LANES):
                v = pre[:, blk * LANES:(blk + 1) * LANES]
                if blk < M_HEADS:
                    v = jnp.where(lane < 2, v, -_softplus(-v))
                else:
                    v = _softplus(v)
                og_ref[:, blk * LANES:(blk + 1) * LANES] = v

    acc = _dot(h_scr[...], w_ref[...])
    if relu2:
        r = jnp.maximum(acc, 0.0)
        acc = r * r
    o_ref[...] = acc.astype(o_ref.dtype)


def _norm_mm_call(x, g, sc, sh, rows_per_mod, w, tm, tn, out_dtype, relu2, gates=None, name="norm_mm"):
    t, d = x.shape
    n = w.shape[1]
    mod_idx = lambda i, j: ((i * tm) // rows_per_mod, 0, 0)
    in_specs = [pl.BlockSpec((tm, d), lambda i, j: (i, 0)),
                pl.BlockSpec((1, d), lambda i, j: (0, 0)),
                pl.BlockSpec((None, 1, d), mod_idx),
                pl.BlockSpec((None, 1, d), mod_idx),
                pl.BlockSpec((d, tn), lambda i, j: (0, j))]
    args = [x, g, sc, sh, w]
    out_specs = pl.BlockSpec((tm, tn), lambda i, j: (i, j))
    out_shape = jax.ShapeDtypeStruct((t, n), out_dtype)
    if gates is not None:
        wg, bg = gates
        in_specs += [pl.BlockSpec((d, N_GATES), lambda i, j: (0, 0)),
                     pl.BlockSpec((1, N_GATES), lambda i, j: (0, 0))]
        args += [wg, bg]
        out_specs = [out_specs, pl.BlockSpec((tm, N_GATES), lambda i, j: (i, 0))]
        out_shape = [out_shape, jax.ShapeDtypeStruct((t, N_GATES), F32)]
    return pl.pallas_call(
        functools.partial(_norm_mm_kernel, with_gates=gates is not None, relu2=relu2),
        grid=(t // tm, n // tn),
        in_specs=in_specs, out_specs=out_specs, out_shape=out_shape,
        scratch_shapes=[pltpu.VMEM((tm, d), BF16)],
        compiler_params=_cparams(2),
        name=name,
    )(*args)


def _mm_resid_kernel(*refs, nk, final_norm):
    if final_norm:
        a_ref, w_ref, r_ref, gate_ref, fg_ref, o_ref, acc_scr = refs
    else:
        a_ref, w_ref, r_ref, gate_ref, o_ref, acc_scr = refs
    k = pl.program_id(1)
    p = _dot(a_ref[...], w_ref[...])

    def finish(acc):
        xn = r_ref[...] + gate_ref[...] * acc
        if final_norm:
            ms = jnp.mean(xn * xn, axis=-1, keepdims=True)
            xn = xn * lax.rsqrt(ms + EPS) * fg_ref[...]
        o_ref[...] = xn

    if nk == 1:
        finish(p)
    else:
        @pl.when(k == 0)
        def _():
            acc_scr[...] = p

        @pl.when(jnp.logical_and(k > 0, k < nk - 1))
        def _():
            acc_scr[...] += p

        @pl.when(k == nk - 1)
        def _():
            finish(acc_scr[...] + p)


def _mm_resid_call(a, w, resid, gate, rows_per_mod, tm, tk, final_g=None, name="mm_resid"):
    t, kdim = a.shape
    n = w.shape[1]
    nk = kdim // tk
    in_specs = [pl.BlockSpec((tm, tk), lambda i, k: (i, k)),
                pl.BlockSpec((tk, n), lambda i, k: (k, 0)),
                pl.BlockSpec((tm, n), lambda i, k: (i, 0)),
                pl.BlockSpec((None, 1, n), lambda i, k: ((i * tm) // rows_per_mod, 0, 0))]
    args = [a, w, resid, gate]
    if final_g is not None:
        in_specs.append(pl.BlockSpec((1, n), lambda i, k: (0, 0)))
        args.append(final_g)
    return pl.pallas_call(
        functools.partial(_mm_resid_kernel, nk=nk, final_norm=final_g is not None),
        grid=(t // tm, nk),
        in_specs=in_specs,
        out_specs=pl.BlockSpec((tm, n), lambda i, k: (i, 0)),
        out_shape=jax.ShapeDtypeStruct((t, n), F32),
        scratch_shapes=[pltpu.VMEM((tm, n), F32)],
        compiler_params=_cparams(2),
        name=name,
    )(*args)


def _merge_kernel(ym_ref, yr_ref, ts_ref, sg_ref, wm_ref, wr_ref, ws_ref, g0_ref, g1_ref, g2_ref,
                  o_ref, ys_scr):
    @pl.when(pl.program_id(1) == 0)
    def _():
        t = ts_ref[...]
        ms = jnp.mean(t * t, axis=-1, keepdims=True)
        ys_scr[...] = (t * lax.rsqrt(ms + EPS) * sg_ref[...]).astype(BF16)

    acc = _sigmoid(g0_ref[...]) * _dot(ym_ref[...], wm_ref[...])
    acc += _sigmoid(g1_ref[...]) * _dot(yr_ref[...], wr_ref[...])
    acc += _sigmoid(g2_ref[...]) * _dot(ys_scr[...], ws_ref[...])
    o_ref[...] = acc.astype(o_ref.dtype)


def _merge_call(ym, yr, ts, sg, wm, wr, ws, proj, tm, tn):
    t, w = ym.shape
    d = wm.shape[1]
    gl_blk = OFF_GL // tn
    per_br = d // tn
    yspec = pl.BlockSpec((tm, w), lambda i, j: (i, 0))
    wspec = pl.BlockSpec((w, tn), lambda i, j: (0, j))
    gspec = lambda br: pl.BlockSpec((tm, tn), lambda i, j: (i, gl_blk + br * per_br + j))
    return pl.pallas_call(
        _merge_kernel,
        grid=(t // tm, d // tn),
        in_specs=[yspec, yspec, yspec, pl.BlockSpec((1, w), lambda i, j: (0, 0)),
                  wspec, wspec, wspec, gspec(0), gspec(1), gspec(2)],
        out_specs=pl.BlockSpec((tm, tn), lambda i, j: (i, j)),
        out_shape=jax.ShapeDtypeStruct((t, d), BF16),
        scratch_shapes=[pltpu.VMEM((tm, w), BF16)],
        compiler_params=_cparams(2),
        name="merge",
    )(ym, yr, ts, sg, wm, wr, ws, proj, proj, proj)


def _chunk_masks():
    row = lax.broadcasted_iota(jnp.int32, (CHUNK, CHUNK), 0)
    col = lax.broadcasted_iota(jnp.int32, (CHUNK, CHUNK), 1)
    lower = col <= row
    upper = col >= row
    tri = jnp.where(lower, 1.0, 0.0).astype(BF16)
    return row, col, lower, upper, tri


def _mlstm_kernel(*refs, nc, has_state, emit_state):
    it = iter(refs)
    q_ref, k_ref, v_ref, o_ref, g_ref, ng_ref = (next(it) for _ in range(6))
    if has_state:
        c0_ref, n0_ref, m0_ref = (next(it) for _ in range(3))
    y_ref = next(it)
    if emit_state:
        cf_ref, nf_ref, mf_ref = (next(it) for _ in range(3))
    hacc, c_scr, n_scr, m_scr = (next(it) for _ in range(4))

    _, _, lower, upper, tri = _chunk_masks()
    if has_state:
        c_scr[...] = c0_ref[...]
        n_scr[0:2, :] = n0_ref[...]
        m_scr[0:2, :] = m0_ref[...]
    else:
        c_scr[...] = jnp.zeros_like(c_scr)
        n_scr[...] = jnp.zeros_like(n_scr)
        m_scr[...] = jnp.zeros_like(m_scr)

    def chunk(c, d):
        off = pl.multiple_of(c * CHUNK, CHUNK)
        rows = pl.ds(off, CHUNK)
        qb = q_ref[rows, :].astype(BF16)
        k = k_ref[rows, :] * (M_DK ** -0.5)
        kb = k.astype(BF16)
        vb = v_ref[rows, :].astype(BF16)
        g = g_ref[rows, :]
        cs = _cumsum_rows(g, tri)
        tot = cs[CHUNK - 1:CHUNK, :]
        bsum = cs if d == 0 else tot - cs + g
        mask = lower if d == 0 else upper
        bsum_t = bsum.T
        g_t = g.T
        b_col = bsum[:, 2 + d:3 + d]
        b_row = bsum_t[2 + d:3 + d, :]
        i_col = g[:, d:d + 1]
        i_row = g_t[d:d + 1, :]
        b_last = tot[:, 2 + d:3 + d]

        logw = jnp.where(mask, b_col - b_row + i_row, -jnp.inf)
        m_loc = jnp.max(logw, axis=1, keepdims=True)
        s = _dot_nt(qb, kb) * jnp.exp(logw - m_loc)
        a_num = _dot(s.astype(BF16), vb)
        a_den = jnp.sum(s, axis=1, keepdims=True)

        m_prev = m_scr[d:d + 1, 0:1]
        c_prev = c_scr[d]
        n_prev = n_scr[d:d + 1, :]
        gg = b_col + m_prev
        m_tot = jnp.maximum(m_loc, gg)
        e_intra = jnp.exp(m_loc - m_tot)
        e_inter = jnp.exp(gg - m_tot)
        q_c = _dot(qb, c_prev.astype(BF16))
        q_n = jnp.sum(qb.astype(F32) * n_prev.astype(BF16).astype(F32), axis=1, keepdims=True)
        num = e_intra * a_num + e_inter * q_c
        den = e_intra * a_den + e_inter * q_n
        h = num / jnp.maximum(jnp.abs(den), jnp.exp(-m_tot))

        m_chunk = jnp.max(b_last - b_row + i_row, axis=1, keepdims=True)
        kw = k * jnp.exp(b_last - b_col + i_col - m_chunk)
        u_c = _dot(kw.T.astype(BF16), vb)
        u_n = jnp.sum(kw, axis=0, keepdims=True)
        m_new = jnp.maximum(b_last + m_prev, m_chunk)
        a = jnp.exp(b_last + m_prev - m_new)
        e = jnp.exp(m_chunk - m_new)
        c_scr[d] = a * c_prev + e * u_c
        n_scr[d:d + 1, :] = a * n_prev + e * u_n
        m_scr[d:d + 1, :] = jnp.broadcast_to(m_new, (1, LANES))
        return rows, h

    def fwd(c, carry):
        rows, h = chunk(c, 0)
        hacc[rows, :] = h
        return carry

    def bwd(i, carry):
        rows, h = chunk(nc - 1 - i, 1)
        ht = hacc[rows, :] + h
        ms = jnp.mean(ht * ht, axis=1, keepdims=True)
        y = ht * lax.rsqrt(ms + EPS) * ng_ref[...] * _sigmoid(o_ref[rows, :])
        y_ref[rows, :] = y.astype(y_ref.dtype)
        return carry

    lax.fori_loop(0, nc, fwd, 0)
    lax.fori_loop(0, nc, bwd, 0)
    if emit_state:
        cf_ref[...] = c_scr[...]
        nf_ref[...] = n_scr[0:2, :]
        mf_ref[...] = m_scr[0:2, :]


def _mlstm_call(proj, gates, ng, nb, seq, state=None, emit_state=False):
    t = proj.shape[0]
    nc = seq // CHUNK
    dk = M_DK
    cspec = lambda base: pl.BlockSpec((seq, dk), lambda b, h: (b, base // dk + h))
    in_specs = [cspec(OFF_MQ), cspec(OFF_MK), cspec(OFF_MV), cspec(OFF_MO),
                pl.BlockSpec((seq, LANES), lambda b, h: (b, h)),
                pl.BlockSpec((1, dk), lambda b, h: (0, h))]
    args = [proj, proj, proj, proj, gates, ng]
    if state is not None:
        in_specs += [pl.BlockSpec((None, 2, None, dk, dk), lambda b, h: (b, 0, h, 0, 0)),
                     pl.BlockSpec((None, None, 2, dk), lambda b, h: (b, h, 0, 0)),
                     pl.BlockSpec((None, None, 2, LANES), lambda b, h: (b, h, 0, 0))]
        args += list(state)
    out_specs = [pl.BlockSpec((seq, dk), lambda b, h: (b, h))]
    out_shape = [jax.ShapeDtypeStruct((t, M_HEADS * dk), BF16)]
    if emit_state:
        out_specs += [pl.BlockSpec((None, 2, None, dk, dk), lambda b, h: (b, 0, h, 0, 0)),
                      pl.BlockSpec((None, None, 2, dk), lambda b, h: (b, h, 0, 0)),
                      pl.BlockSpec((None, None, 2, LANES), lambda b, h: (b, h, 0, 0))]
        out_shape += [jax.ShapeDtypeStruct((nb, 2, M_HEADS, dk, dk), F32),
                      jax.ShapeDtypeStruct((nb, M_HEADS, 2, dk), F32),
                      jax.ShapeDtypeStruct((nb, M_HEADS, 2, LANES), F32)]
    return pl.pallas_call(
        functools.partial(_mlstm_kernel, nc=nc, has_state=state is not None, emit_state=emit_state),
        grid=(nb, M_HEADS),
        in_specs=in_specs, out_specs=out_specs, out_shape=out_shape,
        scratch_shapes=[pltpu.VMEM((seq, dk), F32), pltpu.VMEM((2, dk, dk), F32),
                        pltpu.VMEM((8, dk), F32), pltpu.VMEM((8, LANES), F32)],
        compiler_params=_cparams(2),
        name="mlstm",
    )(*args)


def _ret_kernel(*refs, nc, has_state, emit_state, rope):
    it = iter(refs)
    q_ref, k_ref, v_ref, g_ref, lg_ref, ng_ref = (next(it) for _ in range(6))
    if rope:
        cos_ref, sin_ref = next(it), next(it)
    if has_state:
        s0_ref = next(it)
    y_ref = next(it)
    if emit_state:
        sf_ref = next(it)
    oacc, s_scr = next(it), next(it)

    row, col, lower, upper, _ = _chunk_masks()
    rel = (row - col).astype(F32)
    pos = lax.broadcasted_iota(jnp.int32, (CHUNK, 1), 0).astype(F32)
    lg = -jnp.exp(lg_ref[...])
    if has_state:
        s_scr[...] = s0_ref[...]
    else:
        s_scr[...] = jnp.zeros_like(s_scr)

    def chunk(c, d):
        off = pl.multiple_of(c * CHUNK, CHUNK)
        rows = pl.ds(off, CHUNK)
        lgd = lg[d:d + 1, 0:1]
        q = q_ref[rows, :]
        k = k_ref[rows, :] * (R_DK ** -0.5)
        if rope:
            cs, sn = cos_ref[rows, :], sin_ref[rows, :]
            q = q * cs + pltpu.roll(q, R_DK // 2, 1) * sn
            k = k * cs + pltpu.roll(k, R_DK // 2, 1) * sn
        qb = q.astype(BF16)
        vb = v_ref[rows, :].astype(BF16)
        if d == 0:
            decay = jnp.where(lower, jnp.exp(rel * lgd), 0.0)
            w_q = jnp.exp((pos + 1.0) * lgd)
            w_k = jnp.exp((CHUNK - 1.0 - pos) * lgd)
        else:
            decay = jnp.where(upper, jnp.exp(-rel * lgd), 0.0)
            w_q = jnp.exp((CHUNK - pos) * lgd)
            w_k = jnp.exp(pos * lgd)
        s_prev = s_scr[d]
        sc = _dot_nt(qb, k.astype(BF16)) * decay
        o = _dot(sc.astype(BF16), vb) + w_q * _dot(qb, s_prev.astype(BF16))
        u = _dot((k * w_k).T.astype(BF16), vb)
        s_scr[d] = jnp.exp(CHUNK * lgd) * s_prev + u
        return rows, o

    def fwd(c, carry):
        rows, o = chunk(c, 0)
        oacc[rows, :] = o
        return carry

    def bwd(i, carry):
        rows, o = chunk(nc - 1 - i, 1)
        ot = oacc[rows, :] + o
        ms = jnp.mean(ot * ot, axis=1, keepdims=True)
        y = ot * lax.rsqrt(ms + EPS) * ng_ref[...] * _silu(g_ref[rows, :])
        y_ref[rows, :] = y.astype(y_ref.dtype)
        return carry

    lax.fori_loop(0, nc, fwd, 0)
    lax.fori_loop(0, nc, bwd, 0)
    if emit_state:
        sf_ref[...] = s_scr[...]


def _ret_call(proj, lgr, ng, nb, seq, rope=None, state=None, emit_state=False):
    t = proj.shape[0]
    nc = seq // CHUNK
    dk = R_DK
    cspec = lambda base: pl.BlockSpec((seq, dk), lambda b, h: (b, base // dk + h))
    in_specs = [cspec(OFF_RQ), cspec(OFF_RK), cspec(OFF_RV), cspec(OFF_RG),
                pl.BlockSpec((None, 2, LANES), lambda b, h: (h, 0, 0)),
                pl.BlockSpec((1, dk), lambda b, h: (0, h))]
    args = [proj, proj, proj, proj, lgr, ng]
    if rope is not None:
        in_specs += [pl.BlockSpec((seq, dk), lambda b, h: (0, 0))] * 2
        args += list(rope)
    sspec = pl.BlockSpec((None, 2, None, dk, dk), lambda b, h: (b, 0, h, 0, 0))
    if state is not None:
        in_specs.append(sspec)
        args.append(state)
    out_specs = [pl.BlockSpec((seq, dk), lambda b, h: (b, h))]
    out_shape = [jax.ShapeDtypeStruct((t, R_HEADS * dk), BF16)]
    if emit_state:
        out_specs.append(sspec)
        out_shape.append(jax.ShapeDtypeStruct((nb, 2, R_HEADS, dk, dk), F32))
    return pl.pallas_call(
        functools.partial(_ret_kernel, nc=nc, has_state=state is not None, emit_state=emit_state,
                          rope=rope is not None),
        grid=(nb, R_HEADS),
        in_specs=in_specs, out_specs=out_specs, out_shape=out_shape,
        scratch_shapes=[pltpu.VMEM((seq, dk), F32), pltpu.VMEM((2, dk, dk), F32)],
        compiler_params=_cparams(2),
        name="retention",
    )(*args)


def _ssd_kernel(*refs, nc, has_state, emit_state):
    it = iter(refs)
    (x_ref, b_ref, c_ref, z_ref, g_ref, wx_ref, wb_ref, wc_ref, bx_ref, bb_ref, bc_ref,
     al_ref, sd_ref) = (next(it) for _ in range(13))
    if has_state:
        h0_ref = next(it)
    y_ref = next(it)
    if emit_state:
        hf_ref = next(it)
    xs_scr, bt_scr, bb_scr, cc_scr, ht_scr = (next(it) for _ in range(5))

    _, _, lower, upper, tri = _chunk_masks()
    rowi = lax.broadcasted_iota(jnp.int32, (CHUNK, 1), 0)
    er = lax.broadcasted_iota(jnp.int32, (LANES, S_GW), 0)
    ec = lax.broadcasted_iota(jnp.int32, (LANES, S_GW), 1) // S_P
    expand = [jnp.where(er == ec + d * S_HPG, 1.0, 0.0).astype(BF16) for d in range(2)]
    a_row = -jnp.exp(al_ref[...])

    def conv_silu(ref, w_ref, bias_ref, c, rows):
        off = c * CHUNK
        cur = ref[rows, :]
        prev8 = ref[pl.ds(pl.multiple_of(jnp.maximum(off - 8, 0), 8), 8), :]
        next8 = ref[pl.ds(pl.multiple_of(jnp.minimum(off + CHUNK, (nc - 1) * CHUNK + CHUNK - 8), 8), 8), :]
        has_prev = jnp.where(c > 0, 1.0, 0.0)
        has_next = jnp.where(c < nc - 1, 1.0, 0.0)
        p_last = prev8[7:8, :] * has_prev
        n0 = next8[0:1, :] * has_next
        n1 = next8[1:2, :] * has_next
        xm1 = jnp.where(rowi == 0, p_last, pltpu.roll(cur, 1, 0))
        xp1 = jnp.where(rowi == CHUNK - 1, n0, pltpu.roll(cur, CHUNK - 1, 0))
        xp2 = jnp.where(rowi == CHUNK - 2, n0, jnp.where(rowi == CHUNK - 1, n1, pltpu.roll(cur, CHUNK - 2, 0)))
        w = w_ref[...]
        y = w[0:1, :] * xm1 + w[1:2, :] * cur + w[2:3, :] * xp1 + w[3:4, :] * xp2 + bias_ref[...]
        return _silu(y)

    def prep(c, carry):
        off = pl.multiple_of(c * CHUNK, CHUNK)
        rows = pl.ds(off, CHUNK)
        xs_scr[rows, :] = conv_silu(x_ref, wx_ref, bx_ref, c, rows)
        bm = conv_silu(b_ref, wb_ref, bb_ref, c, rows)
        bb_scr[rows, :] = bm.astype(BF16)
        bt_scr[:, rows] = bm.T.astype(BF16)
        cc_scr[rows, :] = conv_silu(c_ref, wc_ref, bc_ref, c, rows).astype(BF16)
        return carry

    lax.fori_loop(0, nc, prep, 0)

    for d in range(2):
        if has_state:
            ht_scr[d] = h0_ref[d].T
        else:
            ht_scr[d] = jnp.zeros((S_N, S_GW), F32)

    def chunk(c, d):
        off = pl.multiple_of(c * CHUNK, CHUNK)
        rows = pl.ds(off, CHUNK)
        xs = xs_scr[rows, :]
        xb = xs.astype(BF16)
        bmb = bb_scr[rows, :]
        cmb = cc_scr[rows, :]
        dt = g_ref[rows, :]
        a = dt * a_row
        cs = _cumsum_rows(a, tri)
        tot = cs[CHUNK - 1:CHUNK, :]
        bsum = cs if d == 0 else tot - cs + a
        mask = lower if d == 0 else upper
        bsum_t = bsum.T
        dt_t = dt.T
        cb = _dot_nt(cmb, bmb)
        ys = []
        for k in range(S_HPG):
            ln = d * S_HPG + k
            decay = jnp.exp(jnp.where(mask, bsum[:, ln:ln + 1] - bsum_t[ln:ln + 1, :], -jnp.inf))
            m = (cb * decay * dt_t[ln:ln + 1, :]).astype(BF16)
            ys.append(_dot(m, xb[:, k * S_P:(k + 1) * S_P]))
        y = jnp.concatenate(ys, axis=1)
        ht = ht_scr[d]
        y = y + _expand(jnp.exp(bsum), expand[d]) * _dot(cmb, ht.astype(BF16))
        wexp = _expand(jnp.exp(tot - bsum) * dt, expand[d])
        u_t = _dot(bt_scr[:, rows], (xs * wexp).astype(BF16))
        cd = _expand(jnp.broadcast_to(jnp.exp(tot), (8, LANES)), expand[d])[0:1, :]
        ht_scr[d] = ht * cd + u_t
        return rows, y, xs

    def fwd(c, carry):
        rows, y, _ = chunk(c, 0)
        y_ref[rows, :] = y
        return carry

    def bwd(i, carry):
        rows, y, xs = chunk(nc - 1 - i, 1)
        yt = y_ref[rows, :] + y + sd_ref[...] * xs
        y_ref[rows, :] = yt * _silu(z_ref[rows, :])
        return carry

    lax.fori_loop(0, nc, fwd, 0)
    lax.fori_loop(0, nc, bwd, 0)
    if emit_state:
        for d in range(2):
            hf_ref[d] = ht_scr[d].T


def _ssd_call(proj, gates, conv_w, conv_b, alog, sd, nb, seq, state=None, emit_state=False):
    t = proj.shape[0]
    nc = seq // CHUNK
    gw = S_GW
    n = S_N
    xw = S_HEADS * S_P
    in_specs = [pl.BlockSpec((seq, gw), lambda b, g: (b, OFF_SX // gw + g)),
                pl.BlockSpec((seq, n), lambda b, g: (b, OFF_SB // n + g)),
                pl.BlockSpec((seq, n), lambda b, g: (b, OFF_SC // n + g)),
                pl.BlockSpec((seq, gw), lambda b, g: (b, OFF_SZ // gw + g)),
                pl.BlockSpec((seq, LANES), lambda b, g: (b, M_HEADS + g)),
                pl.BlockSpec((4, gw), lambda b, g: (0, g)),
                pl.BlockSpec((4, n), lambda b, g: (0, xw // n + g)),
                pl.BlockSpec((4, n), lambda b, g: (0, xw // n + S_GROUPS + g)),
                pl.BlockSpec((1, gw), lambda b, g: (0, g)),
                pl.BlockSpec((1, n), lambda b, g: (0, xw // n + g)),
                pl.BlockSpec((1, n), lambda b, g: (0, xw // n + S_GROUPS + g)),
                pl.BlockSpec((None, 1, LANES), lambda b, g: (g, 0, 0)),
                pl.BlockSpec((1, gw), lambda b, g: (0, g))]
    args = [proj, proj, proj, proj, gates, conv_w, conv_w, conv_w, conv_b, conv_b, conv_b, alog, sd]
    sspec = pl.BlockSpec((None, 2, None, gw, n), lambda b, g: (b, 0, g, 0, 0))
    if state is not None:
        in_specs.append(sspec)
        args.append(state)
    out_specs = [pl.BlockSpec((seq, gw), lambda b, g: (b, g))]
    out_shape = [jax.ShapeDtypeStruct((t, xw), F32)]
    if emit_state:
        out_specs.append(sspec)
        out_shape.append(jax.ShapeDtypeStruct((nb, 2, S_GROUPS, gw, n), F32))
    return pl.pallas_call(
        functools.partial(_ssd_kernel, nc=nc, has_state=state is not None, emit_state=emit_state),
        grid=(nb, S_GROUPS),
        in_specs=in_specs, out_specs=out_specs, out_shape=out_shape,
        scratch_shapes=[pltpu.VMEM((seq, gw), F32), pltpu.VMEM((n, seq), BF16),
                        pltpu.VMEM((seq, n), BF16), pltpu.VMEM((seq, n), BF16),
                        pltpu.VMEM((2, n, gw), F32)],
        compiler_params=_cparams(2),
        name="ssd",
    )(*args)


_IN_SIZES = (1024, 1024, 1024, 1024, 8, 8, 1024, 1024, 1024, 1024, 1024, 1536, 32, 6144)
_IN_OFFS = np.concatenate([[0], np.cumsum(_IN_SIZES)])


def _gate_lane_index():
    idx = -np.ones((N_GATES,), np.int64)
    for h in range(M_HEADS):
        base = h * LANES
        idx[base + 0] = 0 * M_HEADS + h
        idx[base + 1] = 1 * M_HEADS + h
        idx[base + 2] = 2 * M_HEADS + 0 * M_HEADS + h
        idx[base + 3] = 2 * M_HEADS + 1 * M_HEADS + h
    for g in range(S_GROUPS):
        base = (M_HEADS + g) * LANES
        for d in range(2):
            for k in range(S_HPG):
                idx[base + d * S_HPG + k] = 4 * M_HEADS + d * S_HEADS + g * S_HPG + k
    return idx


def _place(vals, idx):
    taken = jnp.take(vals, jnp.asarray(np.maximum(idx, 0)), axis=-1)
    return jnp.where(jnp.asarray(idx >= 0), taken, 0.0)


def _prep_layer(l, w_in, m_igate_b, m_fgate_b, s_dt_bias, s_a_log, r_decay, s_d):
    w = w_in[l]
    o = _IN_OFFS
    w_big = jnp.concatenate([w[:, o[0]:o[4]], w[:, o[6]:o[12]], w[:, o[13]:o[14]]], axis=1).astype(BF16)
    w_small = jnp.concatenate([w[:, o[4]:o[6]], w[:, o[12]:o[13]]], axis=1)
    gidx = _gate_lane_index()
    w_g = _place(w_small, gidx).astype(BF16)
    b_small = jnp.concatenate([m_igate_b[l].reshape(-1), m_fgate_b[l].reshape(-1), s_dt_bias[l].reshape(-1)])
    b_g = _place(b_small, gidx).reshape(1, N_GATES)
    aidx = -np.ones((S_GROUPS, LANES), np.int64)
    for g in range(S_GROUPS):
        for d in range(2):
            for k in range(S_HPG):
                aidx[g, d * S_HPG + k] = d * S_HEADS + g * S_HPG + k
    alog = jnp.stack([_place(s_a_log[l].reshape(-1), aidx[g]) for g in range(S_GROUPS)]).reshape(S_GROUPS, 1, LANES)
    lgr = jnp.broadcast_to(r_decay[l].T[:, :, None], (R_HEADS, 2, LANES))
    sd = jnp.repeat(s_d[l], S_P).reshape(1, S_HEADS * S_P)
    return w_big, w_g, b_g, alog, lgr, sd


def _rope_tables(seq):
    n_rows = seq // GRID_W
    rows = jnp.repeat(jnp.arange(n_rows, dtype=F32), GRID_W)
    cols = jnp.tile(jnp.arange(GRID_W, dtype=F32), n_rows)
    inv = ROPE_BASE ** (-jnp.arange(ROPE_FREQS, dtype=F32) / ROPE_FREQS)
    ang = jnp.concatenate([rows[:, None] * inv, cols[:, None] * inv], -1)
    cos, sin = jnp.cos(ang), jnp.sin(ang)
    return jnp.concatenate([cos, cos], -1), jnp.concatenate([-sin, sin], -1)


def _layer(x, mods, rows_per_mod, nb, seq, lw, rope, state, emit_state, final_g):
    sh_a, sc_a, g_a, sh_f, sc_f, g_f = mods
    tm_l, tm_s = min(1024, rows_per_mod), min(512, rows_per_mod)
    proj, gates = _norm_mm_call(x, lw["norm_mix_g"], sc_a, sh_a, rows_per_mod, lw["w_big"], tm_l, 768,
                                F32, False, gates=(lw["w_g"], lw["b_g"]), name="in_proj")
    st_m = st_r = st_s = None
    if state is not None:
        st_m, st_r, st_s = state
    om = _mlstm_call(proj, gates, lw["m_norm_g"], nb, seq, state=st_m, emit_state=emit_state)
    orr = _ret_call(proj, lw["lgr"], lw["r_norm_g"], nb, seq, rope=rope, state=st_r, emit_state=emit_state)
    os_ = _ssd_call(proj, gates, lw["s_conv_w"], lw["s_conv_b"], lw["alog"], lw["sd"], nb, seq,
                    state=st_s, emit_state=emit_state)
    merged = _merge_call(om[0], orr[0], os_[0], lw["s_norm_g"], lw["w_br_m"], lw["w_br_r"], lw["w_br_s"],
                         proj, tm_l, 512)
    x = _mm_resid_call(merged, lw["w_out"], x, g_a, rows_per_mod, tm_s, D_MODEL, name="out_proj")
    hid = _norm_mm_call(x, lw["norm_mlp_g"], sc_f, sh_f, rows_per_mod, lw["w_ff1"], tm_l, 1024,
                        BF16, True, name="mlp_up")
    x = _mm_resid_call(hid, lw["w_ff2"], x, g_f, rows_per_mod, tm_s, 1024, final_g=final_g, name="mlp_down")
    new_state = (om[1:], orr[1:], os_[1:]) if emit_state else None
    return x, new_state


def kernel(x_prompt, x_sample, c, state_mlstm_C, state_mlstm_n, state_mlstm_m, state_ret, state_ssd, c_ctx, w_mod, b_mod, norm_mix_g, norm_mlp_g, w_in, m_igate_b, m_fgate_b, m_norm_g, r_decay, r_norm_g, s_conv_w, s_conv_b, s_dt_bias, s_a_log, s_d, s_norm_g, w_br_m, w_br_r, w_br_s, w_out, w_ff1, w_ff2, final_norm_g):
    bp, lp, d = x_prompt.shape
    bs, ls, _ = x_sample.shape
    xp = x_prompt.reshape(bp * lp, d)
    xs = x_sample.reshape(bs * ls, d)

    c_rows = jnp.zeros((8, d), F32).at[:bs].set(c).at[bs].set(c_ctx)
    mod = _mod_call(c_rows, w_mod, b_mod)
    rope = _rope_tables(ls)
    final_g = final_norm_g.reshape(1, d)

    st_c, st_n, st_m, st_r, st_s = [], [], [], [], []
    for l in range(DEPTH):
        w_big, w_g, b_g, alog, lgr, sd = _prep_layer(l, w_in, m_igate_b, m_fgate_b, s_dt_bias, s_a_log,
                                                     r_decay, s_d)
        lw = dict(w_big=w_big, w_g=w_g, b_g=b_g, alog=alog, lgr=lgr, sd=sd,
                  norm_mix_g=norm_mix_g[l].reshape(1, d), norm_mlp_g=norm_mlp_g[l].reshape(1, d),
                  m_norm_g=m_norm_g[l].reshape(1, -1), r_norm_g=r_norm_g[l].reshape(1, -1),
                  s_norm_g=s_norm_g[l].reshape(1, -1),
                  s_conv_w=s_conv_w[l], s_conv_b=s_conv_b[l].reshape(1, -1),
                  w_br_m=w_br_m[l].astype(BF16), w_br_r=w_br_r[l].astype(BF16), w_br_s=w_br_s[l].astype(BF16),
                  w_out=w_out[l].astype(BF16), w_ff1=w_ff1[l].astype(BF16), w_ff2=w_ff2[l].astype(BF16))
        parts = mod[l].reshape(8, 6, 1, d)
        mods_ctx = tuple(parts[bs:bs + 1, i] for i in range(6))
        mods_lat = tuple(parts[:bs, i] for i in range(6))
        fg = final_g if l == DEPTH - 1 else None

        xp, st = _layer(xp, mods_ctx, bp * lp, bp, lp, lw, None, None, True, fg)
        (cf, nf, mf), (rf,), (hf,) = st
        st_c.append(cf)
        st_n.append(jnp.transpose(nf, (0, 2, 1, 3)))
        st_m.append(jnp.transpose(mf[..., 0], (0, 2, 1)))
        st_r.append(rf)
        st_s.append(hf.reshape(bp, 2, S_HEADS, S_P, S_N))

        cache = (
            (state_mlstm_C[:, l],
             jnp.transpose(state_mlstm_n[:, l], (0, 2, 1, 3)),
             jnp.broadcast_to(jnp.transpose(state_mlstm_m[:, l], (0, 2, 1))[..., None], (bs, M_HEADS, 2, LANES))),
            state_ret[:, l],
            state_ssd[:, l].reshape(bs, 2, S_GROUPS, S_GW, S_N),
        )
        xs, _ = _layer(xs, mods_lat, ls, bs, ls, lw, rope, cache, False, fg)

    return (xp.reshape(bp, lp, d), xs.reshape(bs, ls, d),
            jnp.stack(st_c, 1), jnp.stack(st_n, 1), jnp.stack(st_m, 1), jnp.stack(st_r, 1), jnp.stack(st_s, 1))
```

```python
import functools

import numpy as np
import jax
import jax.numpy as jnp
from jax import lax
from jax.experimental import pallas as pl
from jax.experimental.pallas import tpu as pltpu

F32 = jnp.float32
BF16 = jnp.bfloat16

D_MODEL = 2048
DEPTH = 2
CHUNK = 128
EPS = 1e-6
M_HEADS, M_DK = 4, 256
R_HEADS, R_DK = 8, 128
S_HEADS, S_P, S_GROUPS, S_N = 16, 64, 2, 128
S_HPG = S_HEADS // S_GROUPS
S_GW = S_HPG * S_P
GRID_W = 64
ROPE_BASE = 10000.0
ROPE_FREQS = R_DK // 4
D_FF = 4 * D_MODEL
LANES = 128
VMEM_LIMIT = 56 * 1024 * 1024
SCAN_UNROLL = 2

OFF_MQ, OFF_MK, OFF_MV, OFF_MO = 0, 1024, 2048, 3072
OFF_RQ, OFF_RK, OFF_RV, OFF_RG = 4096, 5120, 6144, 7168
OFF_SZ, OFF_SX, OFF_SB, OFF_SC = 8192, 9216, 10240, 10496
OFF_GL = 10752
N_PROJ = 16896
N_GATES = (M_HEADS + S_GROUPS) * LANES


def _cparams(n_axes):
    return pltpu.CompilerParams(dimension_semantics=("arbitrary",) * n_axes,
                                vmem_limit_bytes=VMEM_LIMIT)


def _dot(a, b):
    return jnp.dot(a, b, preferred_element_type=F32)


def _dot_nt(a, b):
    return lax.dot_general(a, b, (((1,), (1,)), ((), ())), preferred_element_type=F32)


def _split2(x):
    hi = x.astype(BF16)
    lo = (x - hi.astype(F32)).astype(BF16)
    return hi, lo


def _split3(x):
    hi = x.astype(BF16)
    r = x - hi.astype(F32)
    mid = r.astype(BF16)
    lo = (r - mid.astype(F32)).astype(BF16)
    return hi, mid, lo


def _cumsum_rows(x, tri):
    hi, mid, lo = _split3(x)
    return _dot(tri, hi) + _dot(tri, mid) + _dot(tri, lo)


def _expand(x, e):
    hi, lo = _split2(x)
    return _dot(hi, e) + _dot(lo, e)


def _softplus(x):
    return jnp.maximum(x, 0.0) + jnp.log1p(jnp.exp(-jnp.abs(x)))


def _sigmoid(x):
    return 1.0 / (1.0 + jnp.exp(-x))


def _silu(x):
    return x * _sigmoid(x)


def _mod_kernel(c_ref, w_ref, b_ref, o_ref):
    a = _silu(c_ref[...]).astype(BF16)
    o_ref[...] = _dot(a, w_ref[...].astype(BF16)) + b_ref[...]


def _mod_call(c_rows, w_mod, b_mod):
    tn = 1024
    n = w_mod.shape[-1]
    return pl.pallas_call(
        _mod_kernel,
        grid=(DEPTH, n // tn),
        in_specs=[pl.BlockSpec((8, D_MODEL), lambda l, j: (0, 0)),
                  pl.BlockSpec((None, D_MODEL, tn), lambda l, j: (l, 0, j)),
                  pl.BlockSpec((None, 1, tn), lambda l, j: (l, 0, j))],
        out_specs=pl.BlockSpec((None, 8, tn), lambda l, j: (l, 0, j)),
        out_shape=jax.ShapeDtypeStruct((DEPTH, 8, n), F32),
        compiler_params=_cparams(2),
        name="mod_proj",
    )(c_rows, w_mod, b_mod.reshape(DEPTH, 1, n))


def _norm_mm_kernel(*refs, with_gates, relu2):
    if with_gates:
        x_ref, g_ref, sc_ref, sh_ref, w_ref, wg_ref, bg_ref, o_ref, og_ref, h_scr = refs
    else:
        x_ref, g_ref, sc_ref, sh_ref, w_ref, o_ref, h_scr = refs

    @pl.when(pl.program_id(1) == 0)
    def _():
        x = x_ref[...]
        ms = jnp.mean(x * x, axis=-1, keepdims=True)
        y = x * lax.rsqrt(ms + EPS) * g_ref[...]
        hb = (y * (1.0 + sc_ref[...]) + sh_ref[...]).astype(BF16)
        h_scr[...] = hb
        if with_gates:
            pre = _dot(hb, wg_ref[...]) + bg_ref[...]
            lane = lax.broadcasted_iota(jnp.int32, (pre.shape[0], LANES), 1)
            for blk in range(N_GATES // LANES):
                v = pre[:, blk * LANES:(blk + 1) * LANES]
                if blk < M_HEADS:
                    v = jnp.where(lane < 2, v, -_softplus(-v))
                else:
                    v = _softplus(v)
                og_ref[:, blk * LANES:(blk + 1) * LANES] = v

    acc = _dot(h_scr[...], w_ref[...])
    if relu2:
        r = jnp.maximum(acc, 0.0)
        acc = r * r
    o_ref[...] = acc.astype(o_ref.dtype)


def _norm_mm_call(x, g, sc, sh, rows_per_mod, w, tm, tn, out_dtype, relu2, gates=None, name="norm_mm"):
    t, d = x.shape
    n = w.shape[1]
    mod_idx = lambda i, j: ((i * tm) // rows_per_mod, 0, 0)
    in_specs = [pl.BlockSpec((tm, d), lambda i, j: (i, 0)),
                pl.BlockSpec((1, d), lambda i, j: (0, 0)),
                pl.BlockSpec((None, 1, d), mod_idx),
                pl.BlockSpec((None, 1, d), mod_idx),
                pl.BlockSpec((d, tn), lambda i, j: (0, j))]
    args = [x, g, sc, sh, w]
    out_specs = pl.BlockSpec((tm, tn), lambda i, j: (i, j))
    out_shape = jax.ShapeDtypeStruct((t, n), out_dtype)
    if gates is not None:
        wg, bg = gates
        in_specs += [pl.BlockSpec((d, N_GATES), lambda i, j: (0, 0)),
                     pl.BlockSpec((1, N_GATES), lambda i, j: (0, 0))]
        args += [wg, bg]
        out_specs = [out_specs, pl.BlockSpec((tm, N_GATES), lambda i, j: (i, 0))]
        out_shape = [out_shape, jax.ShapeDtypeStruct((t, N_GATES), F32)]
    return pl.pallas_call(
        functools.partial(_norm_mm_kernel, with_gates=gates is not None, relu2=relu2),
        grid=(t // tm, n // tn),
        in_specs=in_specs, out_specs=out_specs, out_shape=out_shape,
        scratch_shapes=[pltpu.VMEM((tm, d), BF16)],
        compiler_params=_cparams(2),
        name=name,
    )(*args)


def _mm_resid_kernel(*refs, nk, final_norm):
    if final_norm:
        a_ref, w_ref, r_ref, gate_ref, fg_ref, o_ref, acc_scr = refs
    else:
        a_ref, w_ref, r_ref, gate_ref, o_ref, acc_scr = refs
    k = pl.program_id(1)
    p = _dot(a_ref[...], w_ref[...])

    def finish(acc):
        xn = r_ref[...] + gate_ref[...] * acc
        if final_norm:
            ms = jnp.mean(xn * xn, axis=-1, keepdims=True)
            xn = xn * lax.rsqrt(ms + EPS) * fg_ref[...]
        o_ref[...] = xn

    if nk == 1:
        finish(p)
    else:
        @pl.when(k == 0)
        def _():
            acc_scr[...] = p

        @pl.when(jnp.logical_and(k > 0, k < nk - 1))
        def _():
            acc_scr[...] += p

        @pl.when(k == nk - 1)
        def _():
            finish(acc_scr[...] + p)


def _mm_resid_call(a, w, resid, gate, rows_per_mod, tm, tk, final_g=None, name="mm_resid"):
    t, kdim = a.shape
    n = w.shape[1]
    nk = kdim // tk
    in_specs = [pl.BlockSpec((tm, tk), lambda i, k: (i, k)),
                pl.BlockSpec((tk, n), lambda i, k: (k, 0)),
                pl.BlockSpec((tm, n), lambda i, k: (i, 0)),
                pl.BlockSpec((None, 1, n), lambda i, k: ((i * tm) // rows_per_mod, 0, 0))]
    args = [a, w, resid, gate]
    if final_g is not None:
        in_specs.append(pl.BlockSpec((1, n), lambda i, k: (0, 0)))
        args.append(final_g)
    return pl.pallas_call(
        functools.partial(_mm_resid_kernel, nk=nk, final_norm=final_g is not None),
        grid=(t // tm, nk),
        in_specs=in_specs,
        out_specs=pl.BlockSpec((tm, n), lambda i, k: (i, 0)),
        out_shape=jax.ShapeDtypeStruct((t, n), F32),
        scratch_shapes=[pltpu.VMEM((tm, n), F32)],
        compiler_params=_cparams(2),
        name=name,
    )(*args)


def _merge_kernel(ym_ref, yr_ref, ts_ref, sg_ref, wm_ref, wr_ref, ws_ref, g0_ref, g1_ref, g2_ref,
                  o_ref, ys_scr):
    @pl.when(pl.program_id(1) == 0)
    def _():
        t = ts_ref[...]
        ms = jnp.mean(t * t, axis=-1, keepdims=True)
        ys_scr[...] = (t * lax.rsqrt(ms + EPS) * sg_ref[...]).astype(BF16)

    acc = _sigmoid(g0_ref[...]) * _dot(ym_ref[...], wm_ref[...])
    acc += _sigmoid(g1_ref[...]) * _dot(yr_ref[...], wr_ref[...])
    acc += _sigmoid(g2_ref[...]) * _dot(ys_scr[...], ws_ref[...])
    o_ref[...] = acc.astype(o_ref.dtype)


def _merge_call(ym, yr, ts, sg, wm, wr, ws, proj, tm, tn):
    t, w = ym.shape
    d = wm.shape[1]
    gl_blk = OFF_GL // tn
    per_br = d // tn
    yspec = pl.BlockSpec((tm, w), lambda i, j: (i, 0))
    wspec = pl.BlockSpec((w, tn), lambda i, j: (0, j))
    gspec = lambda br: pl.BlockSpec((tm, tn), lambda i, j: (i, gl_blk + br * per_br + j))
    return pl.pallas_call(
        _merge_kernel,
        grid=(t // tm, d // tn),
        in_specs=[yspec, yspec, yspec, pl.BlockSpec((1, w), lambda i, j: (0, 0)),
                  wspec, wspec, wspec, gspec(0), gspec(1), gspec(2)],
        out_specs=pl.BlockSpec((tm, tn), lambda i, j: (i, j)),
        out_shape=jax.ShapeDtypeStruct((t, d), BF16),
        scratch_shapes=[pltpu.VMEM((tm, w), BF16)],
        compiler_params=_cparams(2),
        name="merge",
    )(ym, yr, ts, sg, wm, wr, ws, proj, proj, proj)


def _chunk_masks():
    row = lax.broadcasted_iota(jnp.int32, (CHUNK, CHUNK), 0)
    col = lax.broadcasted_iota(jnp.int32, (CHUNK, CHUNK), 1)
    lower = col <= row
    upper = col >= row
    tri = jnp.where(lower, 1.0, 0.0).astype(BF16)
    return row, col, lower, upper, tri


def _mlstm_kernel(*refs, nc, has_state, emit_state):
    it = iter(refs)
    q_ref, k_ref, v_ref, o_ref, g_ref, ng_ref = (next(it) for _ in range(6))
    if has_state:
        c0_ref, n0_ref, m0_ref = (next(it) for _ in range(3))
    y_ref = next(it)
    if emit_state:
        cf_ref, nf_ref, mf_ref = (next(it) for _ in range(3))
    hacc = (next(it), next(it))
    c_scr = (next(it), next(it))
    n_scr = (next(it), next(it))
    m_scr = (next(it), next(it))

    _, _, lower, upper, tri = _chunk_masks()
    for d in range(2):
        if has_state:
            c_scr[d][...] = c0_ref[d]
            n_scr[d][0:1, :] = n0_ref[d:d + 1, :]
            m_scr[d][0:1, :] = m0_ref[d:d + 1, :]
        else:
            c_scr[d][...] = jnp.zeros_like(c_scr[d])
            n_scr[d][...] = jnp.zeros_like(n_scr[d])
            m_scr[d][...] = jnp.zeros_like(m_scr[d])

    dirs = (0, 1)
    masks = (lower, upper)

    def scan(i, carry):
        cidx = (i, nc - 1 - i)
        rows = [pl.ds(pl.multiple_of(c * CHUNK, CHUNK), CHUNK) for c in cidx]
        g = [g_ref[r, :] for r in rows]
        cs = [_cumsum_rows(x, tri) for x in g]
        tot = [x[CHUNK - 1:CHUNK, :] for x in cs]
        bsum = [cs[0], tot[1] - cs[1] + g[1]]
        bsum_t = [x.T for x in bsum]
        g_t = [x.T for x in g]
        b_col = [bsum[d][:, 2 + d:3 + d] for d in dirs]
        b_row = [bsum_t[d][2 + d:3 + d, :] for d in dirs]
        i_col = [g[d][:, d:d + 1] for d in dirs]
        i_row = [g_t[d][d:d + 1, :] for d in dirs]
        b_last = [tot[d][:, 2 + d:3 + d] for d in dirs]

        qb = [q_ref[r, :].astype(BF16) for r in rows]
        k = [k_ref[r, :] * (M_DK ** -0.5) for r in rows]
        kb = [x.astype(BF16) for x in k]
        vb = [v_ref[r, :].astype(BF16) for r in rows]
        qk = [_dot_nt(qb[d], kb[d]) for d in dirs]
        logw = [jnp.where(masks[d], b_col[d] - b_row[d] + i_row[d], -jnp.inf) for d in dirs]
        m_loc = [jnp.max(x, axis=1, keepdims=True) for x in logw]
        m_chunk = [jnp.max(b_last[d] - b_row[d] + i_row[d], axis=1, keepdims=True) for d in dirs]
        s = [qk[d] * jnp.exp(logw[d] - m_loc[d]) for d in dirs]
        kw = [k[d] * jnp.exp(b_last[d] - b_col[d] + i_col[d] - m_chunk[d]) for d in dirs]
        a_num = [_dot(s[d].astype(BF16), vb[d]) for d in dirs]
        kw_t = [x.T.astype(BF16) for x in kw]
        a_den = [jnp.sum(x, axis=1, keepdims=True) for x in s]

        m_prev = [m_scr[d][0:1, 0:1] for d in dirs]
        c_prev = [c_scr[d][...] for d in dirs]
        n_prev = [n_scr[d][0:1, :] for d in dirs]
        q_c = [_dot(qb[d], c_prev[d].astype(BF16)) for d in dirs]
        u_c = [_dot(kw_t[d], vb[d]) for d in dirs]
        gg = [b_col[d] + m_prev[d] for d in dirs]
        m_tot = [jnp.maximum(m_loc[d], gg[d]) for d in dirs]
        e_intra = [jnp.exp(m_loc[d] - m_tot[d]) for d in dirs]
        e_inter = [jnp.exp(gg[d] - m_tot[d]) for d in dirs]
        q_n = [jnp.sum(qb[d].astype(F32) * n_prev[d].astype(BF16).astype(F32), axis=1, keepdims=True)
               for d in dirs]
        num = [e_intra[d] * a_num[d] + e_inter[d] * q_c[d] for d in dirs]
        den = [e_intra[d] * a_den[d] + e_inter[d] * q_n[d] for d in dirs]
        for d in dirs:
            hacc[d][rows[d], :] = num[d] / jnp.maximum(jnp.abs(den[d]), jnp.exp(-m_tot[d]))

        u_n = [jnp.sum(x, axis=0, keepdims=True) for x in kw]
        m_new = [jnp.maximum(b_last[d] + m_prev[d], m_chunk[d]) for d in dirs]
        a = [jnp.exp(b_last[d] + m_prev[d] - m_new[d]) for d in dirs]
        e = [jnp.exp(m_chunk[d] - m_new[d]) for d in dirs]
        for d in dirs:
            c_scr[d][...] = a[d] * c_prev[d] + e[d] * u_c[d]
            n_scr[d][0:1, :] = a[d] * n_prev[d] + e[d] * u_n[d]
            m_scr[d][0:1, :] = jnp.broadcast_to(m_new[d], (1, LANES))
        return carry

    def finish(c, carry):
        rows = pl.ds(pl.multiple_of(c * CHUNK, CHUNK), CHUNK)
        ht = hacc[0][rows, :] + hacc[1][rows, :]
        ms = jnp.mean(ht * ht, axis=1, keepdims=True)
        y = ht * lax.rsqrt(ms + EPS) * ng_ref[...] * _sigmoid(o_ref[rows, :])
        y_ref[rows, :] = y.astype(y_ref.dtype)
        return carry

    lax.fori_loop(0, nc, scan, 0, unroll=SCAN_UNROLL)
    lax.fori_loop(0, nc, finish, 0, unroll=SCAN_UNROLL)
    if emit_state:
        for d in range(2):
            cf_ref[d] = c_scr[d][...]
            nf_ref[d:d + 1, :] = n_scr[d][0:1, :]
            mf_ref[d:d + 1, :] = m_scr[d][0:1, :]


def _mlstm_call(proj, gates, ng, nb, seq, state=None, emit_state=False):
    t = proj.shape[0]
    nc = seq // CHUNK
    dk = M_DK
    cspec = lambda base: pl.BlockSpec((seq, dk), lambda b, h: (b, base // dk + h))
    in_specs = [cspec(OFF_MQ), cspec(OFF_MK), cspec(OFF_MV), cspec(OFF_MO),
                pl.BlockSpec((seq, LANES), lambda b, h: (b, h)),
                pl.BlockSpec((1, dk), lambda b, h: (0, h))]
    args = [proj, proj, proj, proj, gates, ng]
    if state is not None:
        in_specs += [pl.BlockSpec((None, 2, None, dk, dk), lambda b, h: (b, 0, h, 0, 0)),
                     pl.BlockSpec((None, None, 2, dk), lambda b, h: (b, h, 0, 0)),
                     pl.BlockSpec((None, None, 2, LANES), lambda b, h: (b, h, 0, 0))]
        args += list(state)
    out_specs = [pl.BlockSpec((seq, dk), lambda b, h: (b, h))]
    out_shape = [jax.ShapeDtypeStruct((t, M_HEADS * dk), BF16)]
    if emit_state:
        out_specs += [pl.BlockSpec((None, 2, None, dk, dk), lambda b, h: (b, 0, h, 0, 0)),
                      pl.BlockSpec((None, None, 2, dk), lambda b, h: (b, h, 0, 0)),
                      pl.BlockSpec((None, None, 2, LANES), lambda b, h: (b, h, 0, 0))]
        out_shape += [jax.ShapeDtypeStruct((nb, 2, M_HEADS, dk, dk), F32),
                      jax.ShapeDtypeStruct((nb, M_HEADS, 2, dk), F32),
                      jax.ShapeDtypeStruct((nb, M_HEADS, 2, LANES), F32)]
    return pl.pallas_call(
        functools.partial(_mlstm_kernel, nc=nc, has_state=state is not None, emit_state=emit_state),
        grid=(nb, M_HEADS),
        in_specs=in_specs, out_specs=out_specs, out_shape=out_shape,
        scratch_shapes=([pltpu.VMEM((seq, dk), F32)] * 2 + [pltpu.VMEM((dk, dk), F32)] * 2
                        + [pltpu.VMEM((8, dk), F32)] * 2 + [pltpu.VMEM((8, LANES), F32)] * 2),
        compiler_params=_cparams(2),
        name="mlstm",
    )(*args)


def _ret_kernel(*refs, nc, has_state, emit_state, rope):
    it = iter(refs)
    q_ref, k_ref, v_ref, g_ref, lg_ref, ng_ref = (next(it) for _ in range(6))
    if rope:
        cos_ref, sin_ref = next(it), next(it)
    if has_state:
        s0_ref = next(it)
    y_ref = next(it)
    if emit_state:
        sf_ref = next(it)
    oacc = (next(it), next(it))
    s_scr = (next(it), next(it))
    qs_scr, ks_scr, kt_scr = (next(it) for _ in range(3))

    row, col, lower, upper, _ = _chunk_masks()
    rel = (row - col).astype(F32)
    pos = row.astype(F32)
    lg = -jnp.exp(lg_ref[...])
    dirs = (0, 1)
    for d in dirs:
        if has_state:
            s_scr[d][...] = s0_ref[d]
        else:
            s_scr[d][...] = jnp.zeros_like(s_scr[d])

    lgd = [lg[d:d + 1, :] for d in dirs]
    decay = [jnp.where(lower, jnp.exp(rel * lgd[0]), 0.0), jnp.where(upper, jnp.exp(-rel * lgd[1]), 0.0)]
    w_q = [jnp.exp((pos + 1.0) * lgd[0]), jnp.exp((CHUNK - pos) * lgd[1])]
    w_k = [jnp.exp((CHUNK - 1.0 - pos) * lgd[0]), jnp.exp(pos * lgd[1])]
    chunk_decay = [jnp.exp(CHUNK * x) for x in lgd]

    def prep(c, carry):
        rows = pl.ds(pl.multiple_of(c * CHUNK, CHUNK), CHUNK)
        q = q_ref[rows, :]
        k = k_ref[rows, :] * (R_DK ** -0.5)
        if rope:
            cs, sn = cos_ref[rows, :], sin_ref[rows, :]
            q = q * cs + pltpu.roll(q, R_DK // 2, 1) * sn
            k = k * cs + pltpu.roll(k, R_DK // 2, 1) * sn
        qs_scr[rows, :] = q.astype(BF16)
        ks_scr[rows, :] = k.astype(BF16)
        kt_scr[:, rows] = k.T.astype(BF16)
        return carry

    def scan(i, carry):
        rows = [pl.ds(pl.multiple_of(c * CHUNK, CHUNK), CHUNK) for c in (i, nc - 1 - i)]
        qb = [qs_scr[r, :] for r in rows]
        kb = [ks_scr[r, :] for r in rows]
        kt = [kt_scr[:, r] for r in rows]
        v = [v_ref[r, :] for r in rows]
        vb = [x.astype(BF16) for x in v]
        vw = [(v[d] * w_k[d]).astype(BF16) for d in dirs]
        s_prev = [s_scr[d][...] for d in dirs]
        sc = [(_dot_nt(qb[d], kb[d]) * decay[d]).astype(BF16) for d in dirs]
        inter = [_dot(qb[d], s_prev[d].astype(BF16)) for d in dirs]
        u = [_dot(kt[d], vw[d]) for d in dirs]
        for d in dirs:
            oacc[d][rows[d], :] = _dot(sc[d], vb[d]) + w_q[d] * inter[d]
            s_scr[d][...] = chunk_decay[d] * s_prev[d] + u[d]
        return carry

    def finish(c, carry):
        rows = pl.ds(pl.multiple_of(c * CHUNK, CHUNK), CHUNK)
        ot = oacc[0][rows, :] + oacc[1][rows, :]
        ms = jnp.mean(ot * ot, axis=1, keepdims=True)
        y = ot * lax.rsqrt(ms + EPS) * ng_ref[...] * _silu(g_ref[rows, :])
        y_ref[rows, :] = y.astype(y_ref.dtype)
        return carry

    lax.fori_loop(0, nc, prep, 0, unroll=SCAN_UNROLL)
    lax.fori_loop(0, nc, scan, 0, unroll=SCAN_UNROLL)
    lax.fori_loop(0, nc, finish, 0, unroll=SCAN_UNROLL)
    if emit_state:
        for d in dirs:
            sf_ref[d] = s_scr[d][...]


def _ret_call(proj, lgr, ng, nb, seq, rope=None, state=None, emit_state=False):
    t = proj.shape[0]
    nc = seq // CHUNK
    dk = R_DK
    cspec = lambda base: pl.BlockSpec((seq, dk), lambda b, h: (b, base // dk + h))
    in_specs = [cspec(OFF_RQ), cspec(OFF_RK), cspec(OFF_RV), cspec(OFF_RG),
                pl.BlockSpec((None, 2, LANES), lambda b, h: (h, 0, 0)),
                pl.BlockSpec((1, dk), lambda b, h: (0, h))]
    args = [proj, proj, proj, proj, lgr, ng]
    if rope is not None:
        in_specs += [pl.BlockSpec((seq, dk), lambda b, h: (0, 0))] * 2
        args += list(rope)
    sspec = pl.BlockSpec((None, 2, None, dk, dk), lambda b, h: (b, 0, h, 0, 0))
    if state is not None:
        in_specs.append(sspec)
        args.append(state)
    out_specs = [pl.BlockSpec((seq, dk), lambda b, h: (b, h))]
    out_shape = [jax.ShapeDtypeStruct((t, R_HEADS * dk), BF16)]
    if emit_state:
        out_specs.append(sspec)
        out_shape.append(jax.ShapeDtypeStruct((nb, 2, R_HEADS, dk, dk), F32))
    return pl.pallas_call(
        functools.partial(_ret_kernel, nc=nc, has_state=state is not None, emit_state=emit_state,
                          rope=rope is not None),
        grid=(nb, R_HEADS),
        in_specs=in_specs, out_specs=out_specs, out_shape=out_shape,
        scratch_shapes=([pltpu.VMEM((seq, dk), F32)] * 2 + [pltpu.VMEM((dk, dk), F32)] * 2
                        + [pltpu.VMEM((seq, dk), BF16)] * 2 + [pltpu.VMEM((dk, seq), BF16)]),
        compiler_params=_cparams(2),
        name="retention",
    )(*args)


def _ssd_kernel(*refs, nc, has_state, emit_state):
    it = iter(refs)
    (x_ref, b_ref, c_ref, z_ref, g_ref, wx_ref, wb_ref, wc_ref, bx_ref, bb_ref, bc_ref,
     al_ref, sd_ref) = (next(it) for _ in range(13))
    if has_state:
        h0_ref = next(it)
    y_ref = next(it)
    if emit_state:
        hf_ref = next(it)
    xs_scr, bt_scr, bb_scr, cc_scr, ht_scr, yb_scr = (next(it) for _ in range(6))

    _, _, lower, upper, tri = _chunk_masks()
    rowi = lax.broadcasted_iota(jnp.int32, (CHUNK, 1), 0)
    er = lax.broadcasted_iota(jnp.int32, (LANES, S_GW), 0)
    ec = lax.broadcasted_iota(jnp.int32, (LANES, S_GW), 1) // S_P
    expand = [jnp.where(er == ec + d * S_HPG, 1.0, 0.0).astype(BF16) for d in range(2)]
    a_row = -jnp.exp(al_ref[...])

    def conv_silu(ref, w_ref, bias_ref, c, rows):
        off = c * CHUNK
        cur = ref[rows, :]
        prev8 = ref[pl.ds(pl.multiple_of(jnp.maximum(off - 8, 0), 8), 8), :]
        next8 = ref[pl.ds(pl.multiple_of(jnp.minimum(off + CHUNK, (nc - 1) * CHUNK + CHUNK - 8), 8), 8), :]
        has_prev = jnp.where(c > 0, 1.0, 0.0)
        has_next = jnp.where(c < nc - 1, 1.0, 0.0)
        p_last = prev8[7:8, :] * has_prev
        n0 = next8[0:1, :] * has_next
        n1 = next8[1:2, :] * has_next
        xm1 = jnp.where(rowi == 0, p_last, pltpu.roll(cur, 1, 0))
        xp1 = jnp.where(rowi == CHUNK - 1, n0, pltpu.roll(cur, CHUNK - 1, 0))
        xp2 = jnp.where(rowi == CHUNK - 2, n0, jnp.where(rowi == CHUNK - 1, n1, pltpu.roll(cur, CHUNK - 2, 0)))
        w = w_ref[...]
        y = w[0:1, :] * xm1 + w[1:2, :] * cur + w[2:3, :] * xp1 + w[3:4, :] * xp2 + bias_ref[...]
        return _silu(y)

    def prep(c, carry):
        off = pl.multiple_of(c * CHUNK, CHUNK)
        rows = pl.ds(off, CHUNK)
        xs_scr[rows, :] = conv_silu(x_ref, wx_ref, bx_ref, c, rows)
        bm = conv_silu(b_ref, wb_ref, bb_ref, c, rows)
        bb_scr[rows, :] = bm.astype(BF16)
        bt_scr[:, rows] = bm.T.astype(BF16)
        cc_scr[rows, :] = conv_silu(c_ref, wc_ref, bc_ref, c, rows).astype(BF16)
        return carry

    lax.fori_loop(0, nc, prep, 0, unroll=SCAN_UNROLL)

    for d in range(2):
        if has_state:
            ht_scr[d] = h0_ref[d].T
        else:
            ht_scr[d] = jnp.zeros((S_N, S_GW), F32)

    dirs = (0, 1)
    masks = (lower, upper)
    y_out = (y_ref, yb_scr)

    def scan(i, carry):
        rows = [pl.ds(pl.multiple_of(c * CHUNK, CHUNK), CHUNK) for c in (i, nc - 1 - i)]
        dt = [g_ref[r, :] for r in rows]
        a = [x * a_row for x in dt]
        cs = [_cumsum_rows(x, tri) for x in a]
        tot = [x[CHUNK - 1:CHUNK, :] for x in cs]
        bsum = [cs[0], tot[1] - cs[1] + a[1]]
        bsum_t = [x.T for x in bsum]
        dt_t = [x.T for x in dt]
        xs = [xs_scr[r, :] for r in rows]
        xb = [x.astype(BF16) for x in xs]
        bmb = [bb_scr[r, :] for r in rows]
        cmb = [cc_scr[r, :] for r in rows]
        btb = [bt_scr[:, r] for r in rows]
        cb = [_dot_nt(cmb[d], bmb[d]) for d in dirs]
        ht = [ht_scr[d] for d in dirs]
        inter = [_dot(cmb[d], ht[d].astype(BF16)) for d in dirs]
        ex = [_expand(jnp.exp(bsum[d]), expand[d]) for d in dirs]
        wexp = [_expand(jnp.exp(tot[d] - bsum[d]) * dt[d], expand[d]) for d in dirs]
        cd = [_expand(jnp.broadcast_to(jnp.exp(tot[d]), (8, LANES)), expand[d])[0:1, :] for d in dirs]
        u_t = [_dot(btb[d], (xs[d] * wexp[d]).astype(BF16)) for d in dirs]
        ys = [[], []]
        for k in range(S_HPG):
            for d in dirs:
                ln = d * S_HPG + k
                decay = jnp.exp(jnp.where(masks[d], bsum[d][:, ln:ln + 1] - bsum_t[d][ln:ln + 1, :], -jnp.inf))
                m = (cb[d] * decay * dt_t[d][ln:ln + 1, :]).astype(BF16)
                ys[d].append(_dot(m, xb[d][:, k * S_P:(k + 1) * S_P]))
        for d in dirs:
            y_out[d][rows[d], :] = jnp.concatenate(ys[d], axis=1) + ex[d] * inter[d]
            ht_scr[d] = ht[d] * cd[d] + u_t[d]
        return carry

    def finish(c, carry):
        rows = pl.ds(pl.multiple_of(c * CHUNK, CHUNK), CHUNK)
        yt = y_ref[rows, :] + yb_scr[rows, :] + sd_ref[...] * xs_scr[rows, :]
        y_ref[rows, :] = yt * _silu(z_ref[rows, :])
        return carry

    lax.fori_loop(0, nc, scan, 0, unroll=SCAN_UNROLL)
    lax.fori_loop(0, nc, finish, 0, unroll=SCAN_UNROLL)
    if emit_state:
        for d in range(2):
            hf_ref[d] = ht_scr[d].T


def _ssd_call(proj, gates, conv_w, conv_b, alog, sd, nb, seq, state=None, emit_state=False):
    t = proj.shape[0]
    nc = seq // CHUNK
    gw = S_GW
    n = S_N
    xw = S_HEADS * S_P
    in_specs = [pl.BlockSpec((seq, gw), lambda b, g: (b, OFF_SX // gw + g)),
                pl.BlockSpec((seq, n), lambda b, g: (b, OFF_SB // n + g)),
                pl.BlockSpec((seq, n), lambda b, g: (b, OFF_SC // n + g)),
                pl.BlockSpec((seq, gw), lambda b, g: (b, OFF_SZ // gw + g)),
                pl.BlockSpec((seq, LANES), lambda b, g: (b, M_HEADS + g)),
                pl.BlockSpec((4, gw), lambda b, g: (0, g)),
                pl.BlockSpec((4, n), lambda b, g: (0, xw // n + g)),
                pl.BlockSpec((4, n), lambda b, g: (0, xw // n + S_GROUPS + g)),
                pl.BlockSpec((1, gw), lambda b, g: (0, g)),
                pl.BlockSpec((1, n), lambda b, g: (0, xw // n + g)),
                pl.BlockSpec((1, n), lambda b, g: (0, xw // n + S_GROUPS + g)),
                pl.BlockSpec((None, 1, LANES), lambda b, g: (g, 0, 0)),
                pl.BlockSpec((1, gw), lambda b, g: (0, g))]
    args = [proj, proj, proj, proj, gates, conv_w, conv_w, conv_w, conv_b, conv_b, conv_b, alog, sd]
    sspec = pl.BlockSpec((None, 2, None, gw, n), lambda b, g: (b, 0, g, 0, 0))
    if state is not None:
        in_specs.append(sspec)
        args.append(state)
    out_specs = [pl.BlockSpec((seq, gw), lambda b, g: (b, g))]
    out_shape = [jax.ShapeDtypeStruct((t, xw), F32)]
    if emit_state:
        out_specs.append(sspec)
        out_shape.append(jax.ShapeDtypeStruct((nb, 2, S_GROUPS, gw, n), F32))
    return pl.pallas_call(
        functools.partial(_ssd_kernel, nc=nc, has_state=state is not None, emit_state=emit_state),
        grid=(nb, S_GROUPS),
        in_specs=in_specs, out_specs=out_specs, out_shape=out_shape,
        scratch_shapes=[pltpu.VMEM((seq, gw), F32), pltpu.VMEM((n, seq), BF16),
                        pltpu.VMEM((seq, n), BF16), pltpu.VMEM((seq, n), BF16),
                        pltpu.VMEM((2, n, gw), F32), pltpu.VMEM((seq, gw), F32)],
        compiler_params=_cparams(2),
        name="ssd",
    )(*args)


_IN_SIZES = (1024, 1024, 1024, 1024, 8, 8, 1024, 1024, 1024, 1024, 1024, 1536, 32, 6144)
_IN_OFFS = np.concatenate([[0], np.cumsum(_IN_SIZES)])


def _gate_lane_index():
    idx = -np.ones((N_GATES,), np.int64)
    for h in range(M_HEADS):
        base = h * LANES
        idx[base + 0] = 0 * M_HEADS + h
        idx[base + 1] = 1 * M_HEADS + h
        idx[base + 2] = 2 * M_HEADS + 0 * M_HEADS + h
        idx[base + 3] = 2 * M_HEADS + 1 * M_HEADS + h
    for g in range(S_GROUPS):
        base = (M_HEADS + g) * LANES
        for d in range(2):
            for k in range(S_HPG):
                idx[base + d * S_HPG + k] = 4 * M_HEADS + d * S_HEADS + g * S_HPG + k
    return idx


def _place(vals, idx):
    taken = jnp.take(vals, jnp.asarray(np.maximum(idx, 0)), axis=-1)
    return jnp.where(jnp.asarray(idx >= 0), taken, 0.0)


def _prep_layer(l, w_in, m_igate_b, m_fgate_b, s_dt_bias, s_a_log, r_decay, s_d):
    w = w_in[l]
    o = _IN_OFFS
    w_big = jnp.concatenate([w[:, o[0]:o[4]], w[:, o[6]:o[12]], w[:, o[13]:o[14]]], axis=1).astype(BF16)
    w_small = jnp.concatenate([w[:, o[4]:o[6]], w[:, o[12]:o[13]]], axis=1)
    gidx = _gate_lane_index()
    w_g = _place(w_small, gidx).astype(BF16)
    b_small = jnp.concatenate([m_igate_b[l].reshape(-1), m_fgate_b[l].reshape(-1), s_dt_bias[l].reshape(-1)])
    b_g = _place(b_small, gidx).reshape(1, N_GATES)
    aidx = -np.ones((S_GROUPS, LANES), np.int64)
    for g in range(S_GROUPS):
        for d in range(2):
            for k in range(S_HPG):
                aidx[g, d * S_HPG + k] = d * S_HEADS + g * S_HPG + k
    alog = jnp.stack([_place(s_a_log[l].reshape(-1), aidx[g]) for g in range(S_GROUPS)]).reshape(S_GROUPS, 1, LANES)
    lgr = jnp.broadcast_to(r_decay[l].T[:, :, None], (R_HEADS, 2, LANES))
    sd = jnp.repeat(s_d[l], S_P).reshape(1, S_HEADS * S_P)
    return w_big, w_g, b_g, alog, lgr, sd


def _rope_tables(seq):
    n_rows = seq // GRID_W
    rows = jnp.repeat(jnp.arange(n_rows, dtype=F32), GRID_W)
    cols = jnp.tile(jnp.arange(GRID_W, dtype=F32), n_rows)
    inv = ROPE_BASE ** (-jnp.arange(ROPE_FREQS, dtype=F32) / ROPE_FREQS)
    ang = jnp.concatenate([rows[:, None] * inv, cols[:, None] * inv], -1)
    cos, sin = jnp.cos(ang), jnp.sin(ang)
    return jnp.concatenate([cos, cos], -1), jnp.concatenate([-sin, sin], -1)


def _layer(x, mods, rows_per_mod, nb, seq, lw, rope, state, emit_state, final_g):
    sh_a, sc_a, g_a, sh_f, sc_f, g_f = mods
    tm_l, tm_s = min(1024, rows_per_mod), min(512, rows_per_mod)
    proj, gates = _norm_mm_call(x, lw["norm_mix_g"], sc_a, sh_a, rows_per_mod, lw["w_big"], tm_l, 768,
                                F32, False, gates=(lw["w_g"], lw["b_g"]), name="in_proj")
    st_m = st_r = st_s = None
    if state is not None:
        st_m, st_r, st_s = state
    om = _mlstm_call(proj, gates, lw["m_norm_g"], nb, seq, state=st_m, emit_state=emit_state)
    orr = _ret_call(proj, lw["lgr"], lw["r_norm_g"], nb, seq, rope=rope, state=st_r, emit_state=emit_state)
    os_ = _ssd_call(proj, gates, lw["s_conv_w"], lw["s_conv_b"], lw["alog"], lw["sd"], nb, seq,
                    state=st_s, emit_state=emit_state)
    merged = _merge_call(om[0], orr[0], os_[0], lw["s_norm_g"], lw["w_br_m"], lw["w_br_r"], lw["w_br_s"],
                         proj, tm_l, 512)
    x = _mm_resid_call(merged, lw["w_out"], x, g_a, rows_per_mod, tm_s, D_MODEL, name="out_proj")
    hid = _norm_mm_call(x, lw["norm_mlp_g"], sc_f, sh_f, rows_per_mod, lw["w_ff1"], tm_l, 1024,
                        BF16, True, name="mlp_up")
    x = _mm_resid_call(hid, lw["w_ff2"], x, g_f, rows_per_mod, tm_s, 1024, final_g=final_g, name="mlp_down")
    new_state = (om[1:], orr[1:], os_[1:]) if emit_state else None
    return x, new_state


def kernel(x_prompt, x_sample, c, state_mlstm_C, state_mlstm_n, state_mlstm_m, state_ret, state_ssd, c_ctx, w_mod, b_mod, norm_mix_g, norm_mlp_g, w_in, m_igate_b, m_fgate_b, m_norm_g, r_decay, r_norm_g, s_conv_w, s_conv_b, s_dt_bias, s_a_log, s_d, s_norm_g, w_br_m, w_br_r, w_br_s, w_out, w_ff1, w_ff2, final_norm_g):
    bp, lp, d = x_prompt.shape
    bs, ls, _ = x_sample.shape
    xp = x_prompt.reshape(bp * lp, d)
    xs = x_sample.reshape(bs * ls, d)

    c_rows = jnp.zeros((8, d), F32).at[:bs].set(c).at[bs].set(c_ctx)
    mod = _mod_call(c_rows, w_mod, b_mod)
    rope = _rope_tables(ls)
    final_g = final_norm_g.reshape(1, d)

    st_c, st_n, st_m, st_r, st_s = [], [], [], [], []
    for l in range(DEPTH):
        w_big, w_g, b_g, alog, lgr, sd = _prep_layer(l, w_in, m_igate_b, m_fgate_b, s_dt_bias, s_a_log,
                                                     r_decay, s_d)
        lw = dict(w_big=w_big, w_g=w_g, b_g=b_g, alog=alog, lgr=lgr, sd=sd,
                  norm_mix_g=norm_mix_g[l].reshape(1, d), norm_mlp_g=norm_mlp_g[l].reshape(1, d),
                  m_norm_g=m_norm_g[l].reshape(1, -1), r_norm_g=r_norm_g[l].reshape(1, -1),
                  s_norm_g=s_norm_g[l].reshape(1, -1),
                  s_conv_w=s_conv_w[l], s_conv_b=s_conv_b[l].reshape(1, -1),
                  w_br_m=w_br_m[l].astype(BF16), w_br_r=w_br_r[l].astype(BF16), w_br_s=w_br_s[l].astype(BF16),
                  w_out=w_out[l].astype(BF16), w_ff1=w_ff1[l].astype(BF16), w_ff2=w_ff2[l].astype(BF16))
        parts = mod[l].reshape(8, 6, 1, d)
        mods_ctx = tuple(parts[bs:bs + 1, i] for i in range(6))
        mods_lat = tuple(parts[:bs, i] for i in range(6))
        fg = final_g if l == DEPTH - 1 else None

        xp, st = _layer(xp, mods_ctx, bp * lp, bp, lp, lw, None, None, True, fg)
        (cf, nf, mf), (rf,), (hf,) = st
        st_c.append(cf)
        st_n.append(jnp.transpose(nf, (0, 2, 1, 3)))
        st_m.append(jnp.transpose(mf[..., 0], (0, 2, 1)))
        st_r.append(rf)
        st_s.append(hf.reshape(bp, 2, S_HEADS, S_P, S_N))

        cache = (
            (state_mlstm_C[:, l],
             jnp.transpose(state_mlstm_n[:, l], (0, 2, 1, 3)),
             jnp.broadcast_to(jnp.transpose(state_mlstm_m[:, l], (0, 2, 1))[..., None], (bs, M_HEADS, 2, LANES))),
            state_ret[:, l],
            state_ssd[:, l].reshape(bs, 2, S_GROUPS, S_GW, S_N),
        )
        xs, _ = _layer(xs, mods_lat, ls, bs, ls, lw, rope, cache, False, fg)

    return (xp.reshape(bp, lp, d), xs.reshape(bs, ls, d),
            jnp.stack(st_c, 1), jnp.stack(st_n, 1), jnp.stack(st_m, 1), jnp.stack(st_r, 1), jnp.stack(st_s, 1))
```

```python
import functools

import numpy as np
import jax
import jax.numpy as jnp
from jax import lax
from jax.experimental import pallas as pl
from jax.experimental.pallas import tpu as pltpu

F32 = jnp.float32
BF16 = jnp.bfloat16

D_MODEL = 2048
DEPTH = 2
CHUNK = 128
EPS = 1e-6
M_HEADS, M_DK = 4, 256
R_HEADS, R_DK = 8, 128
S_HEADS, S_P, S_GROUPS, S_N = 16, 64, 2, 128
S_HPG = S_HEADS // S_GROUPS
S_GW = S_HPG * S_P
GRID_W = 64
ROPE_BASE = 10000.0
ROPE_FREQS = R_DK // 4
D_FF = 4 * D_MODEL
LANES = 128
VMEM_LIMIT = 56 * 1024 * 1024
SCAN_UNROLL = 2

OFF_MQ, OFF_MK, OFF_MV, OFF_MO = 0, 1024, 2048, 3072
OFF_RQ, OFF_RK, OFF_RV, OFF_RG = 4096, 5120, 6144, 7168
OFF_SZ, OFF_SX, OFF_SB, OFF_SC = 8192, 9216, 10240, 10496
N_MIX = 10752
N_GATES = (M_HEADS + S_GROUPS) * LANES
GATE_LANE_OFFSETS = tuple(4 * h for h in range(M_HEADS)) + tuple(4 * M_HEADS + 2 * S_HPG * g for g in range(S_GROUPS))


def _cparams(n_axes):
    return pltpu.CompilerParams(dimension_semantics=("arbitrary",) * n_axes,
                                vmem_limit_bytes=VMEM_LIMIT)


def _dot(a, b):
    return jnp.dot(a, b, preferred_element_type=F32)


def _dot_nt(a, b):
    return lax.dot_general(a, b, (((1,), (1,)), ((), ())), preferred_element_type=F32)


def _split2(x):
    hi = x.astype(BF16)
    lo = (x - hi.astype(F32)).astype(BF16)
    return hi, lo


def _cumsum_rows(x, tri):
    hi, lo = _split2(x)
    return _dot(tri, hi) + _dot(tri, lo)


def _expand(x, e, exact=True):
    if not exact:
        return _dot(x.astype(BF16), e)
    hi, lo = _split2(x)
    return _dot(hi, e) + _dot(lo, e)


def _softplus(x):
    return jnp.maximum(x, 0.0) + jnp.log1p(jnp.exp(-jnp.abs(x)))


def _sigmoid(x):
    return 1.0 / (1.0 + jnp.exp(-x))


def _silu(x):
    return x * _sigmoid(x)


def _mod_kernel(c_ref, w_ref, b_ref, o_ref):
    a = _silu(c_ref[...]).astype(BF16)
    o_ref[...] = _dot(a, w_ref[...].astype(BF16)) + b_ref[...]


def _mod_call(c_rows, w_mod, b_mod):
    tn = 1024
    n = w_mod.shape[-1]
    return pl.pallas_call(
        _mod_kernel,
        grid=(DEPTH, n // tn),
        in_specs=[pl.BlockSpec((8, D_MODEL), lambda l, j: (0, 0)),
                  pl.BlockSpec((None, D_MODEL, tn), lambda l, j: (l, 0, j)),
                  pl.BlockSpec((None, 1, tn), lambda l, j: (l, 0, j))],
        out_specs=pl.BlockSpec((None, 8, tn), lambda l, j: (l, 0, j)),
        out_shape=jax.ShapeDtypeStruct((DEPTH, 8, n), F32),
        compiler_params=_cparams(2),
        name="mod_proj",
    )(c_rows, w_mod, b_mod.reshape(DEPTH, 1, n))


def _norm_mm_kernel(*refs, with_gates, relu2):
    if with_gates:
        x_ref, g_ref, sc_ref, sh_ref, w_ref, wg_ref, bg_ref, o_ref, og_ref, h_scr = refs
    else:
        x_ref, g_ref, sc_ref, sh_ref, w_ref, o_ref, h_scr = refs

    @pl.when(pl.program_id(1) == 0)
    def _():
        x = x_ref[...]
        ms = jnp.mean(x * x, axis=-1, keepdims=True)
        y = x * lax.rsqrt(ms + EPS) * g_ref[...]
        hb = (y * (1.0 + sc_ref[...]) + sh_ref[...]).astype(BF16)
        h_scr[...] = hb
        if with_gates:
            pre = _dot(hb, wg_ref[...]) + bg_ref[...]
            lane = lax.broadcasted_iota(jnp.int32, pre.shape, 1)
            act = jnp.where(lane < 4 * M_HEADS,
                            jnp.where(lane % 4 < 2, pre, -_softplus(-pre)),
                            _softplus(pre))
            for blk, off in enumerate(GATE_LANE_OFFSETS):
                own = 4 if blk < M_HEADS else 2 * S_HPG
                v = act if off == 0 else pltpu.roll(act, LANES - off, 1)
                og_ref[:, blk * LANES:(blk + 1) * LANES] = jnp.where(lane < own, v, 0.0)

    acc = _dot(h_scr[...], w_ref[...])
    if relu2:
        r = jnp.maximum(acc, 0.0)
        acc = r * r
    o_ref[...] = acc.astype(o_ref.dtype)


def _norm_mm_call(x, g, sc, sh, rows_per_mod, w, layer, tm, tn, out_dtype, relu2, gates=None, name="norm_mm"):
    t, d = x.shape
    n = w.shape[-1]
    mod_idx = lambda i, j: ((i * tm) // rows_per_mod, 0, 0)
    in_specs = [pl.BlockSpec((tm, d), lambda i, j: (i, 0)),
                pl.BlockSpec((None, 1, d), lambda i, j: (layer, 0, 0)),
                pl.BlockSpec((None, 1, d), mod_idx),
                pl.BlockSpec((None, 1, d), mod_idx),
                pl.BlockSpec((None, d, tn), lambda i, j: (layer, 0, j))]
    args = [x, g, sc, sh, w]
    out_specs = [pl.BlockSpec((tm, tn), lambda i, j: (i, j))]
    out_shape = [jax.ShapeDtypeStruct((t, n), out_dtype)]
    if gates is not None:
        wg, bg = gates
        in_specs += [pl.BlockSpec((None, d, LANES), lambda i, j: (layer, 0, 0)),
                     pl.BlockSpec((None, 1, LANES), lambda i, j: (layer, 0, 0))]
        args += [wg, bg]
        out_specs.append(pl.BlockSpec((tm, N_GATES), lambda i, j: (i, 0)))
        out_shape.append(jax.ShapeDtypeStruct((t, N_GATES), F32))
    return pl.pallas_call(
        functools.partial(_norm_mm_kernel, with_gates=gates is not None, relu2=relu2),
        grid=(t // tm, n // tn),
        in_specs=in_specs, out_specs=out_specs, out_shape=out_shape,
        scratch_shapes=[pltpu.VMEM((tm, d), BF16)],
        compiler_params=_cparams(2),
        name=name,
    )(*args)


def _mm_resid_kernel(*refs, nj, tn, final_norm):
    if final_norm:
        a_ref, w_ref, r_ref, gate_ref, fg_ref, o_ref = refs
    else:
        a_ref, w_ref, r_ref, gate_ref, o_ref = refs
    j = pl.program_id(1)
    cols = pl.ds(pl.multiple_of(j * tn, tn), tn)
    o_ref[:, cols] = r_ref[...] + gate_ref[...] * _dot(a_ref[...], w_ref[...])
    if final_norm:
        @pl.when(j == nj - 1)
        def _():
            xn = o_ref[...]
            ms = jnp.mean(xn * xn, axis=-1, keepdims=True)
            o_ref[...] = xn * lax.rsqrt(ms + EPS) * fg_ref[...]


def _mm_resid_call(a, w, layer, resid, gate, rows_per_mod, tm, tn, final_g=None, name="mm_resid"):
    t, kdim = a.shape
    n = w.shape[-1]
    nj = n // tn
    mod_idx = lambda i, j: ((i * tm) // rows_per_mod, 0, j)
    in_specs = [pl.BlockSpec((tm, kdim), lambda i, j: (i, 0)),
                pl.BlockSpec((None, kdim, tn), lambda i, j: (layer, 0, j)),
                pl.BlockSpec((tm, tn), lambda i, j: (i, j)),
                pl.BlockSpec((None, 1, tn), mod_idx)]
    args = [a, w, resid, gate]
    if final_g is not None:
        in_specs.append(pl.BlockSpec((1, n), lambda i, j: (0, 0)))
        args.append(final_g)
    return pl.pallas_call(
        functools.partial(_mm_resid_kernel, nj=nj, tn=tn, final_norm=final_g is not None),
        grid=(t // tm, nj),
        in_specs=in_specs,
        out_specs=pl.BlockSpec((tm, n), lambda i, j: (i, 0)),
        out_shape=jax.ShapeDtypeStruct((t, n), F32),
        compiler_params=_cparams(2),
        name=name,
    )(*args)


def _merge_kernel(ym_ref, yr_ref, ts_ref, sg_ref, wm_ref, wr_ref, ws_ref, g0_ref, g1_ref, g2_ref,
                  o_ref, ys_scr):
    @pl.when(pl.program_id(1) == 0)
    def _():
        t = ts_ref[...]
        ms = jnp.mean(t * t, axis=-1, keepdims=True)
        ys_scr[...] = (t * lax.rsqrt(ms + EPS) * sg_ref[...]).astype(BF16)

    acc = _sigmoid(g0_ref[...]) * _dot(ym_ref[...], wm_ref[...])
    acc += _sigmoid(g1_ref[...]) * _dot(yr_ref[...], wr_ref[...])
    acc += _sigmoid(g2_ref[...]) * _dot(ys_scr[...], ws_ref[...])
    o_ref[...] = acc.astype(o_ref.dtype)


def _merge_call(ym, yr, ts, sg, wm, wr, ws, layer, proj, tm, tn):
    t, w = ym.shape
    d = wm.shape[-1]
    gl_blk = N_MIX // tn
    per_br = d // tn
    yspec = pl.BlockSpec((tm, w), lambda i, j: (i, 0))
    wspec = pl.BlockSpec((None, w, tn), lambda i, j: (layer, 0, j))
    gspec = lambda br: pl.BlockSpec((tm, tn), lambda i, j: (i, gl_blk + br * per_br + j))
    return pl.pallas_call(
        _merge_kernel,
        grid=(t // tm, d // tn),
        in_specs=[yspec, yspec, yspec, pl.BlockSpec((1, w), lambda i, j: (0, 0)),
                  wspec, wspec, wspec, gspec(0), gspec(1), gspec(2)],
        out_specs=pl.BlockSpec((tm, tn), lambda i, j: (i, j)),
        out_shape=jax.ShapeDtypeStruct((t, d), BF16),
        scratch_shapes=[pltpu.VMEM((tm, w), BF16)],
        compiler_params=_cparams(2),
        name="merge",
    )(ym, yr, ts, sg, wm, wr, ws, proj, proj, proj)


def _chunk_masks():
    row = lax.broadcasted_iota(jnp.int32, (CHUNK, CHUNK), 0)
    col = lax.broadcasted_iota(jnp.int32, (CHUNK, CHUNK), 1)
    lower = col <= row
    upper = col >= row
    tri = jnp.where(lower, 1.0, 0.0).astype(BF16)
    return row, col, lower, upper, tri


def _mlstm_kernel(*refs, nc, has_state, emit_state, alias_in):
    it = iter(refs)
    q_ref, k_ref, v_ref, o_ref, g_ref, ng_ref = (next(it) for _ in range(6))
    if has_state:
        c0_ref, n0_ref, m0_ref = (next(it) for _ in range(3))
    if alias_in:
        next(it)
    y_ref = next(it)
    if emit_state:
        cf_ref, nf_ref, mf_ref = (next(it) for _ in range(3))
    hacc = (next(it), next(it))
    c_scr = (next(it), next(it))
    n_scr = (next(it), next(it))
    m_scr = (next(it), next(it))

    _, _, lower, upper, tri = _chunk_masks()
    ones_bf = jnp.ones((CHUNK, LANES), BF16)
    for d in range(2):
        if has_state:
            c_scr[d][...] = c0_ref[d]
            n_scr[d][0:1, :] = n0_ref[d:d + 1, :]
            m_scr[d][0:1, :] = m0_ref[d:d + 1, :]
        else:
            c_scr[d][...] = jnp.zeros_like(c_scr[d])
            n_scr[d][...] = jnp.zeros_like(n_scr[d])
            m_scr[d][...] = jnp.zeros_like(m_scr[d])

    dirs = (0, 1)
    masks = (lower, upper)

    def scan(i, carry):
        cidx = (i, nc - 1 - i)
        rows = [pl.ds(pl.multiple_of(c * CHUNK, CHUNK), CHUNK) for c in cidx]
        g = [g_ref[r, :] for r in rows]
        cs = [_cumsum_rows(x, tri) for x in g]
        tot = [x[CHUNK - 1:CHUNK, :] for x in cs]
        bsum = [cs[0], tot[1] - cs[1] + g[1]]
        bsum_t = [x.T for x in bsum]
        g_t = [x.T for x in g]
        b_col = [bsum[d][:, 2 + d:3 + d] for d in dirs]
        b_row = [bsum_t[d][2 + d:3 + d, :] for d in dirs]
        i_col = [g[d][:, d:d + 1] for d in dirs]
        i_row = [g_t[d][d:d + 1, :] for d in dirs]
        b_last = [tot[d][:, 2 + d:3 + d] for d in dirs]

        qb = [q_ref[r, :].astype(BF16) for r in rows]
        k = [k_ref[r, :] * (M_DK ** -0.5) for r in rows]
        kb = [x.astype(BF16) for x in k]
        vb = [v_ref[r, :].astype(BF16) for r in rows]
        qk = [_dot_nt(qb[d], kb[d]) for d in dirs]
        logw = [jnp.where(masks[d], b_col[d] - b_row[d] + i_row[d], -jnp.inf) for d in dirs]
        m_loc = [jnp.max(x, axis=1, keepdims=True) for x in logw]
        m_chunk = [jnp.max(b_last[d] - b_row[d] + i_row[d], axis=1, keepdims=True) for d in dirs]
        s = [qk[d] * jnp.exp(logw[d] - m_loc[d]) for d in dirs]
        kw = [k[d] * jnp.exp(b_last[d] - b_col[d] + i_col[d] - m_chunk[d]) for d in dirs]
        a_num = [_dot(s[d].astype(BF16), vb[d]) for d in dirs]
        kw_t = [x.T.astype(BF16) for x in kw]
        a_den = [_dot(s[d].astype(BF16), ones_bf) for d in dirs]

        m_prev = [m_scr[d][0:1, 0:1] for d in dirs]
        c_prev = [c_scr[d][...] for d in dirs]
        n_prev = [n_scr[d][0:1, :] for d in dirs]
        q_c = [_dot(qb[d], c_prev[d].astype(BF16)) for d in dirs]
        u_c = [_dot(kw_t[d], vb[d]) for d in dirs]
        gg = [b_col[d] + m_prev[d] for d in dirs]
        m_tot = [jnp.maximum(m_loc[d], gg[d]) for d in dirs]
        e_intra = [jnp.exp(m_loc[d] - m_tot[d]) for d in dirs]
        e_inter = [jnp.exp(gg[d] - m_tot[d]) for d in dirs]
        q_n = [_dot_nt(qb[d], jnp.broadcast_to(n_prev[d].astype(BF16), (LANES, M_DK))) for d in dirs]
        num = [e_intra[d] * a_num[d] + e_inter[d] * q_c[d] for d in dirs]
        den = [e_intra[d] * a_den[d] + e_inter[d] * q_n[d] for d in dirs]
        for d in dirs:
            inv = 1.0 / jnp.maximum(jnp.abs(den[d]), jnp.exp(-m_tot[d]))
            hacc[d][rows[d], :] = num[d] * jnp.concatenate([inv, inv], axis=1)

        u_n = [jnp.sum(x, axis=0, keepdims=True) for x in kw]
        m_new = [jnp.maximum(b_last[d] + m_prev[d], m_chunk[d]) for d in dirs]
        a = [jnp.exp(b_last[d] + m_prev[d] - m_new[d]) for d in dirs]
        e = [jnp.exp(m_chunk[d] - m_new[d]) for d in dirs]
        for d in dirs:
            c_scr[d][...] = a[d] * c_prev[d] + e[d] * u_c[d]
            n_scr[d][0:1, :] = a[d] * n_prev[d] + e[d] * u_n[d]
            m_scr[d][0:1, :] = jnp.broadcast_to(m_new[d], (1, LANES))
        return carry

    def finish(c, carry):
        rows = pl.ds(pl.multiple_of(c * CHUNK, CHUNK), CHUNK)
        ht = hacc[0][rows, :] + hacc[1][rows, :]
        ms = jnp.mean(ht * ht, axis=1, keepdims=True)
        y = ht * lax.rsqrt(ms + EPS) * ng_ref[...] * _sigmoid(o_ref[rows, :])
        y_ref[rows, :] = y.astype(y_ref.dtype)
        return carry

    lax.fori_loop(0, nc, scan, 0, unroll=SCAN_UNROLL)
    lax.fori_loop(0, nc, finish, 0, unroll=SCAN_UNROLL)
    if emit_state:
        for d in range(2):
            cf_ref[d] = c_scr[d][...]
            nf_ref[d:d + 1, :] = n_scr[d][0:1, :]
            mf_ref[d:d + 1, :] = m_scr[d][0:1, :]


def _mlstm_call(proj, gates, ng, nb, seq, state=None, emit=None):
    emit_state = emit is not None
    t = proj.shape[0]
    nc = seq // CHUNK
    dk = M_DK
    cspec = lambda base: pl.BlockSpec((seq, dk), lambda b, h: (b, base // dk + h))
    in_specs = [cspec(OFF_MQ), cspec(OFF_MK), cspec(OFF_MV), cspec(OFF_MO),
                pl.BlockSpec((seq, LANES), lambda b, h: (b, h)),
                pl.BlockSpec((1, dk), lambda b, h: (0, h))]
    args = [proj, proj, proj, proj, gates, ng]
    if state is not None:
        sl = state[0]
        in_specs += [pl.BlockSpec((None, None, 2, None, dk, dk), lambda b, h: (b, sl, 0, h, 0, 0)),
                     pl.BlockSpec((None, None, 2, dk), lambda b, h: (b, h, 0, 0)),
                     pl.BlockSpec((None, None, 2, LANES), lambda b, h: (b, h, 0, 0))]
        args += list(state[1:])
    out_specs = [pl.BlockSpec((seq, dk), lambda b, h: (b, h))]
    out_shape = [jax.ShapeDtypeStruct((t, M_HEADS * dk), BF16)]
    aliases = {}
    if emit_state:
        layer, prev = emit
        out_specs += [pl.BlockSpec((None, None, 2, None, dk, dk), lambda b, h: (b, layer, 0, h, 0, 0)),
                      pl.BlockSpec((None, None, 2, dk), lambda b, h: (b, h, 0, 0)),
                      pl.BlockSpec((None, None, 2, LANES), lambda b, h: (b, h, 0, 0))]
        out_shape += [jax.ShapeDtypeStruct((nb, DEPTH, 2, M_HEADS, dk, dk), F32),
                      jax.ShapeDtypeStruct((nb, M_HEADS, 2, dk), F32),
                      jax.ShapeDtypeStruct((nb, M_HEADS, 2, LANES), F32)]
        if prev is not None:
            in_specs.append(pl.BlockSpec(memory_space=pl.ANY))
            args.append(prev)
            aliases = {len(args) - 1: 1}
    return pl.pallas_call(
        functools.partial(_mlstm_kernel, nc=nc, has_state=state is not None, emit_state=emit_state,
                          alias_in=bool(aliases)),
        grid=(nb, M_HEADS),
        in_specs=in_specs, out_specs=out_specs, out_shape=out_shape, input_output_aliases=aliases,
        scratch_shapes=([pltpu.VMEM((seq, dk), F32)] * 2 + [pltpu.VMEM((dk, dk), F32)] * 2
                        + [pltpu.VMEM((8, dk), F32)] * 2 + [pltpu.VMEM((8, LANES), F32)] * 2),
        compiler_params=_cparams(2),
        name="mlstm",
    )(*args)


def _ret_kernel(*refs, nc, has_state, emit_state, rope, alias_in):
    it = iter(refs)
    q_ref, k_ref, v_ref, g_ref, lg_ref, ng_ref = (next(it) for _ in range(6))
    if rope:
        cos_ref, sin_ref = next(it), next(it)
    if has_state:
        s0_ref = next(it)
    if alias_in:
        next(it)
    y_ref = next(it)
    if emit_state:
        sf_ref = next(it)
    oacc = (next(it), next(it))
    s_scr = (next(it), next(it))
    qs_scr, ks_scr, kt_scr = (next(it) for _ in range(3))

    row, col, lower, upper, _ = _chunk_masks()
    rel = (row - col).astype(F32)
    pos = row.astype(F32)
    lg = -jnp.exp(lg_ref[...])
    dirs = (0, 1)
    for d in dirs:
        if has_state:
            s_scr[d][...] = s0_ref[d]
        else:
            s_scr[d][...] = jnp.zeros_like(s_scr[d])

    lgd = [lg[d:d + 1, :] for d in dirs]
    decay = [jnp.where(lower, jnp.exp(rel * lgd[0]), 0.0), jnp.where(upper, jnp.exp(-rel * lgd[1]), 0.0)]
    w_q = [jnp.exp((pos + 1.0) * lgd[0]), jnp.exp((CHUNK - pos) * lgd[1])]
    w_k = [jnp.exp((CHUNK - 1.0 - pos) * lgd[0]), jnp.exp(pos * lgd[1])]
    chunk_decay = [jnp.exp(CHUNK * x) for x in lgd]

    def prep(c, carry):
        rows = pl.ds(pl.multiple_of(c * CHUNK, CHUNK), CHUNK)
        q = q_ref[rows, :]
        k = k_ref[rows, :] * (R_DK ** -0.5)
        if rope:
            cs, sn = cos_ref[rows, :], sin_ref[rows, :]
            q = q * cs + pltpu.roll(q, R_DK // 2, 1) * sn
            k = k * cs + pltpu.roll(k, R_DK // 2, 1) * sn
        qs_scr[rows, :] = q.astype(BF16)
        ks_scr[rows, :] = k.astype(BF16)
        kt_scr[:, rows] = k.T.astype(BF16)
        return carry

    def scan(i, carry):
        rows = [pl.ds(pl.multiple_of(c * CHUNK, CHUNK), CHUNK) for c in (i, nc - 1 - i)]
        qb = [qs_scr[r, :] for r in rows]
        kb = [ks_scr[r, :] for r in rows]
        kt = [kt_scr[:, r] for r in rows]
        v = [v_ref[r, :] for r in rows]
        vb = [x.astype(BF16) for x in v]
        vw = [(v[d] * w_k[d]).astype(BF16) for d in dirs]
        s_prev = [s_scr[d][...] for d in dirs]
        sc = [(_dot_nt(qb[d], kb[d]) * decay[d]).astype(BF16) for d in dirs]
        inter = [_dot(qb[d], s_prev[d].astype(BF16)) for d in dirs]
        u = [_dot(kt[d], vw[d]) for d in dirs]
        for d in dirs:
            oacc[d][rows[d], :] = _dot(sc[d], vb[d]) + w_q[d] * inter[d]
            s_scr[d][...] = chunk_decay[d] * s_prev[d] + u[d]
        return carry

    def finish(c, carry):
        rows = pl.ds(pl.multiple_of(c * CHUNK, CHUNK), CHUNK)
        ot = oacc[0][rows, :] + oacc[1][rows, :]
        ms = jnp.mean(ot * ot, axis=1, keepdims=True)
        y = ot * lax.rsqrt(ms + EPS) * ng_ref[...] * _silu(g_ref[rows, :])
        y_ref[rows, :] = y.astype(y_ref.dtype)
        return carry

    lax.fori_loop(0, nc, prep, 0, unroll=SCAN_UNROLL)
    lax.fori_loop(0, nc, scan, 0, unroll=SCAN_UNROLL)
    lax.fori_loop(0, nc, finish, 0, unroll=SCAN_UNROLL)
    if emit_state:
        for d in dirs:
            sf_ref[d] = s_scr[d][...]


def _ret_call(proj, lgr, ng, nb, seq, rope=None, state=None, emit=None):
    emit_state = emit is not None
    t = proj.shape[0]
    nc = seq // CHUNK
    dk = R_DK
    cspec = lambda base: pl.BlockSpec((seq, dk), lambda b, h: (b, base // dk + h))
    in_specs = [cspec(OFF_RQ), cspec(OFF_RK), cspec(OFF_RV), cspec(OFF_RG),
                pl.BlockSpec((None, 2, LANES), lambda b, h: (h, 0, 0)),
                pl.BlockSpec((1, dk), lambda b, h: (0, h))]
    args = [proj, proj, proj, proj, lgr, ng]
    if rope is not None:
        in_specs += [pl.BlockSpec((seq, dk), lambda b, h: (0, 0))] * 2
        args += list(rope)
    if state is not None:
        sl = state[0]
        in_specs.append(pl.BlockSpec((None, None, 2, None, dk, dk), lambda b, h: (b, sl, 0, h, 0, 0)))
        args.append(state[1])
    out_specs = [pl.BlockSpec((seq, dk), lambda b, h: (b, h))]
    out_shape = [jax.ShapeDtypeStruct((t, R_HEADS * dk), BF16)]
    aliases = {}
    if emit_state:
        layer, prev = emit
        out_specs.append(pl.BlockSpec((None, None, 2, None, dk, dk), lambda b, h: (b, layer, 0, h, 0, 0)))
        out_shape.append(jax.ShapeDtypeStruct((nb, DEPTH, 2, R_HEADS, dk, dk), F32))
        if prev is not None:
            in_specs.append(pl.BlockSpec(memory_space=pl.ANY))
            args.append(prev)
            aliases = {len(args) - 1: 1}
    return pl.pallas_call(
        functools.partial(_ret_kernel, nc=nc, has_state=state is not None, emit_state=emit_state,
                          rope=rope is not None, alias_in=bool(aliases)),
        grid=(nb, R_HEADS),
        in_specs=in_specs, out_specs=out_specs, out_shape=out_shape, input_output_aliases=aliases,
        scratch_shapes=([pltpu.VMEM((seq, dk), F32)] * 2 + [pltpu.VMEM((dk, dk), F32)] * 2
                        + [pltpu.VMEM((seq, dk), BF16)] * 2 + [pltpu.VMEM((dk, seq), BF16)]),
        compiler_params=_cparams(2),
        name="retention",
    )(*args)


def _ssd_kernel(*refs, nc, has_state, emit_state, alias_in):
    it = iter(refs)
    (x_ref, b_ref, c_ref, z_ref, g_ref, wx_ref, wb_ref, wc_ref, bx_ref, bb_ref, bc_ref,
     al_ref, sd_ref) = (next(it) for _ in range(13))
    if has_state:
        h0_ref = next(it)
    if alias_in:
        next(it)
    y_ref = next(it)
    if emit_state:
        hf_ref = next(it)
    xs_scr, bt_scr, bb_scr, cc_scr, ht_scr, yb_scr = (next(it) for _ in range(6))

    _, _, lower, upper, tri = _chunk_masks()
    rowi = lax.broadcasted_iota(jnp.int32, (CHUNK, 1), 0)
    er = lax.broadcasted_iota(jnp.int32, (LANES, S_GW), 0)
    ec = lax.broadcasted_iota(jnp.int32, (LANES, S_GW), 1) // S_P
    expand = [jnp.where(er == ec + d * S_HPG, 1.0, 0.0).astype(BF16) for d in range(2)]
    a_row = -jnp.exp(al_ref[...])

    def conv_silu(ref, w_ref, bias_ref, c, rows):
        off = c * CHUNK
        cur = ref[rows, :]
        prev8 = ref[pl.ds(pl.multiple_of(jnp.maximum(off - 8, 0), 8), 8), :]
        next8 = ref[pl.ds(pl.multiple_of(jnp.minimum(off + CHUNK, (nc - 1) * CHUNK + CHUNK - 8), 8), 8), :]
        has_prev = jnp.where(c > 0, 1.0, 0.0)
        has_next = jnp.where(c < nc - 1, 1.0, 0.0)
        p_last = prev8[7:8, :] * has_prev
        n0 = next8[0:1, :] * has_next
        n1 = next8[1:2, :] * has_next
        xm1 = jnp.where(rowi == 0, p_last, pltpu.roll(cur, 1, 0))
        xp1 = jnp.where(rowi == CHUNK - 1, n0, pltpu.roll(cur, CHUNK - 1, 0))
        xp2 = jnp.where(rowi == CHUNK - 2, n0, jnp.where(rowi == CHUNK - 1, n1, pltpu.roll(cur, CHUNK - 2, 0)))
        w = w_ref[...]
        y = w[0:1, :] * xm1 + w[1:2, :] * cur + w[2:3, :] * xp1 + w[3:4, :] * xp2 + bias_ref[...]
        return _silu(y)

    def prep(c, carry):
        off = pl.multiple_of(c * CHUNK, CHUNK)
        rows = pl.ds(off, CHUNK)
        xs_scr[rows, :] = conv_silu(x_ref, wx_ref, bx_ref, c, rows)
        bm = conv_silu(b_ref, wb_ref, bb_ref, c, rows)
        bb_scr[rows, :] = bm.astype(BF16)
        bt_scr[:, rows] = bm.T.astype(BF16)
        cc_scr[rows, :] = conv_silu(c_ref, wc_ref, bc_ref, c, rows).astype(BF16)
        return carry

    lax.fori_loop(0, nc, prep, 0, unroll=SCAN_UNROLL)

    for d in range(2):
        if has_state:
            ht_scr[d] = h0_ref[d].T
        else:
            ht_scr[d] = jnp.zeros((S_N, S_GW), F32)

    dirs = (0, 1)
    masks = (lower, upper)
    y_out = (y_ref, yb_scr)
    first_half = lax.broadcasted_iota(jnp.int32, (CHUNK, LANES), 1) < S_P

    def scan(i, carry):
        rows = [pl.ds(pl.multiple_of(c * CHUNK, CHUNK), CHUNK) for c in (i, nc - 1 - i)]
        dt = [g_ref[r, :] for r in rows]
        a = [x * a_row for x in dt]
        cs = [_cumsum_rows(x, tri) for x in a]
        tot = [x[CHUNK - 1:CHUNK, :] for x in cs]
        bsum = [cs[0], tot[1] - cs[1] + a[1]]
        bsum_t = [x.T for x in bsum]
        dt_t = [x.T for x in dt]
        xs = [xs_scr[r, :] for r in rows]
        xb = [x.astype(BF16) for x in xs]
        bmb = [bb_scr[r, :] for r in rows]
        cmb = [cc_scr[r, :] for r in rows]
        btb = [bt_scr[:, r] for r in rows]
        cb = [_dot_nt(cmb[d], bmb[d]) for d in dirs]
        ht = [ht_scr[d] for d in dirs]
        inter = [_dot(cmb[d], ht[d].astype(BF16)) for d in dirs]
        ex = [_expand(jnp.exp(bsum[d]), expand[d], exact=False) for d in dirs]
        wexp = [_expand(jnp.exp(tot[d] - bsum[d]) * dt[d], expand[d], exact=False) for d in dirs]
        cd = [_expand(jnp.broadcast_to(jnp.exp(tot[d]), (8, LANES)), expand[d])[0:1, :] for d in dirs]
        u_t = [_dot(btb[d], (xs[d] * wexp[d]).astype(BF16)) for d in dirs]
        ys = [[], []]
        for p in range(S_HPG // 2):
            for d in dirs:
                ms = []
                for k in (2 * p, 2 * p + 1):
                    ln = d * S_HPG + k
                    decay = jnp.exp(jnp.where(masks[d], bsum[d][:, ln:ln + 1] - bsum_t[d][ln:ln + 1, :], -jnp.inf))
                    ms.append((cb[d] * decay * dt_t[d][ln:ln + 1, :]).astype(BF16))
                xp = xb[d][:, p * LANES:(p + 1) * LANES]
                zero = jnp.zeros_like(xp)
                rhs = jnp.concatenate([jnp.where(first_half, xp, zero), jnp.where(first_half, zero, xp)], axis=0)
                ys[d].append(_dot(jnp.concatenate(ms, axis=1), rhs))
        for d in dirs:
            y_out[d][rows[d], :] = jnp.concatenate(ys[d], axis=1) + ex[d] * inter[d]
            ht_scr[d] = ht[d] * cd[d] + u_t[d]
        return carry

    def finish(c, carry):
        rows = pl.ds(pl.multiple_of(c * CHUNK, CHUNK), CHUNK)
        yt = y_ref[rows, :] + yb_scr[rows, :] + sd_ref[...] * xs_scr[rows, :]
        y_ref[rows, :] = yt * _silu(z_ref[rows, :])
        return carry

    lax.fori_loop(0, nc, scan, 0, unroll=SCAN_UNROLL)
    lax.fori_loop(0, nc, finish, 0, unroll=SCAN_UNROLL)
    if emit_state:
        for d in range(2):
            hf_ref[d] = ht_scr[d].T


def _ssd_call(proj, gates, conv_w, conv_b, alog, sd, nb, seq, state=None, emit=None):
    emit_state = emit is not None
    t = proj.shape[0]
    nc = seq // CHUNK
    gw = S_GW
    n = S_N
    xw = S_HEADS * S_P
    in_specs = [pl.BlockSpec((seq, gw), lambda b, g: (b, OFF_SX // gw + g)),
                pl.BlockSpec((seq, n), lambda b, g: (b, OFF_SB // n + g)),
                pl.BlockSpec((seq, n), lambda b, g: (b, OFF_SC // n + g)),
                pl.BlockSpec((seq, gw), lambda b, g: (b, OFF_SZ // gw + g)),
                pl.BlockSpec((seq, LANES), lambda b, g: (b, M_HEADS + g)),
                pl.BlockSpec((4, gw), lambda b, g: (0, g)),
                pl.BlockSpec((4, n), lambda b, g: (0, xw // n + g)),
                pl.BlockSpec((4, n), lambda b, g: (0, xw // n + S_GROUPS + g)),
                pl.BlockSpec((1, gw), lambda b, g: (0, g)),
                pl.BlockSpec((1, n), lambda b, g: (0, xw // n + g)),
                pl.BlockSpec((1, n), lambda b, g: (0, xw // n + S_GROUPS + g)),
                pl.BlockSpec((None, 1, LANES), lambda b, g: (g, 0, 0)),
                pl.BlockSpec((1, gw), lambda b, g: (0, g))]
    args = [proj, proj, proj, proj, gates, conv_w, conv_w, conv_w, conv_b, conv_b, conv_b, alog, sd]
    if state is not None:
        sl = state[0]
        in_specs.append(pl.BlockSpec((None, None, 2, None, gw, n), lambda b, g: (b, sl, 0, g, 0, 0)))
        args.append(state[1])
    out_specs = [pl.BlockSpec((seq, gw), lambda b, g: (b, g))]
    out_shape = [jax.ShapeDtypeStruct((t, xw), F32)]
    aliases = {}
    if emit_state:
        layer, prev = emit
        out_specs.append(pl.BlockSpec((None, None, 2, None, gw, n), lambda b, g: (b, layer, 0, g, 0, 0)))
        out_shape.append(jax.ShapeDtypeStruct((nb, DEPTH, 2, S_GROUPS, gw, n), F32))
        if prev is not None:
            in_specs.append(pl.BlockSpec(memory_space=pl.ANY))
            args.append(prev)
            aliases = {len(args) - 1: 1}
    return pl.pallas_call(
        functools.partial(_ssd_kernel, nc=nc, has_state=state is not None, emit_state=emit_state,
                          alias_in=bool(aliases)),
        grid=(nb, S_GROUPS),
        in_specs=in_specs, out_specs=out_specs, out_shape=out_shape, input_output_aliases=aliases,
        scratch_shapes=[pltpu.VMEM((seq, gw), F32), pltpu.VMEM((n, seq), BF16),
                        pltpu.VMEM((seq, n), BF16), pltpu.VMEM((seq, n), BF16),
                        pltpu.VMEM((2, n, gw), F32), pltpu.VMEM((seq, gw), F32)],
        compiler_params=_cparams(2),
        name="ssd",
    )(*args)


_IN_SIZES = (1024, 1024, 1024, 1024, 8, 8, 1024, 1024, 1024, 1024, 1024, 1536, 32, 6144)
_IN_OFFS = np.concatenate([[0], np.cumsum(_IN_SIZES)])


def _gate_lane_index():
    idx = -np.ones((LANES,), np.int64)
    for h in range(M_HEADS):
        base = GATE_LANE_OFFSETS[h]
        idx[base + 0] = 0 * M_HEADS + h
        idx[base + 1] = 1 * M_HEADS + h
        idx[base + 2] = 2 * M_HEADS + 0 * M_HEADS + h
        idx[base + 3] = 2 * M_HEADS + 1 * M_HEADS + h
    for g in range(S_GROUPS):
        base = GATE_LANE_OFFSETS[M_HEADS + g]
        for d in range(2):
            for k in range(S_HPG):
                idx[base + d * S_HPG + k] = 4 * M_HEADS + d * S_HEADS + g * S_HPG + k
    return idx


def _place(vals, idx):
    taken = jnp.take(vals, jnp.asarray(np.maximum(idx, 0)), axis=-1)
    return jnp.where(jnp.asarray(idx >= 0), taken, 0.0)


def _prep_params(w_in, m_igate_b, m_fgate_b, s_dt_bias, s_a_log, r_decay, s_d):
    o = _IN_OFFS
    w_big = jnp.concatenate([w_in[:, :, o[0]:o[4]], w_in[:, :, o[6]:o[12]], w_in[:, :, o[13]:o[14]]],
                            axis=2).astype(BF16)
    w_small = jnp.concatenate([w_in[:, :, o[4]:o[6]], w_in[:, :, o[12]:o[13]]], axis=2)
    gidx = _gate_lane_index()
    w_g = _place(w_small, gidx).astype(BF16)
    b_small = jnp.concatenate([m_igate_b.reshape(DEPTH, -1), m_fgate_b.reshape(DEPTH, -1),
                               s_dt_bias.reshape(DEPTH, -1)], axis=1)
    b_g = _place(b_small, gidx).reshape(DEPTH, 1, LANES)
    aidx = -np.ones((S_GROUPS, LANES), np.int64)
    for g in range(S_GROUPS):
        for d in range(2):
            for k in range(S_HPG):
                aidx[g, d * S_HPG + k] = d * S_HEADS + g * S_HPG + k
    a_flat = s_a_log.reshape(DEPTH, -1)
    alog = jnp.stack([_place(a_flat, aidx[g]) for g in range(S_GROUPS)], axis=1).reshape(DEPTH, S_GROUPS, 1, LANES)
    lgr = jnp.broadcast_to(jnp.swapaxes(r_decay, 1, 2)[..., None], (DEPTH, R_HEADS, 2, LANES))
    sd = jnp.repeat(s_d, S_P, axis=1).reshape(DEPTH, 1, S_HEADS * S_P)
    return w_big, w_g, b_g, alog, lgr, sd


def _rope_tables(seq):
    n_rows = seq // GRID_W
    rows = jnp.repeat(jnp.arange(n_rows, dtype=F32), GRID_W)
    cols = jnp.tile(jnp.arange(GRID_W, dtype=F32), n_rows)
    inv = ROPE_BASE ** (-jnp.arange(ROPE_FREQS, dtype=F32) / ROPE_FREQS)
    ang = jnp.concatenate([rows[:, None] * inv, cols[:, None] * inv], -1)
    cos, sin = jnp.cos(ang), jnp.sin(ang)
    return jnp.concatenate([cos, cos], -1), jnp.concatenate([-sin, sin], -1)


def _layer(x, l, mods, rows_per_mod, nb, seq, pw, rope, state, emit, final_g):
    sh_a, sc_a, g_a, sh_f, sc_f, g_f = mods
    tm_l, tm_s = min(1024, rows_per_mod), min(512, rows_per_mod)
    proj, gates = _norm_mm_call(x, pw["norm_mix_g"], sc_a, sh_a, rows_per_mod, pw["w_big"], l, tm_l, 768,
                                F32, False, gates=(pw["w_g"], pw["b_g"]), name="in_proj")
    st_m = st_r = st_s = None
    if state is not None:
        st_m, st_r, st_s = state
    em_m = em_r = em_s = None
    if emit is not None:
        em_m, em_r, em_s = ((l, prev) for prev in emit)
    om = _mlstm_call(proj, gates, pw["m_norm_g"][l], nb, seq, state=st_m, emit=em_m)
    orr = _ret_call(proj, pw["lgr"][l], pw["r_norm_g"][l], nb, seq, rope=rope, state=st_r, emit=em_r)
    os_ = _ssd_call(proj, gates, pw["s_conv_w"][l], pw["s_conv_b"][l], pw["alog"][l], pw["sd"][l], nb, seq,
                    state=st_s, emit=em_s)
    merged = _merge_call(om[0], orr[0], os_[0], pw["s_norm_g"][l], pw["w_br_m"], pw["w_br_r"], pw["w_br_s"], l,
                         proj, tm_l, 512)
    x = _mm_resid_call(merged, pw["w_out"], l, x, g_a, rows_per_mod, tm_s, D_MODEL, name="out_proj")
    (hid,) = _norm_mm_call(x, pw["norm_mlp_g"], sc_f, sh_f, rows_per_mod, pw["w_ff1"], l, tm_l, 1024,
                           BF16, True, name="mlp_up")
    x = _mm_resid_call(hid, pw["w_ff2"], l, x, g_f, rows_per_mod, tm_s, 512, final_g=final_g, name="mlp_down")
    new_state = (om[1:], orr[1:], os_[1:]) if emit is not None else None
    return x, new_state


def kernel(x_prompt, x_sample, c, state_mlstm_C, state_mlstm_n, state_mlstm_m, state_ret, state_ssd, c_ctx, w_mod, b_mod, norm_mix_g, norm_mlp_g, w_in, m_igate_b, m_fgate_b, m_norm_g, r_decay, r_norm_g, s_conv_w, s_conv_b, s_dt_bias, s_a_log, s_d, s_norm_g, w_br_m, w_br_r, w_br_s, w_out, w_ff1, w_ff2, final_norm_g):
    bp, lp, d = x_prompt.shape
    bs, ls, _ = x_sample.shape
    xp = x_prompt.reshape(bp * lp, d)
    xs = x_sample.reshape(bs * ls, d)

    c_rows = jnp.zeros((8, d), F32).at[:bs].set(c).at[bs].set(c_ctx)
    mod = _mod_call(c_rows, w_mod, b_mod)
    rope = _rope_tables(ls)
    final_g = final_norm_g.reshape(1, d)

    w_big, w_g, b_g, alog, lgr, sd = _prep_params(w_in, m_igate_b, m_fgate_b, s_dt_bias, s_a_log, r_decay, s_d)
    pw = dict(w_big=w_big, w_g=w_g, b_g=b_g, alog=alog, lgr=lgr, sd=sd,
              norm_mix_g=norm_mix_g.reshape(DEPTH, 1, d), norm_mlp_g=norm_mlp_g.reshape(DEPTH, 1, d),
              m_norm_g=m_norm_g.reshape(DEPTH, 1, -1), r_norm_g=r_norm_g.reshape(DEPTH, 1, -1),
              s_norm_g=s_norm_g.reshape(DEPTH, 1, -1),
              s_conv_w=s_conv_w, s_conv_b=s_conv_b.reshape(DEPTH, 1, -1),
              w_br_m=w_br_m.astype(BF16), w_br_r=w_br_r.astype(BF16), w_br_s=w_br_s.astype(BF16),
              w_out=w_out.astype(BF16), w_ff1=w_ff1.astype(BF16), w_ff2=w_ff2.astype(BF16))
    cache_c = state_mlstm_C
    cache_r = state_ret
    cache_s = state_ssd.reshape(bs, DEPTH, 2, S_GROUPS, S_GW, S_N)

    big = (None, None, None)
    st_n, st_m = [], []
    for l in range(DEPTH):
        parts = mod[l].reshape(8, 6, 1, d)
        mods_ctx = tuple(parts[bs:bs + 1, i] for i in range(6))
        mods_lat = tuple(parts[:bs, i] for i in range(6))
        fg = final_g if l == DEPTH - 1 else None

        xp, st = _layer(xp, l, mods_ctx, bp * lp, bp, lp, pw, None, None, big, fg)
        (cf, nf, mf), (rf,), (hf,) = st
        big = (cf, rf, hf)
        st_n.append(jnp.transpose(nf, (0, 2, 1, 3)))
        st_m.append(jnp.transpose(mf[..., 0], (0, 2, 1)))

        cache = (
            (l, cache_c,
             jnp.transpose(state_mlstm_n[:, l], (0, 2, 1, 3)),
             jnp.broadcast_to(jnp.transpose(state_mlstm_m[:, l], (0, 2, 1))[..., None], (bs, M_HEADS, 2, LANES))),
            (l, cache_r),
            (l, cache_s),
        )
        xs, _ = _layer(xs, l, mods_lat, ls, bs, ls, pw, rope, cache, None, fg)

    return (xp.reshape(bp, lp, d), xs.reshape(bs, ls, d),
            big[0], jnp.stack(st_n, 1), jnp.stack(st_m, 1), big[1],
            big[2].reshape(bp, DEPTH, 2, S_HEADS, S_P, S_N))
```

```python
import functools

import numpy as np
import jax
import jax.numpy as jnp
from jax import lax
from jax.experimental import pallas as pl
from jax.experimental.pallas import tpu as pltpu

F32 = jnp.float32
BF16 = jnp.bfloat16

D_MODEL = 2048
DEPTH = 2
CHUNK = 128
EPS = 1e-6
M_HEADS, M_DK = 4, 256
R_HEADS, R_DK = 8, 128
S_HEADS, S_P, S_GROUPS, S_N = 16, 64, 2, 128
S_HPG = S_HEADS // S_GROUPS
S_GW = S_HPG * S_P
GRID_W = 64
ROPE_BASE = 10000.0
ROPE_FREQS = R_DK // 4
D_FF = 4 * D_MODEL
LANES = 128
VMEM_LIMIT = 56 * 1024 * 1024
SCAN_UNROLL = 2

OFF_MQ, OFF_MK, OFF_MV, OFF_MO = 0, 1024, 2048, 3072
OFF_RQ, OFF_RK, OFF_RV, OFF_RG = 4096, 5120, 6144, 7168
OFF_SZ, OFF_SX, OFF_SB, OFF_SC = 8192, 9216, 10240, 10496
N_MIX = 10752
N_GATES = (M_HEADS + S_GROUPS) * LANES
GATE_LANE_OFFSETS = tuple(4 * h for h in range(M_HEADS)) + tuple(4 * M_HEADS + 2 * S_HPG * g for g in range(S_GROUPS))


def _cparams(n_axes):
    return pltpu.CompilerParams(dimension_semantics=("arbitrary",) * n_axes,
                                vmem_limit_bytes=VMEM_LIMIT)


def _dot(a, b):
    return jnp.dot(a, b, preferred_element_type=F32)


def _dot_nt(a, b):
    return lax.dot_general(a, b, (((1,), (1,)), ((), ())), preferred_element_type=F32)


def _split2(x):
    hi = x.astype(BF16)
    lo = (x - hi.astype(F32)).astype(BF16)
    return hi, lo


def _cumsum_rows(x, tri):
    hi, lo = _split2(x)
    return _dot(tri, hi) + _dot(tri, lo)


def _expand(x, e, exact=True):
    if not exact:
        return _dot(x.astype(BF16), e)
    hi, lo = _split2(x)
    return _dot(hi, e) + _dot(lo, e)


def _softplus(x):
    return jnp.maximum(x, 0.0) + jnp.log1p(jnp.exp(-jnp.abs(x)))


def _sigmoid(x):
    return 1.0 / (1.0 + jnp.exp(-x))


def _silu(x):
    return x * _sigmoid(x)


def _mod_kernel(c_ref, w_ref, b_ref, o_ref):
    a = _silu(c_ref[...]).astype(BF16)
    o_ref[...] = _dot(a, w_ref[...].astype(BF16)) + b_ref[...]


def _mod_call(c_rows, w_mod, b_mod):
    tn = 1024
    n = w_mod.shape[-1]
    return pl.pallas_call(
        _mod_kernel,
        grid=(DEPTH, n // tn),
        in_specs=[pl.BlockSpec((8, D_MODEL), lambda l, j: (0, 0)),
                  pl.BlockSpec((None, D_MODEL, tn), lambda l, j: (l, 0, j)),
                  pl.BlockSpec((None, 1, tn), lambda l, j: (l, 0, j))],
        out_specs=pl.BlockSpec((None, 8, tn), lambda l, j: (l, 0, j)),
        out_shape=jax.ShapeDtypeStruct((DEPTH, 8, n), F32),
        compiler_params=_cparams(2),
        name="mod_proj",
    )(c_rows, w_mod, b_mod.reshape(DEPTH, 1, n))


def _norm_mm_kernel(*refs, with_gates, relu2):
    if with_gates:
        x_ref, g_ref, sc_ref, sh_ref, w_ref, wg_ref, bg_ref, o_ref, og_ref, h_scr = refs
    else:
        x_ref, g_ref, sc_ref, sh_ref, w_ref, o_ref, h_scr = refs

    @pl.when(pl.program_id(1) == 0)
    def _():
        x = x_ref[...]
        ms = jnp.mean(x * x, axis=-1, keepdims=True)
        y = x * lax.rsqrt(ms + EPS) * g_ref[...]
        hb = (y * (1.0 + sc_ref[...]) + sh_ref[...]).astype(BF16)
        h_scr[...] = hb
        if with_gates:
            pre = _dot(hb, wg_ref[...]) + bg_ref[...]
            lane = lax.broadcasted_iota(jnp.int32, pre.shape, 1)
            act = jnp.where(lane < 4 * M_HEADS,
                            jnp.where(lane % 4 < 2, pre, -_softplus(-pre)),
                            _softplus(pre))
            for blk, off in enumerate(GATE_LANE_OFFSETS):
                own = 4 if blk < M_HEADS else 2 * S_HPG
                v = act if off == 0 else pltpu.roll(act, LANES - off, 1)
                og_ref[:, blk * LANES:(blk + 1) * LANES] = jnp.where(lane < own, v, 0.0)

    acc = _dot(h_scr[...], w_ref[...])
    if relu2:
        r = jnp.maximum(acc, 0.0)
        acc = r * r
    o_ref[...] = acc.astype(o_ref.dtype)


def _norm_mm_call(x, g, sc, sh, rows_per_mod, w, layer, tm, tn, out_dtype, relu2, gates=None, name="norm_mm"):
    t, d = x.shape
    n = w.shape[-1]
    mod_idx = lambda i, j: ((i * tm) // rows_per_mod, 0, 0)
    in_specs = [pl.BlockSpec((tm, d), lambda i, j: (i, 0)),
                pl.BlockSpec((None, 1, d), lambda i, j: (layer, 0, 0)),
                pl.BlockSpec((None, 1, d), mod_idx),
                pl.BlockSpec((None, 1, d), mod_idx),
                pl.BlockSpec((None, d, tn), lambda i, j: (layer, 0, j))]
    args = [x, g, sc, sh, w]
    out_specs = [pl.BlockSpec((tm, tn), lambda i, j: (i, j))]
    out_shape = [jax.ShapeDtypeStruct((t, n), out_dtype)]
    if gates is not None:
        wg, bg = gates
        in_specs += [pl.BlockSpec((None, d, LANES), lambda i, j: (layer, 0, 0)),
                     pl.BlockSpec((None, 1, LANES), lambda i, j: (layer, 0, 0))]
        args += [wg, bg]
        out_specs.append(pl.BlockSpec((tm, N_GATES), lambda i, j: (i, 0)))
        out_shape.append(jax.ShapeDtypeStruct((t, N_GATES), F32))
    return pl.pallas_call(
        functools.partial(_norm_mm_kernel, with_gates=gates is not None, relu2=relu2),
        grid=(t // tm, n // tn),
        in_specs=in_specs, out_specs=out_specs, out_shape=out_shape,
        scratch_shapes=[pltpu.VMEM((tm, d), BF16)],
        compiler_params=_cparams(2),
        name=name,
    )(*args)


def _mm_resid_kernel(*refs, nj, tn, final_norm):
    if final_norm:
        a_ref, w_ref, r_ref, gate_ref, fg_ref, o_ref = refs
    else:
        a_ref, w_ref, r_ref, gate_ref, o_ref = refs
    j = pl.program_id(1)
    cols = pl.ds(pl.multiple_of(j * tn, tn), tn)
    o_ref[:, cols] = r_ref[...] + gate_ref[...] * _dot(a_ref[...], w_ref[...])
    if final_norm:
        @pl.when(j == nj - 1)
        def _():
            xn = o_ref[...]
            ms = jnp.mean(xn * xn, axis=-1, keepdims=True)
            o_ref[...] = xn * lax.rsqrt(ms + EPS) * fg_ref[...]


def _mm_resid_call(a, w, layer, resid, gate, rows_per_mod, tm, tn, final_g=None, name="mm_resid"):
    t, kdim = a.shape
    n = w.shape[-1]
    nj = n // tn
    mod_idx = lambda i, j: ((i * tm) // rows_per_mod, 0, j)
    in_specs = [pl.BlockSpec((tm, kdim), lambda i, j: (i, 0)),
                pl.BlockSpec((None, kdim, tn), lambda i, j: (layer, 0, j)),
                pl.BlockSpec((tm, tn), lambda i, j: (i, j)),
                pl.BlockSpec((None, 1, tn), mod_idx)]
    args = [a, w, resid, gate]
    if final_g is not None:
        in_specs.append(pl.BlockSpec((1, n), lambda i, j: (0, 0)))
        args.append(final_g)
    return pl.pallas_call(
        functools.partial(_mm_resid_kernel, nj=nj, tn=tn, final_norm=final_g is not None),
        grid=(t // tm, nj),
        in_specs=in_specs,
        out_specs=pl.BlockSpec((tm, n), lambda i, j: (i, 0)),
        out_shape=jax.ShapeDtypeStruct((t, n), F32),
        compiler_params=_cparams(2),
        name=name,
    )(*args)


def _merge_kernel(ym_ref, yr_ref, ts_ref, sg_ref, wm_ref, wr_ref, ws_ref, g0_ref, g1_ref, g2_ref,
                  o_ref, ys_scr):
    @pl.when(pl.program_id(1) == 0)
    def _():
        t = ts_ref[...]
        ms = jnp.mean(t * t, axis=-1, keepdims=True)
        ys_scr[...] = (t * lax.rsqrt(ms + EPS) * sg_ref[...]).astype(BF16)

    acc = _sigmoid(g0_ref[...].astype(F32)) * _dot(ym_ref[...], wm_ref[...])
    acc += _sigmoid(g1_ref[...].astype(F32)) * _dot(yr_ref[...], wr_ref[...])
    acc += _sigmoid(g2_ref[...].astype(F32)) * _dot(ys_scr[...], ws_ref[...])
    o_ref[...] = acc.astype(o_ref.dtype)


def _merge_call(ym, yr, ts, sg, wm, wr, ws, layer, proj, tm, tn):
    t, w = ym.shape
    d = wm.shape[-1]
    gl_blk = N_MIX // tn
    per_br = d // tn
    yspec = pl.BlockSpec((tm, w), lambda i, j: (i, 0))
    wspec = pl.BlockSpec((None, w, tn), lambda i, j: (layer, 0, j))
    gspec = lambda br: pl.BlockSpec((tm, tn), lambda i, j: (i, gl_blk + br * per_br + j))
    return pl.pallas_call(
        _merge_kernel,
        grid=(t // tm, d // tn),
        in_specs=[yspec, yspec, yspec, pl.BlockSpec((1, w), lambda i, j: (0, 0)),
                  wspec, wspec, wspec, gspec(0), gspec(1), gspec(2)],
        out_specs=pl.BlockSpec((tm, tn), lambda i, j: (i, j)),
        out_shape=jax.ShapeDtypeStruct((t, d), BF16),
        scratch_shapes=[pltpu.VMEM((tm, w), BF16)],
        compiler_params=_cparams(2),
        name="merge",
    )(ym, yr, ts, sg, wm, wr, ws, proj, proj, proj)


def _chunk_masks():
    row = lax.broadcasted_iota(jnp.int32, (CHUNK, CHUNK), 0)
    col = lax.broadcasted_iota(jnp.int32, (CHUNK, CHUNK), 1)
    lower = col <= row
    upper = col >= row
    tri = jnp.where(lower, 1.0, 0.0).astype(BF16)
    return row, col, lower, upper, tri


def _mlstm_kernel(*refs, nc, has_state, emit_state, alias_in):
    it = iter(refs)
    q_ref, k_ref, v_ref, o_ref, g_ref, ng_ref = (next(it) for _ in range(6))
    if has_state:
        c0_ref, n0_ref, m0_ref = (next(it) for _ in range(3))
    if alias_in:
        next(it)
    y_ref = next(it)
    if emit_state:
        cf_ref, nf_ref, mf_ref = (next(it) for _ in range(3))
    hacc = (next(it), next(it))
    c_scr = (next(it), next(it))
    n_scr = (next(it), next(it))
    m_scr = (next(it), next(it))

    _, _, lower, upper, tri = _chunk_masks()
    ones_bf = jnp.ones((CHUNK, LANES), BF16)
    for d in range(2):
        if has_state:
            c_scr[d][...] = c0_ref[d]
            n_scr[d][0:1, :] = n0_ref[d:d + 1, :]
            m_scr[d][0:1, :] = m0_ref[d:d + 1, :]
        else:
            c_scr[d][...] = jnp.zeros_like(c_scr[d])
            n_scr[d][...] = jnp.zeros_like(n_scr[d])
            m_scr[d][...] = jnp.zeros_like(m_scr[d])

    dirs = (0, 1)
    masks = (lower, upper)

    def scan(i, carry):
        cidx = (i, nc - 1 - i)
        rows = [pl.ds(pl.multiple_of(c * CHUNK, CHUNK), CHUNK) for c in cidx]
        g = [g_ref[r, :] for r in rows]
        cs = [_cumsum_rows(x, tri) for x in g]
        tot = [x[CHUNK - 1:CHUNK, :] for x in cs]
        bsum = [cs[0], tot[1] - cs[1] + g[1]]
        bsum_t = [x.T for x in bsum]
        g_t = [x.T for x in g]
        b_col = [bsum[d][:, 2 + d:3 + d] for d in dirs]
        b_row = [bsum_t[d][2 + d:3 + d, :] for d in dirs]
        i_col = [g[d][:, d:d + 1] for d in dirs]
        i_row = [g_t[d][d:d + 1, :] for d in dirs]
        b_last = [tot[d][:, 2 + d:3 + d] for d in dirs]

        qb = [q_ref[r, :].astype(BF16) for r in rows]
        k = [k_ref[r, :].astype(F32) * (M_DK ** -0.5) for r in rows]
        kb = [x.astype(BF16) for x in k]
        vb = [v_ref[r, :].astype(BF16) for r in rows]
        qk = [_dot_nt(qb[d], kb[d]) for d in dirs]
        logw = [jnp.where(masks[d], b_col[d] - b_row[d] + i_row[d], -jnp.inf) for d in dirs]
        m_loc = [jnp.max(x, axis=1, keepdims=True) for x in logw]
        m_chunk = [jnp.max(b_last[d] - b_row[d] + i_row[d], axis=1, keepdims=True) for d in dirs]
        s = [qk[d] * jnp.exp(logw[d] - m_loc[d]) for d in dirs]
        kw = [k[d] * jnp.exp(b_last[d] - b_col[d] + i_col[d] - m_chunk[d]) for d in dirs]
        a_num = [_dot(s[d].astype(BF16), vb[d]) for d in dirs]
        kw_t = [x.T.astype(BF16) for x in kw]
        a_den = [_dot(s[d].astype(BF16), ones_bf) for d in dirs]

        m_prev = [m_scr[d][0:1, 0:1] for d in dirs]
        c_prev = [c_scr[d][...] for d in dirs]
        n_prev = [n_scr[d][0:1, :] for d in dirs]
        q_c = [_dot(qb[d], c_prev[d].astype(BF16)) for d in dirs]
        u_c = [_dot(kw_t[d], vb[d]) for d in dirs]
        gg = [b_col[d] + m_prev[d] for d in dirs]
        m_tot = [jnp.maximum(m_loc[d], gg[d]) for d in dirs]
        e_intra = [jnp.exp(m_loc[d] - m_tot[d]) for d in dirs]
        e_inter = [jnp.exp(gg[d] - m_tot[d]) for d in dirs]
        q_n = [_dot_nt(qb[d], jnp.broadcast_to(n_prev[d].astype(BF16), (LANES, M_DK))) for d in dirs]
        num = [e_intra[d] * a_num[d] + e_inter[d] * q_c[d] for d in dirs]
        den = [e_intra[d] * a_den[d] + e_inter[d] * q_n[d] for d in dirs]
        for d in dirs:
            inv = 1.0 / jnp.maximum(jnp.abs(den[d]), jnp.exp(-m_tot[d]))
            hacc[d][rows[d], :] = num[d] * jnp.concatenate([inv, inv], axis=1)

        u_n = [jnp.sum(x, axis=0, keepdims=True) for x in kw]
        m_new = [jnp.maximum(b_last[d] + m_prev[d], m_chunk[d]) for d in dirs]
        a = [jnp.exp(b_last[d] + m_prev[d] - m_new[d]) for d in dirs]
        e = [jnp.exp(m_chunk[d] - m_new[d]) for d in dirs]
        for d in dirs:
            c_scr[d][...] = a[d] * c_prev[d] + e[d] * u_c[d]
            n_scr[d][0:1, :] = a[d] * n_prev[d] + e[d] * u_n[d]
            m_scr[d][0:1, :] = jnp.broadcast_to(m_new[d], (1, LANES))
        return carry

    def finish(c, carry):
        rows = pl.ds(pl.multiple_of(c * CHUNK, CHUNK), CHUNK)
        ht = hacc[0][rows, :] + hacc[1][rows, :]
        ms = jnp.mean(ht * ht, axis=1, keepdims=True)
        y = ht * lax.rsqrt(ms + EPS) * ng_ref[...] * _sigmoid(o_ref[rows, :].astype(F32))
        y_ref[rows, :] = y.astype(y_ref.dtype)
        return carry

    lax.fori_loop(0, nc, scan, 0, unroll=SCAN_UNROLL)
    lax.fori_loop(0, nc, finish, 0, unroll=SCAN_UNROLL)
    if emit_state:
        for d in range(2):
            cf_ref[d] = c_scr[d][...]
            nf_ref[d:d + 1, :] = n_scr[d][0:1, :]
            mf_ref[d:d + 1, :] = m_scr[d][0:1, :]


def _mlstm_call(proj, gates, ng, nb, seq, state=None, emit=None):
    emit_state = emit is not None
    t = proj.shape[0]
    nc = seq // CHUNK
    dk = M_DK
    cspec = lambda base: pl.BlockSpec((seq, dk), lambda b, h: (b, base // dk + h))
    in_specs = [cspec(OFF_MQ), cspec(OFF_MK), cspec(OFF_MV), cspec(OFF_MO),
                pl.BlockSpec((seq, LANES), lambda b, h: (b, h)),
                pl.BlockSpec((1, dk), lambda b, h: (0, h))]
    args = [proj, proj, proj, proj, gates, ng]
    if state is not None:
        sl = state[0]
        in_specs += [pl.BlockSpec((None, None, 2, None, dk, dk), lambda b, h: (b, sl, 0, h, 0, 0)),
                     pl.BlockSpec((None, None, 2, dk), lambda b, h: (b, h, 0, 0)),
                     pl.BlockSpec((None, None, 2, LANES), lambda b, h: (b, h, 0, 0))]
        args += list(state[1:])
    out_specs = [pl.BlockSpec((seq, dk), lambda b, h: (b, h))]
    out_shape = [jax.ShapeDtypeStruct((t, M_HEADS * dk), BF16)]
    aliases = {}
    if emit_state:
        layer, prev = emit
        out_specs += [pl.BlockSpec((None, None, 2, None, dk, dk), lambda b, h: (b, layer, 0, h, 0, 0)),
                      pl.BlockSpec((None, None, 2, dk), lambda b, h: (b, h, 0, 0)),
                      pl.BlockSpec((None, None, 2, LANES), lambda b, h: (b, h, 0, 0))]
        out_shape += [jax.ShapeDtypeStruct((nb, DEPTH, 2, M_HEADS, dk, dk), F32),
                      jax.ShapeDtypeStruct((nb, M_HEADS, 2, dk), F32),
                      jax.ShapeDtypeStruct((nb, M_HEADS, 2, LANES), F32)]
        if prev is not None:
            in_specs.append(pl.BlockSpec(memory_space=pl.ANY))
            args.append(prev)
            aliases = {len(args) - 1: 1}
    return pl.pallas_call(
        functools.partial(_mlstm_kernel, nc=nc, has_state=state is not None, emit_state=emit_state,
                          alias_in=bool(aliases)),
        grid=(nb, M_HEADS),
        in_specs=in_specs, out_specs=out_specs, out_shape=out_shape, input_output_aliases=aliases,
        scratch_shapes=([pltpu.VMEM((seq, dk), F32)] * 2 + [pltpu.VMEM((dk, dk), F32)] * 2
                        + [pltpu.VMEM((8, dk), F32)] * 2 + [pltpu.VMEM((8, LANES), F32)] * 2),
        compiler_params=_cparams(2),
        name="mlstm",
    )(*args)


R_HPS = 2


def _ret_kernel(*refs, nc, has_state, emit_state, rope, alias_in):
    it = iter(refs)
    q_ref, k_ref, v_ref, g_ref, lg_ref, ng_ref = (next(it) for _ in range(6))
    if rope:
        cos_ref, sin_ref = next(it), next(it)
    if has_state:
        s0_ref = next(it)
    if alias_in:
        next(it)
    y_ref = next(it)
    if emit_state:
        sf_ref = next(it)
    chains = [(h, d) for h in range(R_HPS) for d in range(2)]
    oacc = {c: next(it) for c in chains}
    s_scr = {c: next(it) for c in chains}
    qs_scr, ks_scr, kt_scr = (next(it) for _ in range(3))

    row, col, lower, upper, _ = _chunk_masks()
    rel = (row - col).astype(F32)
    pos = row.astype(F32)
    for (h, d) in chains:
        if has_state:
            s_scr[h, d][...] = s0_ref[d, h]
        else:
            s_scr[h, d][...] = jnp.zeros_like(s_scr[h, d])

    decay, w_q, w_k, chunk_decay = {}, {}, {}, {}
    for (h, d) in chains:
        lgd = -jnp.exp(lg_ref[h, d:d + 1, :])
        if d == 0:
            decay[h, d] = jnp.where(lower, jnp.exp(rel * lgd), 0.0)
            w_q[h, d] = jnp.exp((pos + 1.0) * lgd)
            w_k[h, d] = jnp.exp((CHUNK - 1.0 - pos) * lgd)
        else:
            decay[h, d] = jnp.where(upper, jnp.exp(-rel * lgd), 0.0)
            w_q[h, d] = jnp.exp((CHUNK - pos) * lgd)
            w_k[h, d] = jnp.exp(pos * lgd)
        chunk_decay[h, d] = jnp.exp(CHUNK * lgd)

    def hcols(h):
        return slice(h * R_DK, (h + 1) * R_DK)

    def prep(c, carry):
        rows = pl.ds(pl.multiple_of(c * CHUNK, CHUNK), CHUNK)
        for h in range(R_HPS):
            q = q_ref[rows, hcols(h)].astype(F32)
            k = k_ref[rows, hcols(h)].astype(F32) * (R_DK ** -0.5)
            if rope:
                cs, sn = cos_ref[rows, :], sin_ref[rows, :]
                q = q * cs + pltpu.roll(q, R_DK // 2, 1) * sn
                k = k * cs + pltpu.roll(k, R_DK // 2, 1) * sn
            qs_scr[rows, hcols(h)] = q.astype(BF16)
            ks_scr[rows, hcols(h)] = k.astype(BF16)
            kt_scr[hcols(h), rows] = k.T.astype(BF16)
        return carry

    def scan(i, carry):
        rows = {0: pl.ds(pl.multiple_of(i * CHUNK, CHUNK), CHUNK),
                1: pl.ds(pl.multiple_of((nc - 1 - i) * CHUNK, CHUNK), CHUNK)}
        qb = {(h, d): qs_scr[rows[d], hcols(h)] for (h, d) in chains}
        kb = {(h, d): ks_scr[rows[d], hcols(h)] for (h, d) in chains}
        kt = {(h, d): kt_scr[hcols(h), rows[d]] for (h, d) in chains}
        vb = {(h, d): v_ref[rows[d], hcols(h)].astype(BF16) for (h, d) in chains}
        vw = {c: (vb[c].astype(F32) * w_k[c]).astype(BF16) for c in chains}
        s_prev = {c: s_scr[c][...] for c in chains}
        sc = {c: (_dot_nt(qb[c], kb[c]) * decay[c]).astype(BF16) for c in chains}
        inter = {c: _dot(qb[c], s_prev[c].astype(BF16)) for c in chains}
        u = {c: _dot(kt[c], vw[c]) for c in chains}
        intra = {c: _dot(sc[c], vb[c]) for c in chains}
        for (h, d) in chains:
            oacc[h, d][rows[d], :] = intra[h, d] + w_q[h, d] * inter[h, d]
            s_scr[h, d][...] = chunk_decay[h, d] * s_prev[h, d] + u[h, d]
        return carry

    def finish(c, carry):
        rows = pl.ds(pl.multiple_of(c * CHUNK, CHUNK), CHUNK)
        for h in range(R_HPS):
            ot = oacc[h, 0][rows, :] + oacc[h, 1][rows, :]
            ms = jnp.mean(ot * ot, axis=1, keepdims=True)
            y = ot * lax.rsqrt(ms + EPS) * ng_ref[:, hcols(h)] * _silu(g_ref[rows, hcols(h)].astype(F32))
            y_ref[rows, hcols(h)] = y.astype(y_ref.dtype)
        return carry

    lax.fori_loop(0, nc, prep, 0, unroll=SCAN_UNROLL)
    lax.fori_loop(0, nc, scan, 0, unroll=SCAN_UNROLL)
    lax.fori_loop(0, nc, finish, 0, unroll=SCAN_UNROLL)
    if emit_state:
        for (h, d) in chains:
            sf_ref[d, h] = s_scr[h, d][...]


def _ret_call(proj, lgr, ng, nb, seq, rope=None, state=None, emit=None):
    emit_state = emit is not None
    t = proj.shape[0]
    nc = seq // CHUNK
    dk = R_DK
    bw = R_HPS * dk
    cspec = lambda base: pl.BlockSpec((seq, bw), lambda b, h: (b, base // bw + h))
    in_specs = [cspec(OFF_RQ), cspec(OFF_RK), cspec(OFF_RV), cspec(OFF_RG),
                pl.BlockSpec((R_HPS, 2, LANES), lambda b, h: (h, 0, 0)),
                pl.BlockSpec((1, bw), lambda b, h: (0, h))]
    args = [proj, proj, proj, proj, lgr, ng]
    if rope is not None:
        in_specs += [pl.BlockSpec((seq, dk), lambda b, h: (0, 0))] * 2
        args += list(rope)
    if state is not None:
        sl = state[0]
        in_specs.append(pl.BlockSpec((None, None, 2, R_HPS, dk, dk), lambda b, h: (b, sl, 0, h, 0, 0)))
        args.append(state[1])
    out_specs = [pl.BlockSpec((seq, bw), lambda b, h: (b, h))]
    out_shape = [jax.ShapeDtypeStruct((t, R_HEADS * dk), BF16)]
    aliases = {}
    if emit_state:
        layer, prev = emit
        out_specs.append(pl.BlockSpec((None, None, 2, R_HPS, dk, dk), lambda b, h: (b, layer, 0, h, 0, 0)))
        out_shape.append(jax.ShapeDtypeStruct((nb, DEPTH, 2, R_HEADS, dk, dk), F32))
        if prev is not None:
            in_specs.append(pl.BlockSpec(memory_space=pl.ANY))
            args.append(prev)
            aliases = {len(args) - 1: 1}
    n_chain = 2 * R_HPS
    return pl.pallas_call(
        functools.partial(_ret_kernel, nc=nc, has_state=state is not None, emit_state=emit_state,
                          rope=rope is not None, alias_in=bool(aliases)),
        grid=(nb, R_HEADS // R_HPS),
        in_specs=in_specs, out_specs=out_specs, out_shape=out_shape, input_output_aliases=aliases,
        scratch_shapes=([pltpu.VMEM((seq, dk), F32)] * n_chain + [pltpu.VMEM((dk, dk), F32)] * n_chain
                        + [pltpu.VMEM((seq, bw), BF16)] * 2 + [pltpu.VMEM((bw, seq), BF16)]),
        compiler_params=_cparams(2),
        name="retention",
    )(*args)


def _ssd_kernel(*refs, nc, has_state, emit_state, alias_in):
    it = iter(refs)
    (x_ref, b_ref, c_ref, z_ref, g_ref, wx_ref, wb_ref, wc_ref, bx_ref, bb_ref, bc_ref,
     al_ref, sd_ref) = (next(it) for _ in range(13))
    if has_state:
        h0_ref = next(it)
    if alias_in:
        next(it)
    y_ref = next(it)
    if emit_state:
        hf_ref = next(it)
    xs_scr, bt_scr, bb_scr, cc_scr, ht_scr, yb_scr = (next(it) for _ in range(6))

    _, _, lower, upper, tri = _chunk_masks()
    rowi = lax.broadcasted_iota(jnp.int32, (CHUNK, 1), 0)
    er = lax.broadcasted_iota(jnp.int32, (LANES, S_GW), 0)
    ec = lax.broadcasted_iota(jnp.int32, (LANES, S_GW), 1) // S_P
    expand = [jnp.where(er == ec + d * S_HPG, 1.0, 0.0).astype(BF16) for d in range(2)]
    a_row = -jnp.exp(al_ref[...])

    def conv_silu(ref, w_ref, bias_ref, c, rows):
        off = c * CHUNK
        cur = ref[rows, :].astype(F32)
        prev_w = ref[pl.ds(pl.multiple_of(jnp.maximum(off - 16, 0), 16), 16), :].astype(F32)
        next_w = ref[pl.ds(pl.multiple_of(jnp.minimum(off + CHUNK, nc * CHUNK - 16), 16), 16), :].astype(F32)
        has_prev = jnp.where(c > 0, 1.0, 0.0)
        has_next = jnp.where(c < nc - 1, 1.0, 0.0)
        p_last = prev_w[15:16, :] * has_prev
        n0 = next_w[0:1, :] * has_next
        n1 = next_w[1:2, :] * has_next
        xm1 = jnp.where(rowi == 0, p_last, pltpu.roll(cur, 1, 0))
        xp1 = jnp.where(rowi == CHUNK - 1, n0, pltpu.roll(cur, CHUNK - 1, 0))
        xp2 = jnp.where(rowi == CHUNK - 2, n0, jnp.where(rowi == CHUNK - 1, n1, pltpu.roll(cur, CHUNK - 2, 0)))
        w = w_ref[...]
        y = w[0:1, :] * xm1 + w[1:2, :] * cur + w[2:3, :] * xp1 + w[3:4, :] * xp2 + bias_ref[...]
        return _silu(y)

    def prep(c, carry):
        off = pl.multiple_of(c * CHUNK, CHUNK)
        rows = pl.ds(off, CHUNK)
        xs_scr[rows, :] = conv_silu(x_ref, wx_ref, bx_ref, c, rows)
        bm = conv_silu(b_ref, wb_ref, bb_ref, c, rows)
        bb_scr[rows, :] = bm.astype(BF16)
        bt_scr[:, rows] = bm.T.astype(BF16)
        cc_scr[rows, :] = conv_silu(c_ref, wc_ref, bc_ref, c, rows).astype(BF16)
        return carry

    lax.fori_loop(0, nc, prep, 0, unroll=SCAN_UNROLL)

    for d in range(2):
        if has_state:
            ht_scr[d] = h0_ref[d].T
        else:
            ht_scr[d] = jnp.zeros((S_N, S_GW), F32)

    dirs = (0, 1)
    masks = (lower, upper)
    y_out = (y_ref, yb_scr)
    first_half = lax.broadcasted_iota(jnp.int32, (CHUNK, LANES), 1) < S_P

    def scan(i, carry):
        rows = [pl.ds(pl.multiple_of(c * CHUNK, CHUNK), CHUNK) for c in (i, nc - 1 - i)]
        dt = [g_ref[r, :] for r in rows]
        a = [x * a_row for x in dt]
        cs = [_cumsum_rows(x, tri) for x in a]
        tot = [x[CHUNK - 1:CHUNK, :] for x in cs]
        bsum = [cs[0], tot[1] - cs[1] + a[1]]
        bsum_t = [x.T for x in bsum]
        dt_t = [x.T for x in dt]
        xs = [xs_scr[r, :] for r in rows]
        xb = [x.astype(BF16) for x in xs]
        bmb = [bb_scr[r, :] for r in rows]
        cmb = [cc_scr[r, :] for r in rows]
        btb = [bt_scr[:, r] for r in rows]
        cb = [_dot_nt(cmb[d], bmb[d]) for d in dirs]
        ht = [ht_scr[d] for d in dirs]
        inter = [_dot(cmb[d], ht[d].astype(BF16)) for d in dirs]
        ex = [_expand(jnp.exp(bsum[d]), expand[d], exact=False) for d in dirs]
        wexp = [_expand(jnp.exp(tot[d] - bsum[d]) * dt[d], expand[d], exact=False) for d in dirs]
        cd = [_expand(jnp.broadcast_to(jnp.exp(tot[d]), (8, LANES)), expand[d])[0:1, :] for d in dirs]
        u_t = [_dot(btb[d], (xs[d] * wexp[d]).astype(BF16)) for d in dirs]
        ys = [[], []]
        for p in range(S_HPG // 2):
            for d in dirs:
                ms = []
                for k in (2 * p, 2 * p + 1):
                    ln = d * S_HPG + k
                    decay = jnp.exp(jnp.where(masks[d], bsum[d][:, ln:ln + 1] - bsum_t[d][ln:ln + 1, :], -jnp.inf))
                    ms.append((cb[d] * decay * dt_t[d][ln:ln + 1, :]).astype(BF16))
                xp = xb[d][:, p * LANES:(p + 1) * LANES]
                zero = jnp.zeros_like(xp)
                rhs = jnp.concatenate([jnp.where(first_half, xp, zero), jnp.where(first_half, zero, xp)], axis=0)
                ys[d].append(_dot(jnp.concatenate(ms, axis=1), rhs))
        for d in dirs:
            y_out[d][rows[d], :] = jnp.concatenate(ys[d], axis=1) + ex[d] * inter[d]
            ht_scr[d] = ht[d] * cd[d] + u_t[d]
        return carry

    def finish(c, carry):
        rows = pl.ds(pl.multiple_of(c * CHUNK, CHUNK), CHUNK)
        yt = y_ref[rows, :] + yb_scr[rows, :] + sd_ref[...] * xs_scr[rows, :]
        y_ref[rows, :] = yt * _silu(z_ref[rows, :].astype(F32))
        return carry

    lax.fori_loop(0, nc, scan, 0, unroll=SCAN_UNROLL)
    lax.fori_loop(0, nc, finish, 0, unroll=SCAN_UNROLL)
    if emit_state:
        for d in range(2):
            hf_ref[d] = ht_scr[d].T


def _ssd_call(proj, gates, conv_w, conv_b, alog, sd, nb, seq, state=None, emit=None):
    emit_state = emit is not None
    t = proj.shape[0]
    nc = seq // CHUNK
    gw = S_GW
    n = S_N
    xw = S_HEADS * S_P
    in_specs = [pl.BlockSpec((seq, gw), lambda b, g: (b, OFF_SX // gw + g)),
                pl.BlockSpec((seq, n), lambda b, g: (b, OFF_SB // n + g)),
                pl.BlockSpec((seq, n), lambda b, g: (b, OFF_SC // n + g)),
                pl.BlockSpec((seq, gw), lambda b, g: (b, OFF_SZ // gw + g)),
                pl.BlockSpec((seq, LANES), lambda b, g: (b, M_HEADS + g)),
                pl.BlockSpec((4, gw), lambda b, g: (0, g)),
                pl.BlockSpec((4, n), lambda b, g: (0, xw // n + g)),
                pl.BlockSpec((4, n), lambda b, g: (0, xw // n + S_GROUPS + g)),
                pl.BlockSpec((1, gw), lambda b, g: (0, g)),
                pl.BlockSpec((1, n), lambda b, g: (0, xw // n + g)),
                pl.BlockSpec((1, n), lambda b, g: (0, xw // n + S_GROUPS + g)),
                pl.BlockSpec((None, 1, LANES), lambda b, g: (g, 0, 0)),
                pl.BlockSpec((1, gw), lambda b, g: (0, g))]
    args = [proj, proj, proj, proj, gates, conv_w, conv_w, conv_w, conv_b, conv_b, conv_b, alog, sd]
    if state is not None:
        sl = state[0]
        in_specs.append(pl.BlockSpec((None, None, 2, None, gw, n), lambda b, g: (b, sl, 0, g, 0, 0)))
        args.append(state[1])
    out_specs = [pl.BlockSpec((seq, gw), lambda b, g: (b, g))]
    out_shape = [jax.ShapeDtypeStruct((t, xw), F32)]
    aliases = {}
    if emit_state:
        layer, prev = emit
        out_specs.append(pl.BlockSpec((None, None, 2, None, gw, n), lambda b, g: (b, layer, 0, g, 0, 0)))
        out_shape.append(jax.ShapeDtypeStruct((nb, DEPTH, 2, S_GROUPS, gw, n), F32))
        if prev is not None:
            in_specs.append(pl.BlockSpec(memory_space=pl.ANY))
            args.append(prev)
            aliases = {len(args) - 1: 1}
    return pl.pallas_call(
        functools.partial(_ssd_kernel, nc=nc, has_state=state is not None, emit_state=emit_state,
                          alias_in=bool(aliases)),
        grid=(nb, S_GROUPS),
        in_specs=in_specs, out_specs=out_specs, out_shape=out_shape, input_output_aliases=aliases,
        scratch_shapes=[pltpu.VMEM((seq, gw), F32), pltpu.VMEM((n, seq), BF16),
                        pltpu.VMEM((seq, n), BF16), pltpu.VMEM((seq, n), BF16),
                        pltpu.VMEM((2, n, gw), F32), pltpu.VMEM((seq, gw), F32)],
        compiler_params=_cparams(2),
        name="ssd",
    )(*args)


_IN_SIZES = (1024, 1024, 1024, 1024, 8, 8, 1024, 1024, 1024, 1024, 1024, 1536, 32, 6144)
_IN_OFFS = np.concatenate([[0], np.cumsum(_IN_SIZES)])


def _gate_lane_index():
    idx = -np.ones((LANES,), np.int64)
    for h in range(M_HEADS):
        base = GATE_LANE_OFFSETS[h]
        idx[base + 0] = 0 * M_HEADS + h
        idx[base + 1] = 1 * M_HEADS + h
        idx[base + 2] = 2 * M_HEADS + 0 * M_HEADS + h
        idx[base + 3] = 2 * M_HEADS + 1 * M_HEADS + h
    for g in range(S_GROUPS):
        base = GATE_LANE_OFFSETS[M_HEADS + g]
        for d in range(2):
            for k in range(S_HPG):
                idx[base + d * S_HPG + k] = 4 * M_HEADS + d * S_HEADS + g * S_HPG + k
    return idx


def _place(vals, idx):
    taken = jnp.take(vals, jnp.asarray(np.maximum(idx, 0)), axis=-1)
    return jnp.where(jnp.asarray(idx >= 0), taken, 0.0)


def _prep_params(w_in, m_igate_b, m_fgate_b, s_dt_bias, s_a_log, r_decay, s_d):
    o = _IN_OFFS
    w_big = jnp.concatenate([w_in[:, :, o[0]:o[4]], w_in[:, :, o[6]:o[12]], w_in[:, :, o[13]:o[14]]],
                            axis=2).astype(BF16)
    w_small = jnp.concatenate([w_in[:, :, o[4]:o[6]], w_in[:, :, o[12]:o[13]]], axis=2)
    gidx = _gate_lane_index()
    w_g = _place(w_small, gidx).astype(BF16)
    b_small = jnp.concatenate([m_igate_b.reshape(DEPTH, -1), m_fgate_b.reshape(DEPTH, -1),
                               s_dt_bias.reshape(DEPTH, -1)], axis=1)
    b_g = _place(b_small, gidx).reshape(DEPTH, 1, LANES)
    aidx = -np.ones((S_GROUPS, LANES), np.int64)
    for g in range(S_GROUPS):
        for d in range(2):
            for k in range(S_HPG):
                aidx[g, d * S_HPG + k] = d * S_HEADS + g * S_HPG + k
    a_flat = s_a_log.reshape(DEPTH, -1)
    alog = jnp.stack([_place(a_flat, aidx[g]) for g in range(S_GROUPS)], axis=1).reshape(DEPTH, S_GROUPS, 1, LANES)
    lgr = jnp.broadcast_to(jnp.swapaxes(r_decay, 1, 2)[..., None], (DEPTH, R_HEADS, 2, LANES))
    sd = jnp.repeat(s_d, S_P, axis=1).reshape(DEPTH, 1, S_HEADS * S_P)
    return w_big, w_g, b_g, alog, lgr, sd


def _rope_tables(seq):
    n_rows = seq // GRID_W
    rows = jnp.repeat(jnp.arange(n_rows, dtype=F32), GRID_W)
    cols = jnp.tile(jnp.arange(GRID_W, dtype=F32), n_rows)
    inv = ROPE_BASE ** (-jnp.arange(ROPE_FREQS, dtype=F32) / ROPE_FREQS)
    ang = jnp.concatenate([rows[:, None] * inv, cols[:, None] * inv], -1)
    cos, sin = jnp.cos(ang), jnp.sin(ang)
    return jnp.concatenate([cos, cos], -1), jnp.concatenate([-sin, sin], -1)


def _layer(x, l, mods, rows_per_mod, nb, seq, pw, rope, state, emit, final_g):
    sh_a, sc_a, g_a, sh_f, sc_f, g_f = mods
    tm_l, tm_s = min(1024, rows_per_mod), min(512, rows_per_mod)
    proj, gates = _norm_mm_call(x, pw["norm_mix_g"], sc_a, sh_a, rows_per_mod, pw["w_big"], l, tm_l, 768,
                                BF16, False, gates=(pw["w_g"], pw["b_g"]), name="in_proj")
    st_m = st_r = st_s = None
    if state is not None:
        st_m, st_r, st_s = state
    em_m = em_r = em_s = None
    if emit is not None:
        em_m, em_r, em_s = ((l, prev) for prev in emit)
    om = _mlstm_call(proj, gates, pw["m_norm_g"][l], nb, seq, state=st_m, emit=em_m)
    orr = _ret_call(proj, pw["lgr"][l], pw["r_norm_g"][l], nb, seq, rope=rope, state=st_r, emit=em_r)
    os_ = _ssd_call(proj, gates, pw["s_conv_w"][l], pw["s_conv_b"][l], pw["alog"][l], pw["sd"][l], nb, seq,
                    state=st_s, emit=em_s)
    merged = _merge_call(om[0], orr[0], os_[0], pw["s_norm_g"][l], pw["w_br_m"], pw["w_br_r"], pw["w_br_s"], l,
                         proj, tm_l, 512)
    x = _mm_resid_call(merged, pw["w_out"], l, x, g_a, rows_per_mod, tm_s, D_MODEL, name="out_proj")
    (hid,) = _norm_mm_call(x, pw["norm_mlp_g"], sc_f, sh_f, rows_per_mod, pw["w_ff1"], l, tm_l, 1024,
                           BF16, True, name="mlp_up")
    x = _mm_resid_call(hid, pw["w_ff2"], l, x, g_f, rows_per_mod, tm_s, 512, final_g=final_g, name="mlp_down")
    new_state = (om[1:], orr[1:], os_[1:]) if emit is not None else None
    return x, new_state


def kernel(x_prompt, x_sample, c, state_mlstm_C, state_mlstm_n, state_mlstm_m, state_ret, state_ssd, c_ctx, w_mod, b_mod, norm_mix_g, norm_mlp_g, w_in, m_igate_b, m_fgate_b, m_norm_g, r_decay, r_norm_g, s_conv_w, s_conv_b, s_dt_bias, s_a_log, s_d, s_norm_g, w_br_m, w_br_r, w_br_s, w_out, w_ff1, w_ff2, final_norm_g):
    bp, lp, d = x_prompt.shape
    bs, ls, _ = x_sample.shape
    xp = x_prompt.reshape(bp * lp, d)
    xs = x_sample.reshape(bs * ls, d)

    c_rows = jnp.zeros((8, d), F32).at[:bs].set(c).at[bs].set(c_ctx)
    mod = _mod_call(c_rows, w_mod, b_mod)
    rope = _rope_tables(ls)
    final_g = final_norm_g.reshape(1, d)

    w_big, w_g, b_g, alog, lgr, sd = _prep_params(w_in, m_igate_b, m_fgate_b, s_dt_bias, s_a_log, r_decay, s_d)
    pw = dict(w_big=w_big, w_g=w_g, b_g=b_g, alog=alog, lgr=lgr, sd=sd,
              norm_mix_g=norm_mix_g.reshape(DEPTH, 1, d), norm_mlp_g=norm_mlp_g.reshape(DEPTH, 1, d),
              m_norm_g=m_norm_g.reshape(DEPTH, 1, -1), r_norm_g=r_norm_g.reshape(DEPTH, 1, -1),
              s_norm_g=s_norm_g.reshape(DEPTH, 1, -1),
              s_conv_w=s_conv_w, s_conv_b=s_conv_b.reshape(DEPTH, 1, -1),
              w_br_m=w_br_m.astype(BF16), w_br_r=w_br_r.astype(BF16), w_br_s=w_br_s.astype(BF16),
              w_out=w_out.astype(BF16), w_ff1=w_ff1.astype(BF16), w_ff2=w_ff2.astype(BF16))
    cache_c = state_mlstm_C
    cache_r = state_ret
    cache_s = state_ssd.reshape(bs, DEPTH, 2, S_GROUPS, S_GW, S_N)

    big = (None, None, None)
    st_n, st_m = [], []
    for l in range(DEPTH):
        parts = mod[l].reshape(8, 6, 1, d)
        mods_ctx = tuple(parts[bs:bs + 1, i] for i in range(6))
        mods_lat = tuple(parts[:bs, i] for i in range(6))
        fg = final_g if l == DEPTH - 1 else None

        xp, st = _layer(xp, l, mods_ctx, bp * lp, bp, lp, pw, None, None, big, fg)
        (cf, nf, mf), (rf,), (hf,) = st
        big = (cf, rf, hf)
        st_n.append(jnp.transpose(nf, (0, 2, 1, 3)))
        st_m.append(jnp.transpose(mf[..., 0], (0, 2, 1)))

        cache = (
            (l, cache_c,
             jnp.transpose(state_mlstm_n[:, l], (0, 2, 1, 3)),
             jnp.broadcast_to(jnp.transpose(state_mlstm_m[:, l], (0, 2, 1))[..., None], (bs, M_HEADS, 2, LANES))),
            (l, cache_r),
            (l, cache_s),
        )
        xs, _ = _layer(xs, l, mods_lat, ls, bs, ls, pw, rope, cache, None, fg)

    return (xp.reshape(bp, lp, d), xs.reshape(bs, ls, d),
            big[0], jnp.stack(st_n, 1), jnp.stack(st_m, 1), big[1],
            big[2].reshape(bp, DEPTH, 2, S_HEADS, S_P, S_N))
```

```python
import functools

import numpy as np
import jax
import jax.numpy as jnp
from jax import lax
from jax.experimental import pallas as pl
from jax.experimental.pallas import tpu as pltpu

F32 = jnp.float32
BF16 = jnp.bfloat16

D_MODEL = 2048
DEPTH = 2
CHUNK = 128
EPS = 1e-6
M_HEADS, M_DK = 4, 256
R_HEADS, R_DK = 8, 128
S_HEADS, S_P, S_GROUPS, S_N = 16, 64, 2, 128
S_HPG = S_HEADS // S_GROUPS
S_GW = S_HPG * S_P
GRID_W = 64
ROPE_BASE = 10000.0
ROPE_FREQS = R_DK // 4
D_FF = 4 * D_MODEL
LANES = 128
VMEM_LIMIT = 56 * 1024 * 1024
SCAN_UNROLL = 2

OFF_MQ, OFF_MK, OFF_MV, OFF_MO = 0, 1024, 2048, 3072
OFF_RQ, OFF_RK, OFF_RV, OFF_RG = 4096, 5120, 6144, 7168
OFF_SZ, OFF_SX, OFF_SB, OFF_SC = 8192, 9216, 10240, 10496
N_MIX = 10752
N_GATES = (M_HEADS + S_GROUPS) * LANES
GATE_LANE_OFFSETS = tuple(4 * h for h in range(M_HEADS)) + tuple(4 * M_HEADS + 2 * S_HPG * g for g in range(S_GROUPS))


def _cparams(n_axes):
    return pltpu.CompilerParams(dimension_semantics=("arbitrary",) * n_axes,
                                vmem_limit_bytes=VMEM_LIMIT)


def _dot(a, b):
    return jnp.dot(a, b, preferred_element_type=F32)


def _dot_nt(a, b):
    return lax.dot_general(a, b, (((1,), (1,)), ((), ())), preferred_element_type=F32)


def _split2(x):
    hi = x.astype(BF16)
    lo = (x - hi.astype(F32)).astype(BF16)
    return hi, lo


def _cumsum_rows(x, tri):
    hi, lo = _split2(x)
    return _dot(tri, hi) + _dot(tri, lo)


def _expand(x, e, exact=True):
    if not exact:
        return _dot(x.astype(BF16), e)
    hi, lo = _split2(x)
    return _dot(hi, e) + _dot(lo, e)


def _softplus(x):
    return jnp.maximum(x, 0.0) + jnp.log1p(jnp.exp(-jnp.abs(x)))


def _sigmoid(x):
    return 1.0 / (1.0 + jnp.exp(-x))


def _silu(x):
    return x * _sigmoid(x)


def _mod_kernel(c_ref, w_ref, b_ref, o_ref):
    a = _silu(c_ref[...]).astype(BF16)
    o_ref[...] = _dot(a, w_ref[...].astype(BF16)) + b_ref[...]


def _mod_call(c_rows, w_mod, b_mod):
    tn = 1024
    n = w_mod.shape[-1]
    return pl.pallas_call(
        _mod_kernel,
        grid=(DEPTH, n // tn),
        in_specs=[pl.BlockSpec((8, D_MODEL), lambda l, j: (0, 0)),
                  pl.BlockSpec((None, D_MODEL, tn), lambda l, j: (l, 0, j)),
                  pl.BlockSpec((None, 1, tn), lambda l, j: (l, 0, j))],
        out_specs=pl.BlockSpec((None, 8, tn), lambda l, j: (l, 0, j)),
        out_shape=jax.ShapeDtypeStruct((DEPTH, 8, n), F32),
        compiler_params=_cparams(2),
        name="mod_proj",
    )(c_rows, w_mod, b_mod.reshape(DEPTH, 1, n))


def _norm_mm_kernel(*refs, with_gates, relu2):
    if with_gates:
        x_ref, g_ref, sc_ref, sh_ref, w_ref, wg_ref, bg_ref, o_ref, og_ref, h_scr = refs
    else:
        x_ref, g_ref, sc_ref, sh_ref, w_ref, o_ref, h_scr = refs

    @pl.when(pl.program_id(1) == 0)
    def _():
        x = x_ref[...]
        ms = jnp.mean(x * x, axis=-1, keepdims=True)
        y = x * lax.rsqrt(ms + EPS) * g_ref[...]
        hb = (y * (1.0 + sc_ref[...]) + sh_ref[...]).astype(BF16)
        h_scr[...] = hb
        if with_gates:
            pre = _dot(hb, wg_ref[...]) + bg_ref[...]
            lane = lax.broadcasted_iota(jnp.int32, pre.shape, 1)
            act = jnp.where(lane < 4 * M_HEADS,
                            jnp.where(lane % 4 < 2, pre, -_softplus(-pre)),
                            _softplus(pre))
            for blk, off in enumerate(GATE_LANE_OFFSETS):
                own = 4 if blk < M_HEADS else 2 * S_HPG
                v = act if off == 0 else pltpu.roll(act, LANES - off, 1)
                og_ref[:, blk * LANES:(blk + 1) * LANES] = jnp.where(lane < own, v, 0.0)

    acc = _dot(h_scr[...], w_ref[...].astype(BF16))
    if relu2:
        r = jnp.maximum(acc, 0.0)
        acc = r * r
    o_ref[...] = acc.astype(o_ref.dtype)


def _norm_mm_call(x, g, sc, sh, rows_per_mod, w, layer, tm, tn, out_dtype, relu2, gates=None, name="norm_mm"):
    t, d = x.shape
    n = w.shape[-1]
    mod_idx = lambda i, j: ((i * tm) // rows_per_mod, 0, 0)
    in_specs = [pl.BlockSpec((tm, d), lambda i, j: (i, 0)),
                pl.BlockSpec((None, 1, d), lambda i, j: (layer, 0, 0)),
                pl.BlockSpec((None, 1, d), mod_idx),
                pl.BlockSpec((None, 1, d), mod_idx),
                pl.BlockSpec((None, d, tn), lambda i, j: (layer, 0, j))]
    args = [x, g, sc, sh, w]
    out_specs = [pl.BlockSpec((tm, tn), lambda i, j: (i, j))]
    out_shape = [jax.ShapeDtypeStruct((t, n), out_dtype)]
    if gates is not None:
        wg, bg = gates
        in_specs += [pl.BlockSpec((None, d, LANES), lambda i, j: (layer, 0, 0)),
                     pl.BlockSpec((None, 1, LANES), lambda i, j: (layer, 0, 0))]
        args += [wg, bg]
        out_specs.append(pl.BlockSpec((tm, N_GATES), lambda i, j: (i, 0)))
        out_shape.append(jax.ShapeDtypeStruct((t, N_GATES), F32))
    return pl.pallas_call(
        functools.partial(_norm_mm_kernel, with_gates=gates is not None, relu2=relu2),
        grid=(t // tm, n // tn),
        in_specs=in_specs, out_specs=out_specs, out_shape=out_shape,
        scratch_shapes=[pltpu.VMEM((tm, d), BF16)],
        compiler_params=_cparams(2),
        name=name,
    )(*args)


def _mm_resid_kernel(*refs, nj, tn, final_norm):
    if final_norm:
        a_ref, w_ref, r_ref, gate_ref, fg_ref, o_ref = refs
    else:
        a_ref, w_ref, r_ref, gate_ref, o_ref = refs
    j = pl.program_id(1)
    cols = pl.ds(pl.multiple_of(j * tn, tn), tn)
    o_ref[:, cols] = r_ref[...] + gate_ref[...] * _dot(a_ref[...], w_ref[...])
    if final_norm:
        @pl.when(j == nj - 1)
        def _():
            xn = o_ref[...]
            ms = jnp.mean(xn * xn, axis=-1, keepdims=True)
            o_ref[...] = xn * lax.rsqrt(ms + EPS) * fg_ref[...]


def _mm_resid_call(a, w, layer, resid, gate, rows_per_mod, tm, tn, final_g=None, name="mm_resid"):
    t, kdim = a.shape
    n = w.shape[-1]
    nj = n // tn
    mod_idx = lambda i, j: ((i * tm) // rows_per_mod, 0, j)
    in_specs = [pl.BlockSpec((tm, kdim), lambda i, j: (i, 0)),
                pl.BlockSpec((None, kdim, tn), lambda i, j: (layer, 0, j)),
                pl.BlockSpec((tm, tn), lambda i, j: (i, j)),
                pl.BlockSpec((None, 1, tn), mod_idx)]
    args = [a, w, resid, gate]
    if final_g is not None:
        in_specs.append(pl.BlockSpec((1, n), lambda i, j: (0, 0)))
        args.append(final_g)
    return pl.pallas_call(
        functools.partial(_mm_resid_kernel, nj=nj, tn=tn, final_norm=final_g is not None),
        grid=(t // tm, nj),
        in_specs=in_specs,
        out_specs=pl.BlockSpec((tm, n), lambda i, j: (i, 0)),
        out_shape=jax.ShapeDtypeStruct((t, n), F32),
        compiler_params=_cparams(2),
        name=name,
    )(*args)


def _merge_kernel(ym_ref, yr_ref, ts_ref, sg_ref, wm_ref, wr_ref, ws_ref, g0_ref, g1_ref, g2_ref,
                  o_ref, ys_scr):
    @pl.when(pl.program_id(1) == 0)
    def _():
        t = ts_ref[...]
        ms = jnp.mean(t * t, axis=-1, keepdims=True)
        ys_scr[...] = (t * lax.rsqrt(ms + EPS) * sg_ref[...]).astype(BF16)

    acc = _sigmoid(g0_ref[...].astype(F32)) * _dot(ym_ref[...], wm_ref[...])
    acc += _sigmoid(g1_ref[...].astype(F32)) * _dot(yr_ref[...], wr_ref[...])
    acc += _sigmoid(g2_ref[...].astype(F32)) * _dot(ys_scr[...], ws_ref[...])
    o_ref[...] = acc.astype(o_ref.dtype)


def _merge_call(ym, yr, ts, sg, wm, wr, ws, layer, proj, tm, tn):
    t, w = ym.shape
    d = wm.shape[-1]
    gl_blk = N_MIX // tn
    per_br = d // tn
    yspec = pl.BlockSpec((tm, w), lambda i, j: (i, 0))
    wspec = pl.BlockSpec((None, w, tn), lambda i, j: (layer, 0, j))
    gspec = lambda br: pl.BlockSpec((tm, tn), lambda i, j: (i, gl_blk + br * per_br + j))
    return pl.pallas_call(
        _merge_kernel,
        grid=(t // tm, d // tn),
        in_specs=[yspec, yspec, yspec, pl.BlockSpec((1, w), lambda i, j: (0, 0)),
                  wspec, wspec, wspec, gspec(0), gspec(1), gspec(2)],
        out_specs=pl.BlockSpec((tm, tn), lambda i, j: (i, j)),
        out_shape=jax.ShapeDtypeStruct((t, d), BF16),
        scratch_shapes=[pltpu.VMEM((tm, w), BF16)],
        compiler_params=_cparams(2),
        name="merge",
    )(ym, yr, ts, sg, wm, wr, ws, proj, proj, proj)


def _chunk_masks():
    row = lax.broadcasted_iota(jnp.int32, (CHUNK, CHUNK), 0)
    col = lax.broadcasted_iota(jnp.int32, (CHUNK, CHUNK), 1)
    lower = col <= row
    upper = col >= row
    tri = jnp.where(lower, 1.0, 0.0).astype(BF16)
    return row, col, lower, upper, tri


def _mlstm_kernel(*refs, nc, has_state, emit_state, alias_in):
    it = iter(refs)
    q_ref, k_ref, v_ref, o_ref, g_ref, ng_ref = (next(it) for _ in range(6))
    if has_state:
        c0_ref, n0_ref, m0_ref = (next(it) for _ in range(3))
    if alias_in:
        next(it)
    y_ref = next(it)
    if emit_state:
        cf_ref, nf_ref, mf_ref = (next(it) for _ in range(3))
    hacc = (next(it), next(it))
    c_scr = (next(it), next(it))
    n_scr = (next(it), next(it))
    m_scr = (next(it), next(it))
    p_scr = (next(it), next(it))
    ml_scr = (next(it), next(it))
    bc_scr = (next(it), next(it))
    wl_scr = (next(it), next(it))
    sc_scr = (next(it), next(it))

    _, _, lower, upper, tri = _chunk_masks()
    ones_bf = jnp.ones((CHUNK, LANES), BF16)
    for d in range(2):
        if has_state:
            c_scr[d][...] = c0_ref[d]
            n_scr[d][0:1, :] = n0_ref[d:d + 1, :]
            m_scr[d][0:1, :] = m0_ref[d:d + 1, :]
        else:
            c_scr[d][...] = jnp.zeros_like(c_scr[d])
            n_scr[d][...] = jnp.zeros_like(n_scr[d])
            m_scr[d][...] = jnp.zeros_like(m_scr[d])

    dirs = (0, 1)
    masks = (lower, upper)
    sub8 = lax.broadcasted_iota(jnp.int32, (8, LANES), 0)

    def gates(i, carry):
        cidx = (2 * i, 2 * i + 1)
        two = (0, 1)
        rows = [pl.ds(pl.multiple_of(c * CHUNK, CHUNK), CHUNK) for c in cidx]
        g = [g_ref[r, :] for r in rows]
        cs = [_cumsum_rows(x, tri) for x in g]
        tot = [x[CHUNK - 1:CHUNK, :] for x in cs]
        g_t = [x.T for x in g]
        cs_t = [x.T for x in cs]
        tot_t = [x[:, CHUNK - 1:CHUNK] for x in cs_t]
        bsum = [[cs[j], tot[j] - cs[j] + g[j]] for j in two]
        bsum_t = [[cs_t[j], tot_t[j] - cs_t[j] + g_t[j]] for j in two]
        items = [(j, d) for j in two for d in dirs]
        b_col = {(j, d): bsum[j][d][:, 2 + d:3 + d] for (j, d) in items}
        i_col = {(j, d): g[j][:, d:d + 1] for (j, d) in items}
        t_row = {(j, d): g_t[j][d:d + 1, :] - bsum_t[j][d][2 + d:3 + d, :] for (j, d) in items}
        b_last = {(j, d): tot[j][:, 2 + d:3 + d] for (j, d) in items}
        logw = {(j, d): jnp.where(masks[d], b_col[j, d] + t_row[j, d], -jnp.inf) for (j, d) in items}
        m_loc = {it_: jnp.max(logw[it_], axis=1, keepdims=True) for it_ in items}
        m_chunk = {it_: jnp.max(b_last[it_] + t_row[it_], axis=1, keepdims=True) for it_ in items}
        for (j, d) in items:
            p_scr[d][rows[j], :] = jnp.exp(logw[j, d] - m_loc[j, d])
            ml_scr[d][rows[j], :] = jnp.broadcast_to(m_loc[j, d], (CHUNK, LANES))
            bc_scr[d][rows[j], :] = jnp.broadcast_to(b_col[j, d], (CHUNK, LANES))
            wl_scr[d][rows[j], :] = jnp.broadcast_to(b_last[j, d] - b_col[j, d] + i_col[j, d], (CHUNK, LANES))
            sc_scr[d][pl.ds(pl.multiple_of(cidx[j] * 8, 8), 8), :] = jnp.where(
                sub8 == 0, jnp.broadcast_to(m_chunk[j, d], (8, LANES)), jnp.broadcast_to(b_last[j, d], (8, LANES)))
        return carry

    def rep2(x):
        return jnp.concatenate([x, x], axis=1)

    def scan(i, carry):
        cidx = (i, nc - 1 - i)
        rows = [pl.ds(pl.multiple_of(c * CHUNK, CHUNK), CHUNK) for c in cidx]
        sc = [sc_scr[d][pl.ds(pl.multiple_of(cidx[d] * 8, 8), 8), :] for d in dirs]
        m_chunk = [sc[d][0:1, 0:1] for d in dirs]
        b_last = [sc[d][1:2, 0:1] for d in dirs]
        qb = [q_ref[r, :].astype(BF16) for r in rows]
        k = [k_ref[r, :].astype(F32) * (M_DK ** -0.5) for r in rows]
        kb = [x.astype(BF16) for x in k]
        vb = [v_ref[r, :].astype(BF16) for r in rows]
        qk = [_dot_nt(qb[d], kb[d]) for d in dirs]
        sb = [(qk[d] * p_scr[d][rows[d], :]).astype(BF16) for d in dirs]
        a_num = [_dot(sb[d], vb[d]) for d in dirs]
        a_den = [_dot(sb[d], ones_bf) for d in dirs]

        m_prev = [m_scr[d][0:1, 0:1] for d in dirs]
        c_prev = [c_scr[d][...] for d in dirs]
        n_prev = [n_scr[d][0:1, :] for d in dirs]
        m_new = [jnp.maximum(b_last[d] + m_prev[d], m_chunk[d]) for d in dirs]
        kw = [k[d] * rep2(jnp.exp(wl_scr[d][rows[d], :] - m_new[d])) for d in dirs]
        kw_t = [x.T.astype(BF16) for x in kw]
        q_c = [_dot(qb[d], c_prev[d].astype(BF16)) for d in dirs]
        u_c = [_dot(kw_t[d], vb[d]) for d in dirs]
        m_loc = [ml_scr[d][rows[d], :] for d in dirs]
        gg = [bc_scr[d][rows[d], :] + m_prev[d] for d in dirs]
        m_tot = [jnp.maximum(m_loc[d], gg[d]) for d in dirs]
        e_intra = [jnp.exp(m_loc[d] - m_tot[d]) for d in dirs]
        e_inter = [jnp.exp(gg[d] - m_tot[d]) for d in dirs]
        q_n = [_dot_nt(qb[d], jnp.broadcast_to(n_prev[d].astype(BF16), (LANES, M_DK))) for d in dirs]
        den = [e_intra[d] * a_den[d] + e_inter[d] * q_n[d] for d in dirs]
        inv = [1.0 / jnp.maximum(jnp.abs(den[d]), jnp.exp(-m_tot[d])) for d in dirs]
        for d in dirs:
            hacc[d][rows[d], :] = rep2(e_intra[d] * inv[d]) * a_num[d] + rep2(e_inter[d] * inv[d]) * q_c[d]

        u_n = [jnp.sum(x, axis=0, keepdims=True) for x in kw]
        a = [jnp.exp(b_last[d] + m_prev[d] - m_new[d]) for d in dirs]
        for d in dirs:
            c_scr[d][...] = a[d] * c_prev[d] + u_c[d]
            n_scr[d][0:1, :] = a[d] * n_prev[d] + u_n[d]
            m_scr[d][0:1, :] = jnp.broadcast_to(m_new[d], (1, LANES))
        return carry

    def finish(c, carry):
        rows = pl.ds(pl.multiple_of(c * CHUNK, CHUNK), CHUNK)
        ht = hacc[0][rows, :] + hacc[1][rows, :]
        ms = jnp.mean(ht * ht, axis=1, keepdims=True)
        y = ht * lax.rsqrt(ms + EPS) * ng_ref[...] * _sigmoid(o_ref[rows, :].astype(F32))
        y_ref[rows, :] = y.astype(y_ref.dtype)
        return carry

    lax.fori_loop(0, nc // 2, gates, 0)
    lax.fori_loop(0, nc, scan, 0, unroll=SCAN_UNROLL)
    lax.fori_loop(0, nc, finish, 0, unroll=SCAN_UNROLL)
    if emit_state:
        for d in range(2):
            cf_ref[d] = c_scr[d][...]
            nf_ref[d:d + 1, :] = n_scr[d][0:1, :]
            mf_ref[d:d + 1, :] = m_scr[d][0:1, :]


def _mlstm_call(proj, gates, ng, nb, seq, state=None, emit=None):
    emit_state = emit is not None
    t = proj.shape[0]
    nc = seq // CHUNK
    dk = M_DK
    cspec = lambda base: pl.BlockSpec((seq, dk), lambda b, h: (b, base // dk + h))
    in_specs = [cspec(OFF_MQ), cspec(OFF_MK), cspec(OFF_MV), cspec(OFF_MO),
                pl.BlockSpec((seq, LANES), lambda b, h: (b, h)),
                pl.BlockSpec((1, dk), lambda b, h: (0, h))]
    args = [proj, proj, proj, proj, gates, ng]
    if state is not None:
        sl = state[0]
        in_specs += [pl.BlockSpec((None, None, 2, None, dk, dk), lambda b, h: (b, sl, 0, h, 0, 0)),
                     pl.BlockSpec((None, None, 2, dk), lambda b, h: (b, h, 0, 0)),
                     pl.BlockSpec((None, None, 2, LANES), lambda b, h: (b, h, 0, 0))]
        args += list(state[1:])
    out_specs = [pl.BlockSpec((seq, dk), lambda b, h: (b, h))]
    out_shape = [jax.ShapeDtypeStruct((t, M_HEADS * dk), BF16)]
    aliases = {}
    if emit_state:
        layer, prev = emit
        out_specs += [pl.BlockSpec((None, None, 2, None, dk, dk), lambda b, h: (b, layer, 0, h, 0, 0)),
                      pl.BlockSpec((None, None, 2, dk), lambda b, h: (b, h, 0, 0)),
                      pl.BlockSpec((None, None, 2, LANES), lambda b, h: (b, h, 0, 0))]
        out_shape += [jax.ShapeDtypeStruct((nb, DEPTH, 2, M_HEADS, dk, dk), F32),
                      jax.ShapeDtypeStruct((nb, M_HEADS, 2, dk), F32),
                      jax.ShapeDtypeStruct((nb, M_HEADS, 2, LANES), F32)]
        if prev is not None:
            in_specs.append(pl.BlockSpec(memory_space=pl.ANY))
            args.append(prev)
            aliases = {len(args) - 1: 1}
    return pl.pallas_call(
        functools.partial(_mlstm_kernel, nc=nc, has_state=state is not None, emit_state=emit_state,
                          alias_in=bool(aliases)),
        grid=(nb, M_HEADS),
        in_specs=in_specs, out_specs=out_specs, out_shape=out_shape, input_output_aliases=aliases,
        scratch_shapes=([pltpu.VMEM((seq, dk), F32)] * 2 + [pltpu.VMEM((dk, dk), F32)] * 2
                        + [pltpu.VMEM((8, dk), F32)] * 2 + [pltpu.VMEM((8, LANES), F32)] * 2
                        + [pltpu.VMEM((seq, LANES), F32)] * 8 + [pltpu.VMEM((nc * 8, LANES), F32)] * 2),
        compiler_params=_cparams(2),
        name="mlstm",
    )(*args)


R_HPS = 2


def _ret_kernel(*refs, nc, has_state, emit_state, rope, alias_in):
    it = iter(refs)
    q_ref, k_ref, v_ref, g_ref, lg_ref, ng_ref = (next(it) for _ in range(6))
    if rope:
        cos_ref, sin_ref = next(it), next(it)
    if has_state:
        s0_ref = next(it)
    if alias_in:
        next(it)
    y_ref = next(it)
    if emit_state:
        sf_ref = next(it)
    chains = [(h, d) for h in range(R_HPS) for d in range(2)]
    oacc = {c: next(it) for c in chains}
    s_scr = {c: next(it) for c in chains}
    qs_scr, ks_scr, kt_scr = (next(it) for _ in range(3))

    row, col, lower, upper, _ = _chunk_masks()
    rel = (row - col).astype(F32)
    pos = row.astype(F32)
    for (h, d) in chains:
        if has_state:
            s_scr[h, d][...] = s0_ref[d, h]
        else:
            s_scr[h, d][...] = jnp.zeros_like(s_scr[h, d])

    decay, w_q, w_k, chunk_decay = {}, {}, {}, {}
    for (h, d) in chains:
        lgd = -jnp.exp(lg_ref[h, d:d + 1, :])
        if d == 0:
            decay[h, d] = jnp.where(lower, jnp.exp(rel * lgd), 0.0)
            w_q[h, d] = jnp.exp((pos + 1.0) * lgd)
            w_k[h, d] = jnp.exp((CHUNK - 1.0 - pos) * lgd)
        else:
            decay[h, d] = jnp.where(upper, jnp.exp(-rel * lgd), 0.0)
            w_q[h, d] = jnp.exp((CHUNK - pos) * lgd)
            w_k[h, d] = jnp.exp(pos * lgd)
        chunk_decay[h, d] = jnp.exp(CHUNK * lgd)

    def hcols(h):
        return slice(h * R_DK, (h + 1) * R_DK)

    def prep(c, carry):
        rows = pl.ds(pl.multiple_of(c * CHUNK, CHUNK), CHUNK)
        for h in range(R_HPS):
            q = q_ref[rows, hcols(h)].astype(F32)
            k = k_ref[rows, hcols(h)].astype(F32) * (R_DK ** -0.5)
            if rope:
                cs, sn = cos_ref[rows, :], sin_ref[rows, :]
                q = q * cs + pltpu.roll(q, R_DK // 2, 1) * sn
                k = k * cs + pltpu.roll(k, R_DK // 2, 1) * sn
            qs_scr[rows, hcols(h)] = q.astype(BF16)
            ks_scr[rows, hcols(h)] = k.astype(BF16)
            kt_scr[hcols(h), rows] = k.T.astype(BF16)
        return carry

    def scan(i, carry):
        rows = {0: pl.ds(pl.multiple_of(i * CHUNK, CHUNK), CHUNK),
                1: pl.ds(pl.multiple_of((nc - 1 - i) * CHUNK, CHUNK), CHUNK)}
        qb = {(h, d): qs_scr[rows[d], hcols(h)] for (h, d) in chains}
        kb = {(h, d): ks_scr[rows[d], hcols(h)] for (h, d) in chains}
        kt = {(h, d): kt_scr[hcols(h), rows[d]] for (h, d) in chains}
        vb = {(h, d): v_ref[rows[d], hcols(h)].astype(BF16) for (h, d) in chains}
        vw = {c: (vb[c].astype(F32) * w_k[c]).astype(BF16) for c in chains}
        s_prev = {c: s_scr[c][...] for c in chains}
        sc = {c: (_dot_nt(qb[c], kb[c]) * decay[c]).astype(BF16) for c in chains}
        inter = {c: _dot(qb[c], s_prev[c].astype(BF16)) for c in chains}
        u = {c: _dot(kt[c], vw[c]) for c in chains}
        intra = {c: _dot(sc[c], vb[c]) for c in chains}
        for (h, d) in chains:
            oacc[h, d][rows[d], :] = intra[h, d] + w_q[h, d] * inter[h, d]
            s_scr[h, d][...] = chunk_decay[h, d] * s_prev[h, d] + u[h, d]
        return carry

    def finish(c, carry):
        rows = pl.ds(pl.multiple_of(c * CHUNK, CHUNK), CHUNK)
        for h in range(R_HPS):
            ot = oacc[h, 0][rows, :] + oacc[h, 1][rows, :]
            ms = jnp.mean(ot * ot, axis=1, keepdims=True)
            y = ot * lax.rsqrt(ms + EPS) * ng_ref[:, hcols(h)] * _silu(g_ref[rows, hcols(h)].astype(F32))
            y_ref[rows, hcols(h)] = y.astype(y_ref.dtype)
        return carry

    lax.fori_loop(0, nc, prep, 0, unroll=SCAN_UNROLL)
    lax.fori_loop(0, nc, scan, 0, unroll=SCAN_UNROLL)
    lax.fori_loop(0, nc, finish, 0, unroll=SCAN_UNROLL)
    if emit_state:
        for (h, d) in chains:
            sf_ref[d, h] = s_scr[h, d][...]


def _ret_call(proj, lgr, ng, nb, seq, rope=None, state=None, emit=None):
    emit_state = emit is not None
    t = proj.shape[0]
    nc = seq // CHUNK
    dk = R_DK
    bw = R_HPS * dk
    cspec = lambda base: pl.BlockSpec((seq, bw), lambda b, h: (b, base // bw + h))
    in_specs = [cspec(OFF_RQ), cspec(OFF_RK), cspec(OFF_RV), cspec(OFF_RG),
                pl.BlockSpec((R_HPS, 2, LANES), lambda b, h: (h, 0, 0)),
                pl.BlockSpec((1, bw), lambda b, h: (0, h))]
    args = [proj, proj, proj, proj, lgr, ng]
    if rope is not None:
        in_specs += [pl.BlockSpec((seq, dk), lambda b, h: (0, 0))] * 2
        args += list(rope)
    if state is not None:
        sl = state[0]
        in_specs.append(pl.BlockSpec((None, None, 2, R_HPS, dk, dk), lambda b, h: (b, sl, 0, h, 0, 0)))
        args.append(state[1])
    out_specs = [pl.BlockSpec((seq, bw), lambda b, h: (b, h))]
    out_shape = [jax.ShapeDtypeStruct((t, R_HEADS * dk), BF16)]
    aliases = {}
    if emit_state:
        layer, prev = emit
        out_specs.append(pl.BlockSpec((None, None, 2, R_HPS, dk, dk), lambda b, h: (b, layer, 0, h, 0, 0)))
        out_shape.append(jax.ShapeDtypeStruct((nb, DEPTH, 2, R_HEADS, dk, dk), F32))
        if prev is not None:
            in_specs.append(pl.BlockSpec(memory_space=pl.ANY))
            args.append(prev)
            aliases = {len(args) - 1: 1}
    n_chain = 2 * R_HPS
    return pl.pallas_call(
        functools.partial(_ret_kernel, nc=nc, has_state=state is not None, emit_state=emit_state,
                          rope=rope is not None, alias_in=bool(aliases)),
        grid=(nb, R_HEADS // R_HPS),
        in_specs=in_specs, out_specs=out_specs, out_shape=out_shape, input_output_aliases=aliases,
        scratch_shapes=([pltpu.VMEM((seq, dk), F32)] * n_chain + [pltpu.VMEM((dk, dk), F32)] * n_chain
                        + [pltpu.VMEM((seq, bw), BF16)] * 2 + [pltpu.VMEM((bw, seq), BF16)]),
        compiler_params=_cparams(2),
        name="retention",
    )(*args)


def _ssd_kernel(*refs, nc, has_state, emit_state, alias_in):
    it = iter(refs)
    (x_ref, b_ref, c_ref, z_ref, g_ref, wx_ref, wb_ref, wc_ref, bx_ref, bb_ref, bc_ref,
     al_ref, sd_ref) = (next(it) for _ in range(13))
    if has_state:
        h0_ref = next(it)
    if alias_in:
        next(it)
    y_ref = next(it)
    if emit_state:
        hf_ref = next(it)
    xs_scr, bt_scr, bb_scr, cc_scr, ht_scr, yb_scr = (next(it) for _ in range(6))

    _, _, lower, upper, tri = _chunk_masks()
    rowi = lax.broadcasted_iota(jnp.int32, (CHUNK, 1), 0)
    er = lax.broadcasted_iota(jnp.int32, (LANES, S_GW), 0)
    ec = lax.broadcasted_iota(jnp.int32, (LANES, S_GW), 1) // S_P
    expand = [jnp.where(er == ec + d * S_HPG, 1.0, 0.0).astype(BF16) for d in range(2)]
    a_row = -jnp.exp(al_ref[...])

    wr = lax.broadcasted_iota(jnp.int32, (CHUNK, CHUNK + 32), 0)
    wc = lax.broadcasted_iota(jnp.int32, (CHUNK, CHUNK + 32), 1)
    shifts = [jnp.where(wc == wr + 16 + sft, 1.0, 0.0).astype(BF16) for sft in (-1, 1, 2)]

    def conv_silu(ref, w_ref, bias_ref, c, rows):
        off = c * CHUNK
        cur = ref[rows, :]
        prev_w = ref[pl.ds(pl.multiple_of(jnp.maximum(off - 16, 0), 16), 16), :]
        next_w = ref[pl.ds(pl.multiple_of(jnp.minimum(off + CHUNK, nc * CHUNK - 16), 16), 16), :]
        prev_w = jnp.where(c > 0, prev_w, jnp.zeros_like(prev_w))
        next_w = jnp.where(c < nc - 1, next_w, jnp.zeros_like(next_w))
        win = jnp.concatenate([prev_w, cur, next_w], axis=0)
        xm1, xp1, xp2 = (_dot(sm, win) for sm in shifts)
        w = w_ref[...]
        y = w[0:1, :] * xm1 + w[1:2, :] * cur.astype(F32) + w[2:3, :] * xp1 + w[3:4, :] * xp2 + bias_ref[...]
        return _silu(y)

    def prep(c, carry):
        off = pl.multiple_of(c * CHUNK, CHUNK)
        rows = pl.ds(off, CHUNK)
        xs_scr[rows, :] = conv_silu(x_ref, wx_ref, bx_ref, c, rows)
        bm = conv_silu(b_ref, wb_ref, bb_ref, c, rows)
        bb_scr[rows, :] = bm.astype(BF16)
        bt_scr[:, rows] = bm.T.astype(BF16)
        cc_scr[rows, :] = conv_silu(c_ref, wc_ref, bc_ref, c, rows).astype(BF16)
        return carry

    lax.fori_loop(0, nc, prep, 0, unroll=SCAN_UNROLL)

    for d in range(2):
        if has_state:
            ht_scr[d] = h0_ref[d].T
        else:
            ht_scr[d] = jnp.zeros((S_N, S_GW), F32)

    dirs = (0, 1)
    masks = (lower, upper)
    y_out = (y_ref, yb_scr)
    first_half = lax.broadcasted_iota(jnp.int32, (CHUNK, LANES), 1) < S_P

    def scan(i, carry):
        rows = [pl.ds(pl.multiple_of(c * CHUNK, CHUNK), CHUNK) for c in (i, nc - 1 - i)]
        dt = [g_ref[r, :] for r in rows]
        a = [x * a_row for x in dt]
        cs = [_cumsum_rows(x, tri) for x in a]
        tot = [x[CHUNK - 1:CHUNK, :] for x in cs]
        bsum = [cs[0], tot[1] - cs[1] + a[1]]
        bsum_t = [x.T for x in bsum]
        dt_t = [x.T for x in dt]
        xs = [xs_scr[r, :] for r in rows]
        xb = [x.astype(BF16) for x in xs]
        bmb = [bb_scr[r, :] for r in rows]
        cmb = [cc_scr[r, :] for r in rows]
        btb = [bt_scr[:, r] for r in rows]
        cb = [_dot_nt(cmb[d], bmb[d]) for d in dirs]
        ht = [ht_scr[d] for d in dirs]
        inter = [_dot(cmb[d], ht[d].astype(BF16)) for d in dirs]
        ex = [_expand(jnp.exp(bsum[d]), expand[d], exact=False) for d in dirs]
        wexp = [_expand(jnp.exp(tot[d] - bsum[d]) * dt[d], expand[d], exact=False) for d in dirs]
        cd = [_expand(jnp.broadcast_to(jnp.exp(tot[d]), (8, LANES)), expand[d])[0:1, :] for d in dirs]
        u_t = [_dot(btb[d], (xs[d] * wexp[d]).astype(BF16)) for d in dirs]
        ys = [[], []]
        for p in range(S_HPG // 2):
            for d in dirs:
                ms = []
                for k in (2 * p, 2 * p + 1):
                    ln = d * S_HPG + k
                    decay = jnp.exp(jnp.where(masks[d], bsum[d][:, ln:ln + 1] - bsum_t[d][ln:ln + 1, :], -jnp.inf))
                    ms.append((cb[d] * decay * dt_t[d][ln:ln + 1, :]).astype(BF16))
                xp = xb[d][:, p * LANES:(p + 1) * LANES]
                zero = jnp.zeros_like(xp)
                rhs = jnp.concatenate([jnp.where(first_half, xp, zero), jnp.where(first_half, zero, xp)], axis=0)
                ys[d].append(_dot(jnp.concatenate(ms, axis=1), rhs))
        for d in dirs:
            y_out[d][rows[d], :] = jnp.concatenate(ys[d], axis=1) + ex[d] * inter[d]
            ht_scr[d] = ht[d] * cd[d] + u_t[d]
        return carry

    def finish(c, carry):
        rows = pl.ds(pl.multiple_of(c * CHUNK, CHUNK), CHUNK)
        yt = y_ref[rows, :] + yb_scr[rows, :] + sd_ref[...] * xs_scr[rows, :]
        y_ref[rows, :] = yt * _silu(z_ref[rows, :].astype(F32))
        return carry

    lax.fori_loop(0, nc, scan, 0, unroll=SCAN_UNROLL)
    lax.fori_loop(0, nc, finish, 0, unroll=SCAN_UNROLL)
    if emit_state:
        for d in range(2):
            hf_ref[d] = ht_scr[d].T


def _ssd_call(proj, gates, conv_w, conv_b, alog, sd, nb, seq, state=None, emit=None):
    emit_state = emit is not None
    t = proj.shape[0]
    nc = seq // CHUNK
    gw = S_GW
    n = S_N
    xw = S_HEADS * S_P
    in_specs = [pl.BlockSpec((seq, gw), lambda b, g: (b, OFF_SX // gw + g)),
                pl.BlockSpec((seq, n), lambda b, g: (b, OFF_SB // n + g)),
                pl.BlockSpec((seq, n), lambda b, g: (b, OFF_SC // n + g)),
                pl.BlockSpec((seq, gw), lambda b, g: (b, OFF_SZ // gw + g)),
                pl.BlockSpec((seq, LANES), lambda b, g: (b, M_HEADS + g)),
                pl.BlockSpec((4, gw), lambda b, g: (0, g)),
                pl.BlockSpec((4, n), lambda b, g: (0, xw // n + g)),
                pl.BlockSpec((4, n), lambda b, g: (0, xw // n + S_GROUPS + g)),
                pl.BlockSpec((1, gw), lambda b, g: (0, g)),
                pl.BlockSpec((1, n), lambda b, g: (0, xw // n + g)),
                pl.BlockSpec((1, n), lambda b, g: (0, xw // n + S_GROUPS + g)),
                pl.BlockSpec((None, 1, LANES), lambda b, g: (g, 0, 0)),
                pl.BlockSpec((1, gw), lambda b, g: (0, g))]
    args = [proj, proj, proj, proj, gates, conv_w, conv_w, conv_w, conv_b, conv_b, conv_b, alog, sd]
    if state is not None:
        sl = state[0]
        in_specs.append(pl.BlockSpec((None, None, 2, None, gw, n), lambda b, g: (b, sl, 0, g, 0, 0)))
        args.append(state[1])
    out_specs = [pl.BlockSpec((seq, gw), lambda b, g: (b, g))]
    out_shape = [jax.ShapeDtypeStruct((t, xw), F32)]
    aliases = {}
    if emit_state:
        layer, prev = emit
        out_specs.append(pl.BlockSpec((None, None, 2, None, gw, n), lambda b, g: (b, layer, 0, g, 0, 0)))
        out_shape.append(jax.ShapeDtypeStruct((nb, DEPTH, 2, S_GROUPS, gw, n), F32))
        if prev is not None:
            in_specs.append(pl.BlockSpec(memory_space=pl.ANY))
            args.append(prev)
            aliases = {len(args) - 1: 1}
    return pl.pallas_call(
        functools.partial(_ssd_kernel, nc=nc, has_state=state is not None, emit_state=emit_state,
                          alias_in=bool(aliases)),
        grid=(nb, S_GROUPS),
        in_specs=in_specs, out_specs=out_specs, out_shape=out_shape, input_output_aliases=aliases,
        scratch_shapes=[pltpu.VMEM((seq, gw), F32), pltpu.VMEM((n, seq), BF16),
                        pltpu.VMEM((seq, n), BF16), pltpu.VMEM((seq, n), BF16),
                        pltpu.VMEM((2, n, gw), F32), pltpu.VMEM((seq, gw), F32)],
        compiler_params=_cparams(2),
        name="ssd",
    )(*args)


_IN_SIZES = (1024, 1024, 1024, 1024, 8, 8, 1024, 1024, 1024, 1024, 1024, 1536, 32, 6144)
_IN_OFFS = np.concatenate([[0], np.cumsum(_IN_SIZES)])


def _gate_lane_index():
    idx = -np.ones((LANES,), np.int64)
    for h in range(M_HEADS):
        base = GATE_LANE_OFFSETS[h]
        idx[base + 0] = 0 * M_HEADS + h
        idx[base + 1] = 1 * M_HEADS + h
        idx[base + 2] = 2 * M_HEADS + 0 * M_HEADS + h
        idx[base + 3] = 2 * M_HEADS + 1 * M_HEADS + h
    for g in range(S_GROUPS):
        base = GATE_LANE_OFFSETS[M_HEADS + g]
        for d in range(2):
            for k in range(S_HPG):
                idx[base + d * S_HPG + k] = 4 * M_HEADS + d * S_HEADS + g * S_HPG + k
    return idx


def _place(vals, idx):
    taken = jnp.take(vals, jnp.asarray(np.maximum(idx, 0)), axis=-1)
    return jnp.where(jnp.asarray(idx >= 0), taken, 0.0)


def _prep_params(w_in, m_igate_b, m_fgate_b, s_dt_bias, s_a_log, r_decay, s_d):
    o = _IN_OFFS
    w_big = jnp.concatenate([w_in[:, :, o[0]:o[4]], w_in[:, :, o[6]:o[12]], w_in[:, :, o[13]:o[14]]],
                            axis=2).astype(BF16)
    w_small = jnp.concatenate([w_in[:, :, o[4]:o[6]], w_in[:, :, o[12]:o[13]]], axis=2)
    gidx = _gate_lane_index()
    w_g = _place(w_small, gidx).astype(BF16)
    b_small = jnp.concatenate([m_igate_b.reshape(DEPTH, -1), m_fgate_b.reshape(DEPTH, -1),
                               s_dt_bias.reshape(DEPTH, -1)], axis=1)
    b_g = _place(b_small, gidx).reshape(DEPTH, 1, LANES)
    aidx = -np.ones((S_GROUPS, LANES), np.int64)
    for g in range(S_GROUPS):
        for d in range(2):
            for k in range(S_HPG):
                aidx[g, d * S_HPG + k] = d * S_HEADS + g * S_HPG + k
    a_flat = s_a_log.reshape(DEPTH, -1)
    alog = jnp.stack([_place(a_flat, aidx[g]) for g in range(S_GROUPS)], axis=1).reshape(DEPTH, S_GROUPS, 1, LANES)
    lgr = jnp.broadcast_to(jnp.swapaxes(r_decay, 1, 2)[..., None], (DEPTH, R_HEADS, 2, LANES))
    sd = jnp.repeat(s_d, S_P, axis=1).reshape(DEPTH, 1, S_HEADS * S_P)
    return w_big, w_g, b_g, alog, lgr, sd


def _rope_tables(seq):
    n_rows = seq // GRID_W
    rows = jnp.repeat(jnp.arange(n_rows, dtype=F32), GRID_W)
    cols = jnp.tile(jnp.arange(GRID_W, dtype=F32), n_rows)
    inv = ROPE_BASE ** (-jnp.arange(ROPE_FREQS, dtype=F32) / ROPE_FREQS)
    ang = jnp.concatenate([rows[:, None] * inv, cols[:, None] * inv], -1)
    cos, sin = jnp.cos(ang), jnp.sin(ang)
    return jnp.concatenate([cos, cos], -1), jnp.concatenate([-sin, sin], -1)


def _layer(x, l, mods, rows_per_mod, nb, seq, pw, rope, state, emit, final_g):
    sh_a, sc_a, g_a, sh_f, sc_f, g_f = mods
    tm_l, tm_s = min(1024, rows_per_mod), min(512, rows_per_mod)
    proj, gates = _norm_mm_call(x, pw["norm_mix_g"], sc_a, sh_a, rows_per_mod, pw["w_big"], l, tm_l, 1536,
                                BF16, False, gates=(pw["w_g"], pw["b_g"]), name="in_proj")
    st_m = st_r = st_s = None
    if state is not None:
        st_m, st_r, st_s = state
    em_m = em_r = em_s = None
    if emit is not None:
        em_m, em_r, em_s = ((l, prev) for prev in emit)
    om = _mlstm_call(proj, gates, pw["m_norm_g"][l], nb, seq, state=st_m, emit=em_m)
    orr = _ret_call(proj, pw["lgr"][l], pw["r_norm_g"][l], nb, seq, rope=rope, state=st_r, emit=em_r)
    os_ = _ssd_call(proj, gates, pw["s_conv_w"][l], pw["s_conv_b"][l], pw["alog"][l], pw["sd"][l], nb, seq,
                    state=st_s, emit=em_s)
    merged = _merge_call(om[0], orr[0], os_[0], pw["s_norm_g"][l], pw["w_br_m"], pw["w_br_r"], pw["w_br_s"], l,
                         proj, tm_l, 512)
    x = _mm_resid_call(merged, pw["w_out"], l, x, g_a, rows_per_mod, tm_s, D_MODEL, name="out_proj")
    (hid,) = _norm_mm_call(x, pw["norm_mlp_g"], sc_f, sh_f, rows_per_mod, pw["w_ff1"], l, tm_l, 1024,
                           BF16, True, name="mlp_up")
    x = _mm_resid_call(hid, pw["w_ff2"], l, x, g_f, rows_per_mod, tm_s, 512, final_g=final_g, name="mlp_down")
    new_state = (om[1:], orr[1:], os_[1:]) if emit is not None else None
    return x, new_state


def kernel(x_prompt, x_sample, c, state_mlstm_C, state_mlstm_n, state_mlstm_m, state_ret, state_ssd, c_ctx, w_mod, b_mod, norm_mix_g, norm_mlp_g, w_in, m_igate_b, m_fgate_b, m_norm_g, r_decay, r_norm_g, s_conv_w, s_conv_b, s_dt_bias, s_a_log, s_d, s_norm_g, w_br_m, w_br_r, w_br_s, w_out, w_ff1, w_ff2, final_norm_g):
    bp, lp, d = x_prompt.shape
    bs, ls, _ = x_sample.shape
    xp = x_prompt.reshape(bp * lp, d)
    xs = x_sample.reshape(bs * ls, d)

    c_rows = jnp.zeros((8, d), F32).at[:bs].set(c).at[bs].set(c_ctx)
    mod = _mod_call(c_rows, w_mod, b_mod)
    rope = _rope_tables(ls)
    final_g = final_norm_g.reshape(1, d)

    w_big, w_g, b_g, alog, lgr, sd = _prep_params(w_in, m_igate_b, m_fgate_b, s_dt_bias, s_a_log, r_decay, s_d)
    pw = dict(w_big=w_big, w_g=w_g, b_g=b_g, alog=alog, lgr=lgr, sd=sd,
              norm_mix_g=norm_mix_g.reshape(DEPTH, 1, d), norm_mlp_g=norm_mlp_g.reshape(DEPTH, 1, d),
              m_norm_g=m_norm_g.reshape(DEPTH, 1, -1), r_norm_g=r_norm_g.reshape(DEPTH, 1, -1),
              s_norm_g=s_norm_g.reshape(DEPTH, 1, -1),
              s_conv_w=s_conv_w, s_conv_b=s_conv_b.reshape(DEPTH, 1, -1),
              w_br_m=w_br_m.astype(BF16), w_br_r=w_br_r.astype(BF16), w_br_s=w_br_s.astype(BF16),
              w_out=w_out.astype(BF16), w_ff1=w_ff1, w_ff2=w_ff2.astype(BF16))
    cache_c = state_mlstm_C
    cache_r = state_ret
    cache_s = state_ssd.reshape(bs, DEPTH, 2, S_GROUPS, S_GW, S_N)

    big = (None, None, None)
    st_n, st_m = [], []
    for l in range(DEPTH):
        parts = mod[l].reshape(8, 6, 1, d)
        mods_ctx = tuple(parts[bs:bs + 1, i] for i in range(6))
        mods_lat = tuple(parts[:bs, i] for i in range(6))
        fg = final_g if l == DEPTH - 1 else None

        xp, st = _layer(xp, l, mods_ctx, bp * lp, bp, lp, pw, None, None, big, fg)
        (cf, nf, mf), (rf,), (hf,) = st
        big = (cf, rf, hf)
        st_n.append(jnp.transpose(nf, (0, 2, 1, 3)))
        st_m.append(jnp.transpose(mf[..., 0], (0, 2, 1)))

        cache = (
            (l, cache_c,
             jnp.transpose(state_mlstm_n[:, l], (0, 2, 1, 3)),
             jnp.broadcast_to(jnp.transpose(state_mlstm_m[:, l], (0, 2, 1))[..., None], (bs, M_HEADS, 2, LANES))),
            (l, cache_r),
            (l, cache_s),
        )
        xs, _ = _layer(xs, l, mods_lat, ls, bs, ls, pw, rope, cache, None, fg)

    return (xp.reshape(bp, lp, d), xs.reshape(bs, ls, d),
            big[0], jnp.stack(st_n, 1), jnp.stack(st_m, 1), big[1],
            big[2].reshape(bp, DEPTH, 2, S_HEADS, S_P, S_N))
```

```python
import functools

import numpy as np
import jax
import jax.numpy as jnp
from jax import lax
from jax.experimental import pallas as pl
from jax.experimental.pallas import tpu as pltpu

F32 = jnp.float32
BF16 = jnp.bfloat16

D_MODEL = 2048
DEPTH = 2
CHUNK = 128
EPS = 1e-6
M_HEADS, M_DK = 4, 256
R_HEADS, R_DK = 8, 128
S_HEADS, S_P, S_GROUPS, S_N = 16, 64, 2, 128
S_HPG = S_HEADS // S_GROUPS
S_GW = S_HPG * S_P
GRID_W = 64
ROPE_BASE = 10000.0
ROPE_FREQS = R_DK // 4
D_FF = 4 * D_MODEL
LANES = 128
VMEM_LIMIT = 56 * 1024 * 1024
SCAN_UNROLL = 2
SEQ_ROWS_PER_STEP = 512
TM_LARGE, TM_SMALL = 1024, 512
TN_IN_PROJ, TN_MLP_UP, TN_MLP_DOWN, TN_MERGE = 1536, 2048, 512, 512

OFF_MQ, OFF_MK, OFF_MV, OFF_MO = 0, 1024, 2048, 3072
OFF_RQ, OFF_RK, OFF_RV, OFF_RG = 4096, 5120, 6144, 7168
OFF_SZ, OFF_SX, OFF_SB, OFF_SC = 8192, 9216, 10240, 10496
N_MIX = 10752
N_GATES = (M_HEADS + S_GROUPS) * LANES
GATE_LANE_OFFSETS = tuple(4 * h for h in range(M_HEADS)) + tuple(4 * M_HEADS + 2 * S_HPG * g for g in range(S_GROUPS))


def _cparams(n_axes):
    return pltpu.CompilerParams(dimension_semantics=("arbitrary",) * n_axes,
                                vmem_limit_bytes=VMEM_LIMIT)


def _dot(a, b):
    return jnp.dot(a, b, preferred_element_type=F32)


def _dot_nt(a, b):
    return lax.dot_general(a, b, (((1,), (1,)), ((), ())), preferred_element_type=F32)


def _split2(x):
    hi = x.astype(BF16)
    lo = (x - hi.astype(F32)).astype(BF16)
    return hi, lo


def _cumsum_rows(x, tri):
    hi, lo = _split2(x)
    return _dot(tri, hi) + _dot(tri, lo)


def _expand(x, e, exact=True):
    if not exact:
        return _dot(x.astype(BF16), e)
    hi, lo = _split2(x)
    return _dot(hi, e) + _dot(lo, e)


def _softplus(x):
    return jnp.maximum(x, 0.0) + jnp.log1p(jnp.exp(-jnp.abs(x)))


def _sigmoid(x):
    return 1.0 / (1.0 + jnp.exp(-x))


def _silu(x):
    return x * _sigmoid(x)


def _mod_kernel(c_ref, w_ref, b_ref, o_ref):
    a = _silu(c_ref[...]).astype(BF16)
    o_ref[...] = _dot(a, w_ref[...].astype(BF16)) + b_ref[...]


def _mod_call(c_rows, w_mod, b_mod):
    tn = 1024
    n = w_mod.shape[-1]
    return pl.pallas_call(
        _mod_kernel,
        grid=(DEPTH, n // tn),
        in_specs=[pl.BlockSpec((8, D_MODEL), lambda l, j: (0, 0)),
                  pl.BlockSpec((None, D_MODEL, tn), lambda l, j: (l, 0, j)),
                  pl.BlockSpec((None, 1, tn), lambda l, j: (l, 0, j))],
        out_specs=pl.BlockSpec((None, 8, tn), lambda l, j: (l, 0, j)),
        out_shape=jax.ShapeDtypeStruct((DEPTH, 8, n), F32),
        compiler_params=_cparams(2),
        name="mod_proj",
    )(c_rows, w_mod, b_mod.reshape(DEPTH, 1, n))


def _norm_mm_kernel(*refs, with_gates, relu2):
    if with_gates:
        x_ref, g_ref, sc_ref, sh_ref, w_ref, wg_ref, bg_ref, o_ref, og_ref, h_scr = refs
    else:
        x_ref, g_ref, sc_ref, sh_ref, w_ref, o_ref, h_scr = refs

    @pl.when(pl.program_id(1) == 0)
    def _():
        x = x_ref[...]
        ms = jnp.mean(x * x, axis=-1, keepdims=True)
        y = x * lax.rsqrt(ms + EPS) * g_ref[...]
        hb = (y * (1.0 + sc_ref[...]) + sh_ref[...]).astype(BF16)
        h_scr[...] = hb
        if with_gates:
            pre = _dot(hb, wg_ref[...]) + bg_ref[...]
            lane = lax.broadcasted_iota(jnp.int32, pre.shape, 1)
            act = jnp.where(lane < 4 * M_HEADS,
                            jnp.where(lane % 4 < 2, pre, -_softplus(-pre)),
                            _softplus(pre))
            for blk, off in enumerate(GATE_LANE_OFFSETS):
                own = 4 if blk < M_HEADS else 2 * S_HPG
                v = act if off == 0 else pltpu.roll(act, LANES - off, 1)
                og_ref[:, blk * LANES:(blk + 1) * LANES] = jnp.where(lane < own, v, 0.0)

    acc = _dot(h_scr[...], w_ref[...])
    if relu2:
        r = jnp.maximum(acc, 0.0)
        acc = r * r
    o_ref[...] = acc.astype(o_ref.dtype)


def _norm_mm_call(x, g, sc, sh, rows_per_mod, w, layer, tm, tn, out_dtype, relu2, gates=None, name="norm_mm"):
    t, d = x.shape
    n = w.shape[-1]
    assert t % tm == 0 and n % tn == 0 and rows_per_mod % tm == 0
    mod_idx = lambda i, j: ((i * tm) // rows_per_mod, 0, 0)
    in_specs = [pl.BlockSpec((tm, d), lambda i, j: (i, 0)),
                pl.BlockSpec((None, 1, d), lambda i, j: (layer, 0, 0)),
                pl.BlockSpec((None, 1, d), mod_idx),
                pl.BlockSpec((None, 1, d), mod_idx),
                pl.BlockSpec((None, d, tn), lambda i, j: (layer, 0, j))]
    args = [x, g, sc, sh, w]
    out_specs = [pl.BlockSpec((tm, tn), lambda i, j: (i, j))]
    out_shape = [jax.ShapeDtypeStruct((t, n), out_dtype)]
    if gates is not None:
        wg, bg = gates
        in_specs += [pl.BlockSpec((None, d, LANES), lambda i, j: (layer, 0, 0)),
                     pl.BlockSpec((None, 1, LANES), lambda i, j: (layer, 0, 0))]
        args += [wg, bg]
        out_specs.append(pl.BlockSpec((tm, N_GATES), lambda i, j: (i, 0)))
        out_shape.append(jax.ShapeDtypeStruct((t, N_GATES), F32))
    return pl.pallas_call(
        functools.partial(_norm_mm_kernel, with_gates=gates is not None, relu2=relu2),
        grid=(t // tm, n // tn),
        in_specs=in_specs, out_specs=out_specs, out_shape=out_shape,
        scratch_shapes=[pltpu.VMEM((tm, d), BF16)],
        compiler_params=_cparams(2),
        name=name,
    )(*args)


def _mm_resid_kernel(*refs, nj, tn, final_norm):
    if final_norm:
        a_ref, w_ref, r_ref, gate_ref, fg_ref, o_ref = refs
    else:
        a_ref, w_ref, r_ref, gate_ref, o_ref = refs
    j = pl.program_id(1)
    cols = pl.ds(pl.multiple_of(j * tn, tn), tn)
    o_ref[:, cols] = r_ref[...] + gate_ref[...] * _dot(a_ref[...], w_ref[...])
    if final_norm:
        @pl.when(j == nj - 1)
        def _():
            xn = o_ref[...]
            ms = jnp.mean(xn * xn, axis=-1, keepdims=True)
            o_ref[...] = xn * lax.rsqrt(ms + EPS) * fg_ref[...]


def _mm_resid_call(a, w, layer, resid, gate, rows_per_mod, tm, tn, final_g=None, name="mm_resid"):
    t, kdim = a.shape
    n = w.shape[-1]
    assert t % tm == 0 and n % tn == 0 and rows_per_mod % tm == 0
    nj = n // tn
    mod_idx = lambda i, j: ((i * tm) // rows_per_mod, 0, j)
    in_specs = [pl.BlockSpec((tm, kdim), lambda i, j: (i, 0)),
                pl.BlockSpec((None, kdim, tn), lambda i, j: (layer, 0, j)),
                pl.BlockSpec((tm, tn), lambda i, j: (i, j)),
                pl.BlockSpec((None, 1, tn), mod_idx)]
    args = [a, w, resid, gate]
    if final_g is not None:
        in_specs.append(pl.BlockSpec((1, n), lambda i, j: (0, 0)))
        args.append(final_g)
    return pl.pallas_call(
        functools.partial(_mm_resid_kernel, nj=nj, tn=tn, final_norm=final_g is not None),
        grid=(t // tm, nj),
        in_specs=in_specs,
        out_specs=pl.BlockSpec((tm, n), lambda i, j: (i, 0)),
        out_shape=jax.ShapeDtypeStruct((t, n), F32),
        compiler_params=_cparams(2),
        name=name,
    )(*args)


def _merge_kernel(ym_ref, yr_ref, ts_ref, sg_ref, wm_ref, wr_ref, ws_ref, g0_ref, g1_ref, g2_ref,
                  o_ref, ys_scr):
    @pl.when(pl.program_id(1) == 0)
    def _():
        t = ts_ref[...]
        ms = jnp.mean(t * t, axis=-1, keepdims=True)
        ys_scr[...] = (t * lax.rsqrt(ms + EPS) * sg_ref[...]).astype(BF16)

    acc = _sigmoid(g0_ref[...].astype(F32)) * _dot(ym_ref[...], wm_ref[...])
    acc += _sigmoid(g1_ref[...].astype(F32)) * _dot(yr_ref[...], wr_ref[...])
    acc += _sigmoid(g2_ref[...].astype(F32)) * _dot(ys_scr[...], ws_ref[...])
    o_ref[...] = acc.astype(o_ref.dtype)


def _merge_call(ym, yr, ts, sg, wm, wr, ws, layer, proj, tm, tn):
    t, w = ym.shape
    d = wm.shape[-1]
    assert t % tm == 0 and d % tn == 0 and N_MIX % tn == 0
    gl_blk = N_MIX // tn
    per_br = d // tn
    yspec = pl.BlockSpec((tm, w), lambda i, j: (i, 0))
    wspec = pl.BlockSpec((None, w, tn), lambda i, j: (layer, 0, j))
    gspec = lambda br: pl.BlockSpec((tm, tn), lambda i, j: (i, gl_blk + br * per_br + j))
    return pl.pallas_call(
        _merge_kernel,
        grid=(t // tm, d // tn),
        in_specs=[yspec, yspec, yspec, pl.BlockSpec((1, w), lambda i, j: (0, 0)),
                  wspec, wspec, wspec, gspec(0), gspec(1), gspec(2)],
        out_specs=pl.BlockSpec((tm, tn), lambda i, j: (i, j)),
        out_shape=jax.ShapeDtypeStruct((t, d), BF16),
        scratch_shapes=[pltpu.VMEM((tm, w), BF16)],
        compiler_params=_cparams(2),
        name="merge",
    )(ym, yr, ts, sg, wm, wr, ws, proj, proj, proj)


def _chunk_masks():
    row = lax.broadcasted_iota(jnp.int32, (CHUNK, CHUNK), 0)
    col = lax.broadcasted_iota(jnp.int32, (CHUNK, CHUNK), 1)
    lower = col <= row
    upper = col >= row
    tri = jnp.where(lower, 1.0, 0.0).astype(BF16)
    return row, col, lower, upper, tri


def _mlstm_kernel(*refs, nc, ns, has_state, emit_state, alias_in):
    it = iter(refs)
    q_ref, k_ref, v_ref, o_ref, g_ref, ng_ref = (next(it) for _ in range(6))
    if has_state:
        c0_ref, n0_ref, m0_ref = (next(it) for _ in range(3))
    if alias_in:
        next(it)
    y_ref = next(it)
    if emit_state:
        cf_ref, nf_ref, mf_ref = (next(it) for _ in range(3))
    hacc = (next(it), next(it))
    c_scr = (next(it), next(it))
    n_scr = (next(it), next(it))
    m_scr = (next(it), next(it))
    p_scr = (next(it), next(it))
    ml_scr = (next(it), next(it))
    bc_scr = (next(it), next(it))
    wl_scr = (next(it), next(it))
    sc_scr = (next(it), next(it))

    _, _, lower, upper, tri = _chunk_masks()
    ones_bf = jnp.ones((CHUNK, LANES), BF16)
    dirs = (0, 1)
    masks = (lower, upper)
    sub8 = lax.broadcasted_iota(jnp.int32, (8, LANES), 0)

    def gates(i, carry, base):
        cidx = (2 * i, 2 * i + 1)
        two = (0, 1)
        rows = [pl.ds(pl.multiple_of(c * CHUNK, CHUNK), CHUNK) for c in cidx]
        grow = [pl.ds(pl.multiple_of(base + c * CHUNK, CHUNK), CHUNK) for c in cidx]
        g = [g_ref[r, :] for r in grow]
        cs = [_cumsum_rows(x, tri) for x in g]
        tot = [x[CHUNK - 1:CHUNK, :] for x in cs]
        g_t = [x.T for x in g]
        cs_t = [x.T for x in cs]
        tot_t = [x[:, CHUNK - 1:CHUNK] for x in cs_t]
        bsum = [[cs[j], tot[j] - cs[j] + g[j]] for j in two]
        bsum_t = [[cs_t[j], tot_t[j] - cs_t[j] + g_t[j]] for j in two]
        items = [(j, d) for j in two for d in dirs]
        b_col = {(j, d): bsum[j][d][:, 2 + d:3 + d] for (j, d) in items}
        i_col = {(j, d): g[j][:, d:d + 1] for (j, d) in items}
        t_row = {(j, d): g_t[j][d:d + 1, :] - bsum_t[j][d][2 + d:3 + d, :] for (j, d) in items}
        b_last = {(j, d): tot[j][:, 2 + d:3 + d] for (j, d) in items}
        logw = {(j, d): jnp.where(masks[d], b_col[j, d] + t_row[j, d], -jnp.inf) for (j, d) in items}
        m_loc = {it_: jnp.max(logw[it_], axis=1, keepdims=True) for it_ in items}
        m_chunk = {it_: jnp.max(b_last[it_] + t_row[it_], axis=1, keepdims=True) for it_ in items}
        for (j, d) in items:
            p_scr[d][rows[j], :] = jnp.exp(logw[j, d] - m_loc[j, d])
            ml_scr[d][rows[j], :] = jnp.broadcast_to(m_loc[j, d], (CHUNK, LANES))
            bc_scr[d][rows[j], :] = jnp.broadcast_to(b_col[j, d], (CHUNK, LANES))
            wl_scr[d][rows[j], :] = jnp.broadcast_to(b_last[j, d] - b_col[j, d] + i_col[j, d], (CHUNK, LANES))
            sc_scr[d][pl.ds(pl.multiple_of(cidx[j] * 8, 8), 8), :] = jnp.where(
                sub8 == 0, jnp.broadcast_to(m_chunk[j, d], (8, LANES)), jnp.broadcast_to(b_last[j, d], (8, LANES)))
        return carry

    def rep2(x):
        return jnp.concatenate([x, x], axis=1)

    def scan(i, carry, base):
        cidx = (i, nc - 1 - i)
        rows = [pl.ds(pl.multiple_of(c * CHUNK, CHUNK), CHUNK) for c in cidx]
        grow = [pl.ds(pl.multiple_of(base + c * CHUNK, CHUNK), CHUNK) for c in cidx]
        sc = [sc_scr[d][pl.ds(pl.multiple_of(cidx[d] * 8, 8), 8), :] for d in dirs]
        m_chunk = [sc[d][0:1, 0:1] for d in dirs]
        b_last = [sc[d][1:2, 0:1] for d in dirs]
        qb = [q_ref[r, :].astype(BF16) for r in grow]
        k = [k_ref[r, :].astype(F32) * (M_DK ** -0.5) for r in grow]
        kb = [x.astype(BF16) for x in k]
        vb = [v_ref[r, :].astype(BF16) for r in grow]
        qk = [_dot_nt(qb[d], kb[d]) for d in dirs]
        sb = [(qk[d] * p_scr[d][rows[d], :]).astype(BF16) for d in dirs]
        a_num = [_dot(sb[d], vb[d]) for d in dirs]
        a_den = [_dot(sb[d], ones_bf) for d in dirs]

        m_prev = [m_scr[d][0:1, 0:1] for d in dirs]
        c_prev = [c_scr[d][...] for d in dirs]
        n_prev = [n_scr[d][0:1, :] for d in dirs]
        m_new = [jnp.maximum(b_last[d] + m_prev[d], m_chunk[d]) for d in dirs]
        kw = [k[d] * rep2(jnp.exp(wl_scr[d][rows[d], :] - m_new[d])) for d in dirs]
        kw_t = [x.T.astype(BF16) for x in kw]
        q_c = [_dot(qb[d], c_prev[d].astype(BF16)) for d in dirs]
        u_c = [_dot(kw_t[d], vb[d]) for d in dirs]
        m_loc = [ml_scr[d][rows[d], :] for d in dirs]
        gg = [bc_scr[d][rows[d], :] + m_prev[d] for d in dirs]
        m_tot = [jnp.maximum(m_loc[d], gg[d]) for d in dirs]
        e_intra = [jnp.exp(m_loc[d] - m_tot[d]) for d in dirs]
        e_inter = [jnp.exp(gg[d] - m_tot[d]) for d in dirs]
        q_n = [_dot_nt(qb[d], jnp.broadcast_to(n_prev[d].astype(BF16), (LANES, M_DK))) for d in dirs]
        den = [e_intra[d] * a_den[d] + e_inter[d] * q_n[d] for d in dirs]
        inv = [1.0 / jnp.maximum(jnp.abs(den[d]), jnp.exp(-m_tot[d])) for d in dirs]
        for d in dirs:
            hacc[d][rows[d], :] = rep2(e_intra[d] * inv[d]) * a_num[d] + rep2(e_inter[d] * inv[d]) * q_c[d]

        u_n = [jnp.sum(x, axis=0, keepdims=True) for x in kw]
        a = [jnp.exp(b_last[d] + m_prev[d] - m_new[d]) for d in dirs]
        for d in dirs:
            c_scr[d][...] = a[d] * c_prev[d] + u_c[d]
            n_scr[d][0:1, :] = a[d] * n_prev[d] + u_n[d]
            m_scr[d][0:1, :] = jnp.broadcast_to(m_new[d], (1, LANES))
        return carry

    def finish(c, carry, base):
        rows = pl.ds(pl.multiple_of(c * CHUNK, CHUNK), CHUNK)
        grow = pl.ds(pl.multiple_of(base + c * CHUNK, CHUNK), CHUNK)
        ht = hacc[0][rows, :] + hacc[1][rows, :]
        ms = jnp.mean(ht * ht, axis=1, keepdims=True)
        y = ht * lax.rsqrt(ms + EPS) * ng_ref[...] * _sigmoid(o_ref[grow, :].astype(F32))
        y_ref[grow, :] = y.astype(y_ref.dtype)
        return carry

    for sq in range(ns):
        base = sq * nc * CHUNK
        for d in range(2):
            if has_state:
                c_scr[d][...] = c0_ref[sq, d]
                n_scr[d][0:1, :] = n0_ref[sq, d:d + 1, :]
                m_scr[d][0:1, :] = m0_ref[sq, d:d + 1, :]
            else:
                c_scr[d][...] = jnp.zeros_like(c_scr[d])
                n_scr[d][...] = jnp.zeros_like(n_scr[d])
                m_scr[d][...] = jnp.zeros_like(m_scr[d])
        lax.fori_loop(0, nc // 2, functools.partial(gates, base=base), 0)
        lax.fori_loop(0, nc, functools.partial(scan, base=base), 0, unroll=SCAN_UNROLL)
        lax.fori_loop(0, nc, functools.partial(finish, base=base), 0, unroll=SCAN_UNROLL)
        if emit_state:
            for d in range(2):
                if alias_in:
                    cf_ref[sq, d] = c_scr[d][...]
                else:
                    for slot in range(DEPTH):
                        cf_ref[sq, slot, d] = c_scr[d][...]
                nf_ref[sq, d:d + 1, :] = n_scr[d][0:1, :]
                mf_ref[sq, d:d + 1, :] = m_scr[d][0:1, :]


def _mlstm_call(proj, gates, ng, nb, seq, ns, state=None, emit=None):
    emit_state = emit is not None
    assert nb % ns == 0
    t = proj.shape[0]
    nc = seq // CHUNK
    dk = M_DK
    cspec = lambda base: pl.BlockSpec((ns * seq, dk), lambda b, h: (b, base // dk + h))
    in_specs = [cspec(OFF_MQ), cspec(OFF_MK), cspec(OFF_MV), cspec(OFF_MO),
                pl.BlockSpec((ns * seq, LANES), lambda b, h: (b, h)),
                pl.BlockSpec((1, dk), lambda b, h: (0, h))]
    args = [proj, proj, proj, proj, gates, ng]
    if state is not None:
        sl = state[0]
        in_specs += [pl.BlockSpec((ns, None, 2, None, dk, dk), lambda b, h: (b, sl, 0, h, 0, 0)),
                     pl.BlockSpec((ns, None, 2, dk), lambda b, h: (b, h, 0, 0)),
                     pl.BlockSpec((ns, None, 2, LANES), lambda b, h: (b, h, 0, 0))]
        args += list(state[1:])
    out_specs = [pl.BlockSpec((ns * seq, dk), lambda b, h: (b, h))]
    out_shape = [jax.ShapeDtypeStruct((t, M_HEADS * dk), BF16)]
    aliases = {}
    if emit_state:
        layer, prev = emit
        slot_blk, slot_idx = (None, layer) if prev is not None else (DEPTH, 0)
        out_specs += [pl.BlockSpec((ns, slot_blk, 2, None, dk, dk), lambda b, h: (b, slot_idx, 0, h, 0, 0)),
                      pl.BlockSpec((ns, None, 2, dk), lambda b, h: (b, h, 0, 0)),
                      pl.BlockSpec((ns, None, 2, LANES), lambda b, h: (b, h, 0, 0))]
        out_shape += [jax.ShapeDtypeStruct((nb, DEPTH, 2, M_HEADS, dk, dk), F32),
                      jax.ShapeDtypeStruct((nb, M_HEADS, 2, dk), F32),
                      jax.ShapeDtypeStruct((nb, M_HEADS, 2, LANES), F32)]
        if prev is not None:
            in_specs.append(pl.BlockSpec(memory_space=pl.ANY))
            args.append(prev)
            aliases = {len(args) - 1: 1}
    return pl.pallas_call(
        functools.partial(_mlstm_kernel, nc=nc, ns=ns, has_state=state is not None, emit_state=emit_state,
                          alias_in=bool(aliases)),
        grid=(nb // ns, M_HEADS),
        in_specs=in_specs, out_specs=out_specs, out_shape=out_shape, input_output_aliases=aliases,
        scratch_shapes=([pltpu.VMEM((seq, dk), F32)] * 2 + [pltpu.VMEM((dk, dk), F32)] * 2
                        + [pltpu.VMEM((8, dk), F32)] * 2 + [pltpu.VMEM((8, LANES), F32)] * 2
                        + [pltpu.VMEM((seq, LANES), F32)] * 8 + [pltpu.VMEM((nc * 8, LANES), F32)] * 2),
        compiler_params=_cparams(2),
        name="mlstm",
    )(*args)


R_HPS = 2


def _ret_kernel(*refs, nc, ns, has_state, emit_state, rope, alias_in):
    it = iter(refs)
    q_ref, k_ref, v_ref, g_ref, lg_ref, ng_ref = (next(it) for _ in range(6))
    if rope:
        cos_ref, sin_ref = next(it), next(it)
    if has_state:
        s0_ref = next(it)
    if alias_in:
        next(it)
    y_ref = next(it)
    if emit_state:
        sf_ref = next(it)
    chains = [(h, d) for h in range(R_HPS) for d in range(2)]
    oacc = {c: next(it) for c in chains}
    s_scr = {c: next(it) for c in chains}
    qs_scr, ks_scr, kt_scr = (next(it) for _ in range(3))

    row, col, lower, upper, _ = _chunk_masks()
    rel = (row - col).astype(F32)
    pos = row.astype(F32)
    decay, w_q, w_k, chunk_decay = {}, {}, {}, {}
    for (h, d) in chains:
        lgd = -jnp.exp(lg_ref[h, d:d + 1, :])
        if d == 0:
            decay[h, d] = jnp.where(lower, jnp.exp(rel * lgd), 0.0)
            w_q[h, d] = jnp.exp((pos + 1.0) * lgd)
            w_k[h, d] = jnp.exp((CHUNK - 1.0 - pos) * lgd)
        else:
            decay[h, d] = jnp.where(upper, jnp.exp(-rel * lgd), 0.0)
            w_q[h, d] = jnp.exp((CHUNK - pos) * lgd)
            w_k[h, d] = jnp.exp(pos * lgd)
        chunk_decay[h, d] = jnp.exp(CHUNK * lgd)

    def hcols(h):
        return slice(h * R_DK, (h + 1) * R_DK)

    def prep(c, carry, base):
        rows = pl.ds(pl.multiple_of(c * CHUNK, CHUNK), CHUNK)
        grow = pl.ds(pl.multiple_of(base + c * CHUNK, CHUNK), CHUNK)
        for h in range(R_HPS):
            q = q_ref[grow, hcols(h)].astype(F32)
            k = k_ref[grow, hcols(h)].astype(F32) * (R_DK ** -0.5)
            if rope:
                cs, sn = cos_ref[grow, :], sin_ref[grow, :]
                q = q * cs + pltpu.roll(q, R_DK // 2, 1) * sn
                k = k * cs + pltpu.roll(k, R_DK // 2, 1) * sn
            qs_scr[rows, hcols(h)] = q.astype(BF16)
            ks_scr[rows, hcols(h)] = k.astype(BF16)
            kt_scr[hcols(h), rows] = k.T.astype(BF16)
        return carry

    def scan(i, carry, base):
        rows = {0: pl.ds(pl.multiple_of(i * CHUNK, CHUNK), CHUNK),
                1: pl.ds(pl.multiple_of((nc - 1 - i) * CHUNK, CHUNK), CHUNK)}
        grow = {0: pl.ds(pl.multiple_of(base + i * CHUNK, CHUNK), CHUNK),
                1: pl.ds(pl.multiple_of(base + (nc - 1 - i) * CHUNK, CHUNK), CHUNK)}
        qb = {(h, d): qs_scr[rows[d], hcols(h)] for (h, d) in chains}
        kb = {(h, d): ks_scr[rows[d], hcols(h)] for (h, d) in chains}
        kt = {(h, d): kt_scr[hcols(h), rows[d]] for (h, d) in chains}
        vb = {(h, d): v_ref[grow[d], hcols(h)].astype(BF16) for (h, d) in chains}
        vw = {c: (vb[c].astype(F32) * w_k[c]).astype(BF16) for c in chains}
        s_prev = {c: s_scr[c][...] for c in chains}
        sc = {c: (_dot_nt(qb[c], kb[c]) * decay[c]).astype(BF16) for c in chains}
        inter = {c: _dot(qb[c], s_prev[c].astype(BF16)) for c in chains}
        u = {c: _dot(kt[c], vw[c]) for c in chains}
        intra = {c: _dot(sc[c], vb[c]) for c in chains}
        for (h, d) in chains:
            oacc[h, d][rows[d], :] = intra[h, d] + w_q[h, d] * inter[h, d]
            s_scr[h, d][...] = chunk_decay[h, d] * s_prev[h, d] + u[h, d]
        return carry

    def finish(c, carry, base):
        rows = pl.ds(pl.multiple_of(c * CHUNK, CHUNK), CHUNK)
        grow = pl.ds(pl.multiple_of(base + c * CHUNK, CHUNK), CHUNK)
        for h in range(R_HPS):
            ot = oacc[h, 0][rows, :] + oacc[h, 1][rows, :]
            ms = jnp.mean(ot * ot, axis=1, keepdims=True)
            y = ot * lax.rsqrt(ms + EPS) * ng_ref[:, hcols(h)] * _silu(g_ref[grow, hcols(h)].astype(F32))
            y_ref[grow, hcols(h)] = y.astype(y_ref.dtype)
        return carry

    for sq in range(ns):
        base = sq * nc * CHUNK
        for (h, d) in chains:
            if has_state:
                s_scr[h, d][...] = s0_ref[sq, d, h]
            else:
                s_scr[h, d][...] = jnp.zeros_like(s_scr[h, d])
        lax.fori_loop(0, nc, functools.partial(prep, base=base), 0, unroll=SCAN_UNROLL)
        lax.fori_loop(0, nc, functools.partial(scan, base=base), 0, unroll=SCAN_UNROLL)
        lax.fori_loop(0, nc, functools.partial(finish, base=base), 0, unroll=SCAN_UNROLL)
        if emit_state:
            for (h, d) in chains:
                if alias_in:
                    sf_ref[sq, d, h] = s_scr[h, d][...]
                else:
                    for slot in range(DEPTH):
                        sf_ref[sq, slot, d, h] = s_scr[h, d][...]


def _ret_call(proj, lgr, ng, nb, seq, ns, rope=None, state=None, emit=None):
    emit_state = emit is not None
    assert nb % ns == 0 and (rope is None or ns == 1)
    t = proj.shape[0]
    nc = seq // CHUNK
    dk = R_DK
    bw = R_HPS * dk
    cspec = lambda base: pl.BlockSpec((ns * seq, bw), lambda b, h: (b, base // bw + h))
    in_specs = [cspec(OFF_RQ), cspec(OFF_RK), cspec(OFF_RV), cspec(OFF_RG),
                pl.BlockSpec((R_HPS, 2, LANES), lambda b, h: (h, 0, 0)),
                pl.BlockSpec((1, bw), lambda b, h: (0, h))]
    args = [proj, proj, proj, proj, lgr, ng]
    if rope is not None:
        in_specs += [pl.BlockSpec((seq, dk), lambda b, h: (0, 0))] * 2
        args += list(rope)
    if state is not None:
        sl = state[0]
        in_specs.append(pl.BlockSpec((ns, None, 2, R_HPS, dk, dk), lambda b, h: (b, sl, 0, h, 0, 0)))
        args.append(state[1])
    out_specs = [pl.BlockSpec((ns * seq, bw), lambda b, h: (b, h))]
    out_shape = [jax.ShapeDtypeStruct((t, R_HEADS * dk), BF16)]
    aliases = {}
    if emit_state:
        layer, prev = emit
        slot_blk, slot_idx = (None, layer) if prev is not None else (DEPTH, 0)
        out_specs.append(pl.BlockSpec((ns, slot_blk, 2, R_HPS, dk, dk), lambda b, h: (b, slot_idx, 0, h, 0, 0)))
        out_shape.append(jax.ShapeDtypeStruct((nb, DEPTH, 2, R_HEADS, dk, dk), F32))
        if prev is not None:
            in_specs.append(pl.BlockSpec(memory_space=pl.ANY))
            args.append(prev)
            aliases = {len(args) - 1: 1}
    n_chain = 2 * R_HPS
    return pl.pallas_call(
        functools.partial(_ret_kernel, nc=nc, ns=ns, has_state=state is not None, emit_state=emit_state,
                          rope=rope is not None, alias_in=bool(aliases)),
        grid=(nb // ns, R_HEADS // R_HPS),
        in_specs=in_specs, out_specs=out_specs, out_shape=out_shape, input_output_aliases=aliases,
        scratch_shapes=([pltpu.VMEM((seq, dk), F32)] * n_chain + [pltpu.VMEM((dk, dk), F32)] * n_chain
                        + [pltpu.VMEM((seq, bw), BF16)] * 2 + [pltpu.VMEM((bw, seq), BF16)]),
        compiler_params=_cparams(2),
        name="retention",
    )(*args)


def _ssd_kernel(*refs, nc, ns, has_state, emit_state, alias_in):
    it = iter(refs)
    (x_ref, b_ref, c_ref, z_ref, g_ref, wx_ref, wb_ref, wc_ref, bx_ref, bb_ref, bc_ref,
     al_ref, sd_ref) = (next(it) for _ in range(13))
    if has_state:
        h0_ref = next(it)
    if alias_in:
        next(it)
    y_ref = next(it)
    if emit_state:
        hf_ref = next(it)
    xs_scr, bt_scr, bb_scr, cc_scr, ht_scr, yb_scr = (next(it) for _ in range(6))

    _, _, lower, upper, tri = _chunk_masks()
    er = lax.broadcasted_iota(jnp.int32, (LANES, S_GW), 0)
    ec = lax.broadcasted_iota(jnp.int32, (LANES, S_GW), 1) // S_P
    expand = [jnp.where(er == ec + d * S_HPG, 1.0, 0.0).astype(BF16) for d in range(2)]
    a_row = -jnp.exp(al_ref[...])

    wr = lax.broadcasted_iota(jnp.int32, (CHUNK, CHUNK + 32), 0)
    wc = lax.broadcasted_iota(jnp.int32, (CHUNK, CHUNK + 32), 1)
    shifts = [jnp.where(wc == wr + 16 + sft, 1.0, 0.0).astype(BF16) for sft in (-1, 1, 2)]

    def conv_silu(ref, w_ref, bias_ref, c, base):
        off = base + c * CHUNK
        cur = ref[pl.ds(pl.multiple_of(off, CHUNK), CHUNK), :]
        prev_w = ref[pl.ds(pl.multiple_of(jnp.maximum(off - 16, 0), 16), 16), :]
        next_w = ref[pl.ds(pl.multiple_of(jnp.minimum(off + CHUNK, ns * nc * CHUNK - 16), 16), 16), :]
        prev_w = jnp.where(c > 0, prev_w, jnp.zeros_like(prev_w))
        next_w = jnp.where(c < nc - 1, next_w, jnp.zeros_like(next_w))
        win = jnp.concatenate([prev_w, cur, next_w], axis=0)
        xm1, xp1, xp2 = (_dot(sm, win) for sm in shifts)
        w = w_ref[...]
        y = w[0:1, :] * xm1 + w[1:2, :] * cur.astype(F32) + w[2:3, :] * xp1 + w[3:4, :] * xp2 + bias_ref[...]
        return _silu(y)

    def prep(c, carry, base):
        off = pl.multiple_of(c * CHUNK, CHUNK)
        rows = pl.ds(off, CHUNK)
        xs_scr[rows, :] = conv_silu(x_ref, wx_ref, bx_ref, c, base)
        bm = conv_silu(b_ref, wb_ref, bb_ref, c, base)
        bb_scr[rows, :] = bm.astype(BF16)
        bt_scr[:, rows] = bm.T.astype(BF16)
        cc_scr[rows, :] = conv_silu(c_ref, wc_ref, bc_ref, c, base).astype(BF16)
        return carry

    dirs = (0, 1)
    masks = (lower, upper)
    first_half = lax.broadcasted_iota(jnp.int32, (CHUNK, LANES), 1) < S_P

    def scan(i, carry, base):
        rows = [pl.ds(pl.multiple_of(c * CHUNK, CHUNK), CHUNK) for c in (i, nc - 1 - i)]
        grow = [pl.ds(pl.multiple_of(base + c * CHUNK, CHUNK), CHUNK) for c in (i, nc - 1 - i)]
        dt = [g_ref[r, :] for r in grow]
        a = [x * a_row for x in dt]
        cs = [_cumsum_rows(x, tri) for x in a]
        tot = [x[CHUNK - 1:CHUNK, :] for x in cs]
        bsum = [cs[0], tot[1] - cs[1] + a[1]]
        bsum_t = [x.T for x in bsum]
        dt_t = [x.T for x in dt]
        xs = [xs_scr[r, :] for r in rows]
        xb = [x.astype(BF16) for x in xs]
        bmb = [bb_scr[r, :] for r in rows]
        cmb = [cc_scr[r, :] for r in rows]
        btb = [bt_scr[:, r] for r in rows]
        cb = [_dot_nt(cmb[d], bmb[d]) for d in dirs]
        ht = [ht_scr[d] for d in dirs]
        inter = [_dot(cmb[d], ht[d].astype(BF16)) for d in dirs]
        ex = [_expand(jnp.exp(bsum[d]), expand[d], exact=False) for d in dirs]
        wexp = [_expand(jnp.exp(tot[d] - bsum[d]) * dt[d], expand[d], exact=False) for d in dirs]
        cd = [_expand(jnp.broadcast_to(jnp.exp(tot[d]), (8, LANES)), expand[d])[0:1, :] for d in dirs]
        u_t = [_dot(btb[d], (xs[d] * wexp[d]).astype(BF16)) for d in dirs]
        ys = [[], []]
        for p in range(S_HPG // 2):
            for d in dirs:
                ms = []
                for k in (2 * p, 2 * p + 1):
                    ln = d * S_HPG + k
                    decay = jnp.exp(jnp.where(masks[d], bsum[d][:, ln:ln + 1] - bsum_t[d][ln:ln + 1, :], -jnp.inf))
                    ms.append((cb[d] * decay * dt_t[d][ln:ln + 1, :]).astype(BF16))
                xp = xb[d][:, p * LANES:(p + 1) * LANES]
                zero = jnp.zeros_like(xp)
                rhs = jnp.concatenate([jnp.where(first_half, xp, zero), jnp.where(first_half, zero, xp)], axis=0)
                ys[d].append(_dot(jnp.concatenate(ms, axis=1), rhs))
        y_ref[grow[0], :] = jnp.concatenate(ys[0], axis=1) + ex[0] * inter[0]
        yb_scr[rows[1], :] = jnp.concatenate(ys[1], axis=1) + ex[1] * inter[1]
        for d in dirs:
            ht_scr[d] = ht[d] * cd[d] + u_t[d]
        return carry

    def finish(c, carry, base):
        rows = pl.ds(pl.multiple_of(c * CHUNK, CHUNK), CHUNK)
        grow = pl.ds(pl.multiple_of(base + c * CHUNK, CHUNK), CHUNK)
        yt = y_ref[grow, :] + yb_scr[rows, :] + sd_ref[...] * xs_scr[rows, :]
        y_ref[grow, :] = yt * _silu(z_ref[grow, :].astype(F32))
        return carry

    for sq in range(ns):
        base = sq * nc * CHUNK
        lax.fori_loop(0, nc, functools.partial(prep, base=base), 0, unroll=SCAN_UNROLL)
        for d in range(2):
            if has_state:
                ht_scr[d] = h0_ref[sq, d].T
            else:
                ht_scr[d] = jnp.zeros((S_N, S_GW), F32)
        lax.fori_loop(0, nc, functools.partial(scan, base=base), 0, unroll=SCAN_UNROLL)
        lax.fori_loop(0, nc, functools.partial(finish, base=base), 0, unroll=SCAN_UNROLL)
        if emit_state:
            for d in range(2):
                h_fin = ht_scr[d].T
                if alias_in:
                    hf_ref[sq, d] = h_fin
                else:
                    for slot in range(DEPTH):
                        hf_ref[sq, slot, d] = h_fin


def _ssd_call(proj, gates, conv_w, conv_b, alog, sd, nb, seq, ns, state=None, emit=None):
    emit_state = emit is not None
    assert nb % ns == 0
    t = proj.shape[0]
    nc = seq // CHUNK
    gw = S_GW
    n = S_N
    xw = S_HEADS * S_P
    in_specs = [pl.BlockSpec((ns * seq, gw), lambda b, g: (b, OFF_SX // gw + g)),
                pl.BlockSpec((ns * seq, n), lambda b, g: (b, OFF_SB // n + g)),
                pl.BlockSpec((ns * seq, n), lambda b, g: (b, OFF_SC // n + g)),
                pl.BlockSpec((ns * seq, gw), lambda b, g: (b, OFF_SZ // gw + g)),
                pl.BlockSpec((ns * seq, LANES), lambda b, g: (b, M_HEADS + g)),
                pl.BlockSpec((4, gw), lambda b, g: (0, g)),
                pl.BlockSpec((4, n), lambda b, g: (0, xw // n + g)),
                pl.BlockSpec((4, n), lambda b, g: (0, xw // n + S_GROUPS + g)),
                pl.BlockSpec((1, gw), lambda b, g: (0, g)),
                pl.BlockSpec((1, n), lambda b, g: (0, xw // n + g)),
                pl.BlockSpec((1, n), lambda b, g: (0, xw // n + S_GROUPS + g)),
                pl.BlockSpec((None, 1, LANES), lambda b, g: (g, 0, 0)),
                pl.BlockSpec((1, gw), lambda b, g: (0, g))]
    args = [proj, proj, proj, proj, gates, conv_w, conv_w, conv_w, conv_b, conv_b, conv_b, alog, sd]
    if state is not None:
        sl = state[0]
        in_specs.append(pl.BlockSpec((ns, None, 2, None, gw, n), lambda b, g: (b, sl, 0, g, 0, 0)))
        args.append(state[1])
    out_specs = [pl.BlockSpec((ns * seq, gw), lambda b, g: (b, g))]
    out_shape = [jax.ShapeDtypeStruct((t, xw), F32)]
    aliases = {}
    if emit_state:
        layer, prev = emit
        slot_blk, slot_idx = (None, layer) if prev is not None else (DEPTH, 0)
        out_specs.append(pl.BlockSpec((ns, slot_blk, 2, None, gw, n), lambda b, g: (b, slot_idx, 0, g, 0, 0)))
        out_shape.append(jax.ShapeDtypeStruct((nb, DEPTH, 2, S_GROUPS, gw, n), F32))
        if prev is not None:
            in_specs.append(pl.BlockSpec(memory_space=pl.ANY))
            args.append(prev)
            aliases = {len(args) - 1: 1}
    return pl.pallas_call(
        functools.partial(_ssd_kernel, nc=nc, ns=ns, has_state=state is not None, emit_state=emit_state,
                          alias_in=bool(aliases)),
        grid=(nb // ns, S_GROUPS),
        in_specs=in_specs, out_specs=out_specs, out_shape=out_shape, input_output_aliases=aliases,
        scratch_shapes=[pltpu.VMEM((seq, gw), F32), pltpu.VMEM((n, seq), BF16),
                        pltpu.VMEM((seq, n), BF16), pltpu.VMEM((seq, n), BF16),
                        pltpu.VMEM((2, n, gw), F32), pltpu.VMEM((seq, gw), F32)],
        compiler_params=_cparams(2),
        name="ssd",
    )(*args)


_IN_SIZES = (1024, 1024, 1024, 1024, 8, 8, 1024, 1024, 1024, 1024, 1024, 1536, 32, 6144)
_IN_OFFS = np.concatenate([[0], np.cumsum(_IN_SIZES)])


def _gate_lane_index():
    idx = -np.ones((LANES,), np.int64)
    for h in range(M_HEADS):
        base = GATE_LANE_OFFSETS[h]
        idx[base + 0] = 0 * M_HEADS + h
        idx[base + 1] = 1 * M_HEADS + h
        idx[base + 2] = 2 * M_HEADS + 0 * M_HEADS + h
        idx[base + 3] = 2 * M_HEADS + 1 * M_HEADS + h
    for g in range(S_GROUPS):
        base = GATE_LANE_OFFSETS[M_HEADS + g]
        for d in range(2):
            for k in range(S_HPG):
                idx[base + d * S_HPG + k] = 4 * M_HEADS + d * S_HEADS + g * S_HPG + k
    return idx


def _place(vals, idx):
    taken = jnp.take(vals, jnp.asarray(np.maximum(idx, 0)), axis=-1)
    return jnp.where(jnp.asarray(idx >= 0), taken, 0.0)


def _prep_params(w_in, m_igate_b, m_fgate_b, s_dt_bias, s_a_log, r_decay, s_d):
    o = _IN_OFFS
    w_big = jnp.concatenate([w_in[:, :, o[0]:o[4]], w_in[:, :, o[6]:o[12]], w_in[:, :, o[13]:o[14]]],
                            axis=2).astype(BF16)
    w_small = jnp.concatenate([w_in[:, :, o[4]:o[6]], w_in[:, :, o[12]:o[13]]], axis=2)
    gidx = _gate_lane_index()
    w_g = _place(w_small, gidx).astype(BF16)
    b_small = jnp.concatenate([m_igate_b.reshape(DEPTH, -1), m_fgate_b.reshape(DEPTH, -1),
                               s_dt_bias.reshape(DEPTH, -1)], axis=1)
    b_g = _place(b_small, gidx).reshape(DEPTH, 1, LANES)
    aidx = -np.ones((S_GROUPS, LANES), np.int64)
    for g in range(S_GROUPS):
        for d in range(2):
            for k in range(S_HPG):
                aidx[g, d * S_HPG + k] = d * S_HEADS + g * S_HPG + k
    a_flat = s_a_log.reshape(DEPTH, -1)
    alog = jnp.stack([_place(a_flat, aidx[g]) for g in range(S_GROUPS)], axis=1).reshape(DEPTH, S_GROUPS, 1, LANES)
    lgr = jnp.broadcast_to(jnp.swapaxes(r_decay, 1, 2)[..., None], (DEPTH, R_HEADS, 2, LANES))
    sd = jnp.repeat(s_d, S_P, axis=1).reshape(DEPTH, 1, S_HEADS * S_P)
    return w_big, w_g, b_g, alog, lgr, sd


def _rope_tables(seq):
    n_rows = seq // GRID_W
    rows = jnp.repeat(jnp.arange(n_rows, dtype=F32), GRID_W)
    cols = jnp.tile(jnp.arange(GRID_W, dtype=F32), n_rows)
    inv = ROPE_BASE ** (-jnp.arange(ROPE_FREQS, dtype=F32) / ROPE_FREQS)
    ang = jnp.concatenate([rows[:, None] * inv, cols[:, None] * inv], -1)
    cos, sin = jnp.cos(ang), jnp.sin(ang)
    return jnp.concatenate([cos, cos], -1), jnp.concatenate([-sin, sin], -1)


def _layer(x, l, mods, rows_per_mod, nb, seq, pw, rope, state, emit, final_g):
    sh_a, sc_a, g_a, sh_f, sc_f, g_f = mods
    tm_l, tm_s = min(TM_LARGE, rows_per_mod), min(TM_SMALL, rows_per_mod)
    proj, gates = _norm_mm_call(x, pw["norm_mix_g"], sc_a, sh_a, rows_per_mod, pw["w_big"], l, tm_l, TN_IN_PROJ,
                                BF16, False, gates=(pw["w_g"], pw["b_g"]), name="in_proj")
    st_m = st_r = st_s = None
    if state is not None:
        st_m, st_r, st_s = state
    em_m = em_r = em_s = None
    if emit is not None:
        em_m, em_r, em_s = ((l, prev) for prev in emit)
    ns = max(1, SEQ_ROWS_PER_STEP // seq)
    om = _mlstm_call(proj, gates, pw["m_norm_g"][l], nb, seq, 1, state=st_m, emit=em_m)
    orr = _ret_call(proj, pw["lgr"][l], pw["r_norm_g"][l], nb, seq, ns, rope=rope, state=st_r, emit=em_r)
    os_ = _ssd_call(proj, gates, pw["s_conv_w"][l], pw["s_conv_b"][l], pw["alog"][l], pw["sd"][l], nb, seq, ns,
                    state=st_s, emit=em_s)
    merged = _merge_call(om[0], orr[0], os_[0], pw["s_norm_g"][l], pw["w_br_m"], pw["w_br_r"], pw["w_br_s"], l,
                         proj, tm_l, TN_MERGE)
    x = _mm_resid_call(merged, pw["w_out"], l, x, g_a, rows_per_mod, tm_s, D_MODEL, name="out_proj")
    (hid,) = _norm_mm_call(x, pw["norm_mlp_g"], sc_f, sh_f, rows_per_mod, pw["w_ff1"], l, tm_l, TN_MLP_UP,
                           BF16, True, name="mlp_up")
    x = _mm_resid_call(hid, pw["w_ff2"], l, x, g_f, rows_per_mod, tm_s, TN_MLP_DOWN, final_g=final_g, name="mlp_down")
    new_state = (om[1:], orr[1:], os_[1:]) if emit is not None else None
    return x, new_state


def kernel(x_prompt, x_sample, c, state_mlstm_C, state_mlstm_n, state_mlstm_m, state_ret, state_ssd, c_ctx, w_mod, b_mod, norm_mix_g, norm_mlp_g, w_in, m_igate_b, m_fgate_b, m_norm_g, r_decay, r_norm_g, s_conv_w, s_conv_b, s_dt_bias, s_a_log, s_d, s_norm_g, w_br_m, w_br_r, w_br_s, w_out, w_ff1, w_ff2, final_norm_g):
    bp, lp, d = x_prompt.shape
    bs, ls, _ = x_sample.shape
    xp = x_prompt.reshape(bp * lp, d)
    xs = x_sample.reshape(bs * ls, d)

    c_rows = jnp.zeros((8, d), F32).at[:bs].set(c).at[bs].set(c_ctx)
    mod = _mod_call(c_rows, w_mod, b_mod)
    rope = _rope_tables(ls)
    final_g = final_norm_g.reshape(1, d)

    w_big, w_g, b_g, alog, lgr, sd = _prep_params(w_in, m_igate_b, m_fgate_b, s_dt_bias, s_a_log, r_decay, s_d)
    pw = dict(w_big=w_big, w_g=w_g, b_g=b_g, alog=alog, lgr=lgr, sd=sd,
              norm_mix_g=norm_mix_g.reshape(DEPTH, 1, d), norm_mlp_g=norm_mlp_g.reshape(DEPTH, 1, d),
              m_norm_g=m_norm_g.reshape(DEPTH, 1, -1), r_norm_g=r_norm_g.reshape(DEPTH, 1, -1),
              s_norm_g=s_norm_g.reshape(DEPTH, 1, -1),
              s_conv_w=s_conv_w, s_conv_b=s_conv_b.reshape(DEPTH, 1, -1),
              w_br_m=w_br_m.astype(BF16), w_br_r=w_br_r.astype(BF16), w_br_s=w_br_s.astype(BF16),
              w_out=w_out.astype(BF16), w_ff1=w_ff1.astype(BF16), w_ff2=w_ff2.astype(BF16))
    cache_c = state_mlstm_C
    cache_r = state_ret
    cache_s = state_ssd.reshape(bs, DEPTH, 2, S_GROUPS, S_GW, S_N)

    big = (None, None, None)
    st_n, st_m = [], []
    for l in range(DEPTH):
        parts = mod[l].reshape(8, 6, 1, d)
        mods_ctx = tuple(parts[bs:bs + 1, i] for i in range(6))
        mods_lat = tuple(parts[:bs, i] for i in range(6))
        fg = final_g if l == DEPTH - 1 else None

        xp, st = _layer(xp, l, mods_ctx, bp * lp, bp, lp, pw, None, None, big, fg)
        (cf, nf, mf), (rf,), (hf,) = st
        big = (cf, rf, hf)
        st_n.append(jnp.transpose(nf, (0, 2, 1, 3)))
        st_m.append(jnp.transpose(mf[..., 0], (0, 2, 1)))

        cache = (
            (l, cache_c,
             jnp.transpose(state_mlstm_n[:, l], (0, 2, 1, 3)),
             jnp.broadcast_to(jnp.transpose(state_mlstm_m[:, l], (0, 2, 1))[..., None], (bs, M_HEADS, 2, LANES))),
            (l, cache_r),
            (l, cache_s),
        )
        xs, _ = _layer(xs, l, mods_lat, ls, bs, ls, pw, rope, cache, None, fg)

    return (xp.reshape(bp, lp, d), xs.reshape(bs, ls, d),
            big[0], jnp.stack(st_n, 1), jnp.stack(st_m, 1), big[1],
            big[2].reshape(bp, DEPTH, 2, S_HEADS, S_P, S_N))
```

```python
import functools

import numpy as np
import jax
import jax.numpy as jnp
from jax import lax
from jax.experimental import pallas as pl
from jax.experimental.pallas import tpu as pltpu

F32 = jnp.float32
BF16 = jnp.bfloat16

D_MODEL = 2048
DEPTH = 2
CHUNK = 128
EPS = 1e-6
M_HEADS, M_DK = 4, 256
R_HEADS, R_DK = 8, 128
S_HEADS, S_P, S_GROUPS, S_N = 16, 64, 2, 128
S_HPG = S_HEADS // S_GROUPS
S_GW = S_HPG * S_P
GRID_W = 64
ROPE_BASE = 10000.0
ROPE_FREQS = R_DK // 4
D_FF = 4 * D_MODEL
LANES = 128
VMEM_LIMIT = 56 * 1024 * 1024
SCAN_UNROLL = 2
SEQ_ROWS_PER_STEP = 1024
TM_LARGE, TM_SMALL = 1024, 512
TN_IN_PROJ, TN_MLP_UP, TN_MLP_DOWN, TN_MERGE = 1536, 2048, 512, 1024

N_GL = 3 * D_MODEL
OFF_MQ, OFF_MK, OFF_MV, OFF_MO = (N_GL + o for o in (0, 1024, 2048, 3072))
OFF_RQ, OFF_RK, OFF_RV, OFF_RG = (N_GL + o for o in (4096, 5120, 6144, 7168))
OFF_SZ, OFF_SX, OFF_SB, OFF_SC = (N_GL + o for o in (8192, 9216, 10240, 10496))
N_GATES = (M_HEADS + S_GROUPS) * LANES
GATE_LANE_OFFSETS = tuple(4 * h for h in range(M_HEADS)) + tuple(4 * M_HEADS + 2 * S_HPG * g for g in range(S_GROUPS))


def _cparams(n_axes):
    return pltpu.CompilerParams(dimension_semantics=("arbitrary",) * n_axes,
                                vmem_limit_bytes=VMEM_LIMIT)


def _dot(a, b):
    return jnp.dot(a, b, preferred_element_type=F32)


def _dot_nt(a, b):
    return lax.dot_general(a, b, (((1,), (1,)), ((), ())), preferred_element_type=F32)


def _split2(x):
    hi = x.astype(BF16)
    lo = (x - hi.astype(F32)).astype(BF16)
    return hi, lo


def _cumsum_rows(x, tri):
    hi, lo = _split2(x)
    return _dot(tri, hi) + _dot(tri, lo)


def _expand(x, e, exact=True):
    if not exact:
        return _dot(x.astype(BF16), e)
    hi, lo = _split2(x)
    return _dot(hi, e) + _dot(lo, e)


def _softplus(x):
    return jnp.maximum(x, 0.0) + jnp.log1p(jnp.exp(-jnp.abs(x)))


def _sigmoid(x):
    return 1.0 / (1.0 + jnp.exp(-x))


def _silu(x):
    return x * _sigmoid(x)


def _mod_kernel(c_ref, w_ref, b_ref, o_ref):
    a = _silu(c_ref[...]).astype(BF16)
    o_ref[...] = _dot(a, w_ref[...].astype(BF16)) + b_ref[...]


def _mod_call(c_rows, w_mod, b_mod):
    tn = 1024
    n = w_mod.shape[-1]
    return pl.pallas_call(
        _mod_kernel,
        grid=(DEPTH, n // tn),
        in_specs=[pl.BlockSpec((8, D_MODEL), lambda l, j: (0, 0)),
                  pl.BlockSpec((None, D_MODEL, tn), lambda l, j: (l, 0, j)),
                  pl.BlockSpec((None, 1, tn), lambda l, j: (l, 0, j))],
        out_specs=pl.BlockSpec((None, 8, tn), lambda l, j: (l, 0, j)),
        out_shape=jax.ShapeDtypeStruct((DEPTH, 8, n), F32),
        compiler_params=_cparams(2),
        name="mod_proj",
    )(c_rows, w_mod, b_mod.reshape(DEPTH, 1, n))


def _norm_mm_kernel(*refs, with_gates, relu2):
    if with_gates:
        x_ref, g_ref, sc_ref, sh_ref, w_ref, wg_ref, bg_ref, o_ref, og_ref, h_scr = refs
    else:
        x_ref, g_ref, sc_ref, sh_ref, w_ref, o_ref, h_scr = refs

    @pl.when(pl.program_id(1) == 0)
    def _():
        x = x_ref[...]
        ms = jnp.mean(x * x, axis=-1, keepdims=True)
        y = x * lax.rsqrt(ms + EPS) * g_ref[...]
        hb = (y * (1.0 + sc_ref[...]) + sh_ref[...]).astype(BF16)
        h_scr[...] = hb
        if with_gates:
            pre = _dot(hb, wg_ref[...]) + bg_ref[...]
            lane = lax.broadcasted_iota(jnp.int32, pre.shape, 1)
            act = jnp.where(lane < 4 * M_HEADS,
                            jnp.where(lane % 4 < 2, pre, -_softplus(-pre)),
                            _softplus(pre))
            for blk, off in enumerate(GATE_LANE_OFFSETS):
                own = 4 if blk < M_HEADS else 2 * S_HPG
                v = act if off == 0 else pltpu.roll(act, LANES - off, 1)
                og_ref[:, blk * LANES:(blk + 1) * LANES] = jnp.where(lane < own, v, 0.0)

    acc = _dot(h_scr[...], w_ref[...])
    if relu2:
        r = jnp.maximum(acc, 0.0)
        acc = r * r
    o_ref[...] = acc.astype(o_ref.dtype)


def _norm_mm_call(x, g, sc, sh, rows_per_mod, w, layer, tm, tn, out_dtype, relu2, gates=None, name="norm_mm"):
    t, d = x.shape
    n = w.shape[-1]
    assert t % tm == 0 and n % tn == 0 and rows_per_mod % tm == 0
    mod_idx = lambda i, j: ((i * tm) // rows_per_mod, 0, 0)
    in_specs = [pl.BlockSpec((tm, d), lambda i, j: (i, 0)),
                pl.BlockSpec((None, 1, d), lambda i, j: (layer, 0, 0)),
                pl.BlockSpec((None, 1, d), mod_idx),
                pl.BlockSpec((None, 1, d), mod_idx),
                pl.BlockSpec((None, d, tn), lambda i, j: (layer, 0, j))]
    args = [x, g, sc, sh, w]
    out_specs = [pl.BlockSpec((tm, tn), lambda i, j: (i, j))]
    out_shape = [jax.ShapeDtypeStruct((t, n), out_dtype)]
    if gates is not None:
        wg, bg = gates
        in_specs += [pl.BlockSpec((None, d, LANES), lambda i, j: (layer, 0, 0)),
                     pl.BlockSpec((None, 1, LANES), lambda i, j: (layer, 0, 0))]
        args += [wg, bg]
        out_specs.append(pl.BlockSpec((tm, N_GATES), lambda i, j: (i, 0)))
        out_shape.append(jax.ShapeDtypeStruct((t, N_GATES), F32))
    return pl.pallas_call(
        functools.partial(_norm_mm_kernel, with_gates=gates is not None, relu2=relu2),
        grid=(t // tm, n // tn),
        in_specs=in_specs, out_specs=out_specs, out_shape=out_shape,
        scratch_shapes=[pltpu.VMEM((tm, d), BF16)],
        compiler_params=_cparams(2),
        name=name,
    )(*args)


def _mm_resid_kernel(*refs, nj, tn, final_norm):
    if final_norm:
        a_ref, w_ref, r_ref, gate_ref, fg_ref, o_ref = refs
    else:
        a_ref, w_ref, r_ref, gate_ref, o_ref = refs
    j = pl.program_id(1)
    cols = pl.ds(pl.multiple_of(j * tn, tn), tn)
    o_ref[:, cols] = r_ref[...] + gate_ref[...] * _dot(a_ref[...], w_ref[...])
    if final_norm:
        @pl.when(j == nj - 1)
        def _():
            xn = o_ref[...]
            ms = jnp.mean(xn * xn, axis=-1, keepdims=True)
            o_ref[...] = xn * lax.rsqrt(ms + EPS) * fg_ref[...]


def _mm_resid_call(a, w, layer, resid, gate, rows_per_mod, tm, tn, final_g=None, name="mm_resid"):
    t, kdim = a.shape
    n = w.shape[-1]
    assert t % tm == 0 and n % tn == 0 and rows_per_mod % tm == 0
    nj = n // tn
    mod_idx = lambda i, j: ((i * tm) // rows_per_mod, 0, j)
    in_specs = [pl.BlockSpec((tm, kdim), lambda i, j: (i, 0)),
                pl.BlockSpec((None, kdim, tn), lambda i, j: (layer, 0, j)),
                pl.BlockSpec((tm, tn), lambda i, j: (i, j)),
                pl.BlockSpec((None, 1, tn), mod_idx)]
    args = [a, w, resid, gate]
    if final_g is not None:
        in_specs.append(pl.BlockSpec((1, n), lambda i, j: (0, 0)))
        args.append(final_g)
    return pl.pallas_call(
        functools.partial(_mm_resid_kernel, nj=nj, tn=tn, final_norm=final_g is not None),
        grid=(t // tm, nj),
        in_specs=in_specs,
        out_specs=pl.BlockSpec((tm, n), lambda i, j: (i, 0)),
        out_shape=jax.ShapeDtypeStruct((t, n), F32),
        compiler_params=_cparams(2),
        name=name,
    )(*args)


def _merge_kernel(ym_ref, yr_ref, ts_ref, sg_ref, wm_ref, wr_ref, ws_ref, g0_ref, g1_ref, g2_ref,
                  o_ref, ys_scr):
    @pl.when(pl.program_id(1) == 0)
    def _():
        t = ts_ref[...]
        ms = jnp.mean(t * t, axis=-1, keepdims=True)
        ys_scr[...] = (t * lax.rsqrt(ms + EPS) * sg_ref[...]).astype(BF16)

    acc = _sigmoid(g0_ref[...].astype(F32)) * _dot(ym_ref[...], wm_ref[...])
    acc += _sigmoid(g1_ref[...].astype(F32)) * _dot(yr_ref[...], wr_ref[...])
    acc += _sigmoid(g2_ref[...].astype(F32)) * _dot(ys_scr[...], ws_ref[...])
    o_ref[...] = acc.astype(o_ref.dtype)


def _merge_call(ym, yr, ts, sg, wm, wr, ws, layer, proj, tm, tn):
    t, w = ym.shape
    d = wm.shape[-1]
    assert t % tm == 0 and d % tn == 0
    per_br = d // tn
    yspec = pl.BlockSpec((tm, w), lambda i, j: (i, 0))
    wspec = pl.BlockSpec((None, w, tn), lambda i, j: (layer, 0, j))
    gspec = lambda br: pl.BlockSpec((tm, tn), lambda i, j: (i, br * per_br + j))
    return pl.pallas_call(
        _merge_kernel,
        grid=(t // tm, d // tn),
        in_specs=[yspec, yspec, yspec, pl.BlockSpec((1, w), lambda i, j: (0, 0)),
                  wspec, wspec, wspec, gspec(0), gspec(1), gspec(2)],
        out_specs=pl.BlockSpec((tm, tn), lambda i, j: (i, j)),
        out_shape=jax.ShapeDtypeStruct((t, d), BF16),
        scratch_shapes=[pltpu.VMEM((tm, w), BF16)],
        compiler_params=_cparams(2),
        name="merge",
    )(ym, yr, ts, sg, wm, wr, ws, proj, proj, proj)


def _chunk_masks():
    row = lax.broadcasted_iota(jnp.int32, (CHUNK, CHUNK), 0)
    col = lax.broadcasted_iota(jnp.int32, (CHUNK, CHUNK), 1)
    lower = col <= row
    upper = col >= row
    tri = jnp.where(lower, 1.0, 0.0).astype(BF16)
    return row, col, lower, upper, tri


def _mlstm_kernel(*refs, nc, ns, has_state, emit_state, alias_in):
    it = iter(refs)
    q_ref, k_ref, v_ref, o_ref, g_ref, ng_ref = (next(it) for _ in range(6))
    if has_state:
        c0_ref, n0_ref, m0_ref = (next(it) for _ in range(3))
    if alias_in:
        next(it)
    y_ref = next(it)
    if emit_state:
        cf_ref, nf_ref, mf_ref = (next(it) for _ in range(3))
    hacc = (next(it), next(it))
    c_scr = (next(it), next(it))
    n_scr = (next(it), next(it))
    m_scr = (next(it), next(it))
    p_scr = (next(it), next(it))
    ml_scr = (next(it), next(it))
    bc_scr = (next(it), next(it))
    wl_scr = (next(it), next(it))
    sc_scr = (next(it), next(it))

    _, _, lower, upper, tri = _chunk_masks()
    ones_bf = jnp.ones((CHUNK, LANES), BF16)
    dirs = (0, 1)
    masks = (lower, upper)
    sub8 = lax.broadcasted_iota(jnp.int32, (8, LANES), 0)

    def gates(i, carry, base):
        cidx = (2 * i, 2 * i + 1)
        two = (0, 1)
        rows = [pl.ds(pl.multiple_of(c * CHUNK, CHUNK), CHUNK) for c in cidx]
        grow = [pl.ds(pl.multiple_of(base + c * CHUNK, CHUNK), CHUNK) for c in cidx]
        g = [g_ref[r, :] for r in grow]
        cs = [_cumsum_rows(x, tri) for x in g]
        tot = [x[CHUNK - 1:CHUNK, :] for x in cs]
        g_t = [x.T for x in g]
        cs_t = [x.T for x in cs]
        tot_t = [x[:, CHUNK - 1:CHUNK] for x in cs_t]
        bsum = [[cs[j], tot[j] - cs[j] + g[j]] for j in two]
        bsum_t = [[cs_t[j], tot_t[j] - cs_t[j] + g_t[j]] for j in two]
        items = [(j, d) for j in two for d in dirs]
        b_col = {(j, d): bsum[j][d][:, 2 + d:3 + d] for (j, d) in items}
        i_col = {(j, d): g[j][:, d:d + 1] for (j, d) in items}
        t_row = {(j, d): g_t[j][d:d + 1, :] - bsum_t[j][d][2 + d:3 + d, :] for (j, d) in items}
        b_last = {(j, d): tot[j][:, 2 + d:3 + d] for (j, d) in items}
        logw = {(j, d): jnp.where(masks[d], b_col[j, d] + t_row[j, d], -jnp.inf) for (j, d) in items}
        m_loc = {it_: jnp.max(logw[it_], axis=1, keepdims=True) for it_ in items}
        m_chunk = {it_: jnp.max(b_last[it_] + t_row[it_], axis=1, keepdims=True) for it_ in items}
        for (j, d) in items:
            p_scr[d][rows[j], :] = jnp.exp(logw[j, d] - m_loc[j, d])
            ml_scr[d][rows[j], :] = jnp.broadcast_to(m_loc[j, d], (CHUNK, LANES))
            bc_scr[d][rows[j], :] = jnp.broadcast_to(b_col[j, d], (CHUNK, LANES))
            wl_scr[d][rows[j], :] = jnp.broadcast_to(b_last[j, d] - b_col[j, d] + i_col[j, d], (CHUNK, LANES))
            sc_scr[d][pl.ds(pl.multiple_of(cidx[j] * 8, 8), 8), :] = jnp.where(
                sub8 == 0, jnp.broadcast_to(m_chunk[j, d], (8, LANES)), jnp.broadcast_to(b_last[j, d], (8, LANES)))
        return carry

    def rep2(x):
        return jnp.concatenate([x, x], axis=1)

    def scan(i, carry, base):
        cidx = (i, nc - 1 - i)
        rows = [pl.ds(pl.multiple_of(c * CHUNK, CHUNK), CHUNK) for c in cidx]
        grow = [pl.ds(pl.multiple_of(base + c * CHUNK, CHUNK), CHUNK) for c in cidx]
        sc = [sc_scr[d][pl.ds(pl.multiple_of(cidx[d] * 8, 8), 8), :] for d in dirs]
        m_chunk = [sc[d][0:1, 0:1] for d in dirs]
        b_last = [sc[d][1:2, 0:1] for d in dirs]
        qb = [q_ref[r, :].astype(BF16) for r in grow]
        k = [k_ref[r, :].astype(F32) * (M_DK ** -0.5) for r in grow]
        kb = [x.astype(BF16) for x in k]
        vb = [v_ref[r, :].astype(BF16) for r in grow]
        qk = [_dot_nt(qb[d], kb[d]) for d in dirs]
        sb = [(qk[d] * p_scr[d][rows[d], :]).astype(BF16) for d in dirs]
        a_num = [_dot(sb[d], vb[d]) for d in dirs]
        a_den = [_dot(sb[d], ones_bf) for d in dirs]

        m_prev = [m_scr[d][0:1, 0:1] for d in dirs]
        c_prev = [c_scr[d][...] for d in dirs]
        n_prev = [n_scr[d][0:1, :] for d in dirs]
        m_new = [jnp.maximum(b_last[d] + m_prev[d], m_chunk[d]) for d in dirs]
        kw = [k[d] * rep2(jnp.exp(wl_scr[d][rows[d], :] - m_new[d])) for d in dirs]
        kw_t = [x.T.astype(BF16) for x in kw]
        q_c = [_dot(qb[d], c_prev[d].astype(BF16)) for d in dirs]
        u_c = [_dot(kw_t[d], vb[d]) for d in dirs]
        m_loc = [ml_scr[d][rows[d], :] for d in dirs]
        gg = [bc_scr[d][rows[d], :] + m_prev[d] for d in dirs]
        m_tot = [jnp.maximum(m_loc[d], gg[d]) for d in dirs]
        e_intra = [jnp.exp(m_loc[d] - m_tot[d]) for d in dirs]
        e_inter = [jnp.exp(gg[d] - m_tot[d]) for d in dirs]
        q_n = [_dot_nt(qb[d], jnp.broadcast_to(n_prev[d].astype(BF16), (LANES, M_DK))) for d in dirs]
        den = [e_intra[d] * a_den[d] + e_inter[d] * q_n[d] for d in dirs]
        inv = [1.0 / jnp.maximum(jnp.abs(den[d]), jnp.exp(-m_tot[d])) for d in dirs]
        for d in dirs:
            hacc[d][rows[d], :] = rep2(e_intra[d] * inv[d]) * a_num[d] + rep2(e_inter[d] * inv[d]) * q_c[d]

        u_n = [jnp.sum(x, axis=0, keepdims=True) for x in kw]
        a = [jnp.exp(b_last[d] + m_prev[d] - m_new[d]) for d in dirs]
        for d in dirs:
            c_scr[d][...] = a[d] * c_prev[d] + u_c[d]
            n_scr[d][0:1, :] = a[d] * n_prev[d] + u_n[d]
            m_scr[d][0:1, :] = jnp.broadcast_to(m_new[d], (1, LANES))
        return carry

    def finish(c, carry, base):
        rows = pl.ds(pl.multiple_of(c * CHUNK, CHUNK), CHUNK)
        grow = pl.ds(pl.multiple_of(base + c * CHUNK, CHUNK), CHUNK)
        ht = hacc[0][rows, :] + hacc[1][rows, :]
        ms = jnp.mean(ht * ht, axis=1, keepdims=True)
        y = ht * lax.rsqrt(ms + EPS) * ng_ref[...] * _sigmoid(o_ref[grow, :].astype(F32))
        y_ref[grow, :] = y.astype(y_ref.dtype)
        return carry

    for sq in range(ns):
        base = sq * nc * CHUNK
        for d in range(2):
            if has_state:
                c_scr[d][...] = c0_ref[sq, d]
                n_scr[d][0:1, :] = n0_ref[sq, d:d + 1, :]
                m_scr[d][0:1, :] = m0_ref[sq, d:d + 1, :]
            else:
                c_scr[d][...] = jnp.zeros_like(c_scr[d])
                n_scr[d][...] = jnp.zeros_like(n_scr[d])
                m_scr[d][...] = jnp.zeros_like(m_scr[d])
        lax.fori_loop(0, nc // 2, functools.partial(gates, base=base), 0)
        lax.fori_loop(0, nc, functools.partial(scan, base=base), 0, unroll=SCAN_UNROLL)
        lax.fori_loop(0, nc, functools.partial(finish, base=base), 0, unroll=SCAN_UNROLL)
        if emit_state:
            for d in range(2):
                if alias_in:
                    cf_ref[sq, d] = c_scr[d][...]
                else:
                    for slot in range(DEPTH):
                        cf_ref[sq, slot, d] = c_scr[d][...]
                nf_ref[sq, d:d + 1, :] = n_scr[d][0:1, :]
                mf_ref[sq, d:d + 1, :] = m_scr[d][0:1, :]


def _mlstm_call(proj, gates, ng, nb, seq, ns, state=None, emit=None):
    emit_state = emit is not None
    assert nb % ns == 0
    t = proj.shape[0]
    nc = seq // CHUNK
    dk = M_DK
    cspec = lambda base: pl.BlockSpec((ns * seq, dk), lambda b, h: (b, base // dk + h))
    in_specs = [cspec(OFF_MQ), cspec(OFF_MK), cspec(OFF_MV), cspec(OFF_MO),
                pl.BlockSpec((ns * seq, LANES), lambda b, h: (b, h)),
                pl.BlockSpec((1, dk), lambda b, h: (0, h))]
    args = [proj, proj, proj, proj, gates, ng]
    if state is not None:
        sl = state[0]
        in_specs += [pl.BlockSpec((ns, None, 2, None, dk, dk), lambda b, h: (b, sl, 0, h, 0, 0)),
                     pl.BlockSpec((ns, None, 2, dk), lambda b, h: (b, h, 0, 0)),
                     pl.BlockSpec((ns, None, 2, LANES), lambda b, h: (b, h, 0, 0))]
        args += list(state[1:])
    out_specs = [pl.BlockSpec((ns * seq, dk), lambda b, h: (b, h))]
    out_shape = [jax.ShapeDtypeStruct((t, M_HEADS * dk), BF16)]
    aliases = {}
    if emit_state:
        layer, prev = emit
        slot_blk, slot_idx = (None, layer) if prev is not None else (DEPTH, 0)
        out_specs += [pl.BlockSpec((ns, slot_blk, 2, None, dk, dk), lambda b, h: (b, slot_idx, 0, h, 0, 0)),
                      pl.BlockSpec((ns, None, 2, dk), lambda b, h: (b, h, 0, 0)),
                      pl.BlockSpec((ns, None, 2, LANES), lambda b, h: (b, h, 0, 0))]
        out_shape += [jax.ShapeDtypeStruct((nb, DEPTH, 2, M_HEADS, dk, dk), F32),
                      jax.ShapeDtypeStruct((nb, M_HEADS, 2, dk), F32),
                      jax.ShapeDtypeStruct((nb, M_HEADS, 2, LANES), F32)]
        if prev is not None:
            in_specs.append(pl.BlockSpec(memory_space=pl.ANY))
            args.append(prev)
            aliases = {len(args) - 1: 1}
    return pl.pallas_call(
        functools.partial(_mlstm_kernel, nc=nc, ns=ns, has_state=state is not None, emit_state=emit_state,
                          alias_in=bool(aliases)),
        grid=(nb // ns, M_HEADS),
        in_specs=in_specs, out_specs=out_specs, out_shape=out_shape, input_output_aliases=aliases,
        scratch_shapes=([pltpu.VMEM((seq, dk), F32)] * 2 + [pltpu.VMEM((dk, dk), F32)] * 2
                        + [pltpu.VMEM((8, dk), F32)] * 2 + [pltpu.VMEM((8, LANES), F32)] * 2
                        + [pltpu.VMEM((seq, LANES), F32)] * 8 + [pltpu.VMEM((nc * 8, LANES), F32)] * 2),
        compiler_params=_cparams(2),
        name="mlstm",
    )(*args)


R_HPS = 2


def _ret_kernel(*refs, nc, ns, has_state, emit_state, rope, alias_in):
    it = iter(refs)
    q_ref, k_ref, v_ref, g_ref, lg_ref, ng_ref = (next(it) for _ in range(6))
    if rope:
        cos_ref, sin_ref = next(it), next(it)
    if has_state:
        s0_ref = next(it)
    if alias_in:
        next(it)
    y_ref = next(it)
    if emit_state:
        sf_ref = next(it)
    chains = [(h, d) for h in range(R_HPS) for d in range(2)]
    oacc = {c: next(it) for c in chains}
    s_scr = {c: next(it) for c in chains}
    qs_scr, ks_scr, kt_scr = (next(it) for _ in range(3))

    row, col, lower, upper, _ = _chunk_masks()
    rel = (row - col).astype(F32)
    pos = row.astype(F32)
    decay, w_q, w_k, chunk_decay = {}, {}, {}, {}
    for (h, d) in chains:
        lgd = -jnp.exp(lg_ref[h, d:d + 1, :])
        if d == 0:
            decay[h, d] = jnp.where(lower, jnp.exp(rel * lgd), 0.0)
            w_q[h, d] = jnp.exp((pos + 1.0) * lgd)
            w_k[h, d] = jnp.exp((CHUNK - 1.0 - pos) * lgd)
        else:
            decay[h, d] = jnp.where(upper, jnp.exp(-rel * lgd), 0.0)
            w_q[h, d] = jnp.exp((CHUNK - pos) * lgd)
            w_k[h, d] = jnp.exp(pos * lgd)
        chunk_decay[h, d] = jnp.exp(CHUNK * lgd)

    def hcols(h):
        return slice(h * R_DK, (h + 1) * R_DK)

    def prep(c, carry, base):
        rows = pl.ds(pl.multiple_of(c * CHUNK, CHUNK), CHUNK)
        grow = pl.ds(pl.multiple_of(base + c * CHUNK, CHUNK), CHUNK)
        for h in range(R_HPS):
            q = q_ref[grow, hcols(h)].astype(F32)
            k = k_ref[grow, hcols(h)].astype(F32) * (R_DK ** -0.5)
            if rope:
                cs, sn = cos_ref[grow, :], sin_ref[grow, :]
                q = q * cs + pltpu.roll(q, R_DK // 2, 1) * sn
                k = k * cs + pltpu.roll(k, R_DK // 2, 1) * sn
            qs_scr[rows, hcols(h)] = q.astype(BF16)
            ks_scr[rows, hcols(h)] = k.astype(BF16)
            kt_scr[hcols(h), rows] = k.T.astype(BF16)
        return carry

    def scan(i, carry, base):
        rows = {0: pl.ds(pl.multiple_of(i * CHUNK, CHUNK), CHUNK),
                1: pl.ds(pl.multiple_of((nc - 1 - i) * CHUNK, CHUNK), CHUNK)}
        grow = {0: pl.ds(pl.multiple_of(base + i * CHUNK, CHUNK), CHUNK),
                1: pl.ds(pl.multiple_of(base + (nc - 1 - i) * CHUNK, CHUNK), CHUNK)}
        qb = {(h, d): qs_scr[rows[d], hcols(h)] for (h, d) in chains}
        kb = {(h, d): ks_scr[rows[d], hcols(h)] for (h, d) in chains}
        kt = {(h, d): kt_scr[hcols(h), rows[d]] for (h, d) in chains}
        vb = {(h, d): v_ref[grow[d], hcols(h)].astype(BF16) for (h, d) in chains}
        vw = {c: (vb[c].astype(F32) * w_k[c]).astype(BF16) for c in chains}
        s_prev = {c: s_scr[c][...] for c in chains}
        sc = {c: (_dot_nt(qb[c], kb[c]) * decay[c]).astype(BF16) for c in chains}
        inter = {c: _dot(qb[c], s_prev[c].astype(BF16)) for c in chains}
        u = {c: _dot(kt[c], vw[c]) for c in chains}
        intra = {c: _dot(sc[c], vb[c]) for c in chains}
        for (h, d) in chains:
            oacc[h, d][rows[d], :] = intra[h, d] + w_q[h, d] * inter[h, d]
            s_scr[h, d][...] = chunk_decay[h, d] * s_prev[h, d] + u[h, d]
        return carry

    def finish(c, carry, base):
        rows = pl.ds(pl.multiple_of(c * CHUNK, CHUNK), CHUNK)
        grow = pl.ds(pl.multiple_of(base + c * CHUNK, CHUNK), CHUNK)
        for h in range(R_HPS):
            ot = oacc[h, 0][rows, :] + oacc[h, 1][rows, :]
            ms = jnp.mean(ot * ot, axis=1, keepdims=True)
            y = ot * lax.rsqrt(ms + EPS) * ng_ref[:, hcols(h)] * _silu(g_ref[grow, hcols(h)].astype(F32))
            y_ref[grow, hcols(h)] = y.astype(y_ref.dtype)
        return carry

    for sq in range(ns):
        base = sq * nc * CHUNK
        for (h, d) in chains:
            if has_state:
                s_scr[h, d][...] = s0_ref[sq, d, h]
            else:
                s_scr[h, d][...] = jnp.zeros_like(s_scr[h, d])
        lax.fori_loop(0, nc, functools.partial(prep, base=base), 0, unroll=SCAN_UNROLL)
        lax.fori_loop(0, nc, functools.partial(scan, base=base), 0, unroll=SCAN_UNROLL)
        lax.fori_loop(0, nc, functools.partial(finish, base=base), 0, unroll=SCAN_UNROLL)
        if emit_state:
            for (h, d) in chains:
                if alias_in:
                    sf_ref[sq, d, h] = s_scr[h, d][...]
                else:
                    for slot in range(DEPTH):
                        sf_ref[sq, slot, d, h] = s_scr[h, d][...]


def _ret_call(proj, lgr, ng, nb, seq, ns, rope=None, state=None, emit=None):
    emit_state = emit is not None
    assert nb % ns == 0 and (rope is None or ns == 1)
    t = proj.shape[0]
    nc = seq // CHUNK
    dk = R_DK
    bw = R_HPS * dk
    cspec = lambda base: pl.BlockSpec((ns * seq, bw), lambda b, h: (b, base // bw + h))
    in_specs = [cspec(OFF_RQ), cspec(OFF_RK), cspec(OFF_RV), cspec(OFF_RG),
                pl.BlockSpec((R_HPS, 2, LANES), lambda b, h: (h, 0, 0)),
                pl.BlockSpec((1, bw), lambda b, h: (0, h))]
    args = [proj, proj, proj, proj, lgr, ng]
    if rope is not None:
        in_specs += [pl.BlockSpec((seq, dk), lambda b, h: (0, 0))] * 2
        args += list(rope)
    if state is not None:
        sl = state[0]
        in_specs.append(pl.BlockSpec((ns, None, 2, R_HPS, dk, dk), lambda b, h: (b, sl, 0, h, 0, 0)))
        args.append(state[1])
    out_specs = [pl.BlockSpec((ns * seq, bw), lambda b, h: (b, h))]
    out_shape = [jax.ShapeDtypeStruct((t, R_HEADS * dk), BF16)]
    aliases = {}
    if emit_state:
        layer, prev = emit
        slot_blk, slot_idx = (None, layer) if prev is not None else (DEPTH, 0)
        out_specs.append(pl.BlockSpec((ns, slot_blk, 2, R_HPS, dk, dk), lambda b, h: (b, slot_idx, 0, h, 0, 0)))
        out_shape.append(jax.ShapeDtypeStruct((nb, DEPTH, 2, R_HEADS, dk, dk), F32))
        if prev is not None:
            in_specs.append(pl.BlockSpec(memory_space=pl.ANY))
            args.append(prev)
            aliases = {len(args) - 1: 1}
    n_chain = 2 * R_HPS
    return pl.pallas_call(
        functools.partial(_ret_kernel, nc=nc, ns=ns, has_state=state is not None, emit_state=emit_state,
                          rope=rope is not None, alias_in=bool(aliases)),
        grid=(nb // ns, R_HEADS // R_HPS),
        in_specs=in_specs, out_specs=out_specs, out_shape=out_shape, input_output_aliases=aliases,
        scratch_shapes=([pltpu.VMEM((seq, dk), F32)] * n_chain + [pltpu.VMEM((dk, dk), F32)] * n_chain
                        + [pltpu.VMEM((seq, bw), BF16)] * 2 + [pltpu.VMEM((bw, seq), BF16)]),
        compiler_params=_cparams(2),
        name="retention",
    )(*args)


def _ssd_kernel(*refs, nc, ns, has_state, emit_state, alias_in):
    it = iter(refs)
    (x_ref, b_ref, c_ref, z_ref, g_ref, wx_ref, wb_ref, wc_ref, bx_ref, bb_ref, bc_ref,
     al_ref, sd_ref) = (next(it) for _ in range(13))
    if has_state:
        h0_ref = next(it)
    if alias_in:
        next(it)
    y_ref = next(it)
    if emit_state:
        hf_ref = next(it)
    xs_scr, bt_scr, bb_scr, cc_scr, ht_scr, yb_scr = (next(it) for _ in range(6))

    _, _, lower, upper, tri = _chunk_masks()
    er = lax.broadcasted_iota(jnp.int32, (LANES, S_GW), 0)
    ec = lax.broadcasted_iota(jnp.int32, (LANES, S_GW), 1) // S_P
    expand = [jnp.where(er == ec + d * S_HPG, 1.0, 0.0).astype(BF16) for d in range(2)]
    a_row = -jnp.exp(al_ref[...])

    wr = lax.broadcasted_iota(jnp.int32, (CHUNK, CHUNK + 32), 0)
    wc = lax.broadcasted_iota(jnp.int32, (CHUNK, CHUNK + 32), 1)
    shifts = [jnp.where(wc == wr + 16 + sft, 1.0, 0.0).astype(BF16) for sft in (-1, 1, 2)]

    def conv_silu(ref, w_ref, bias_ref, c, base):
        off = base + c * CHUNK
        cur = ref[pl.ds(pl.multiple_of(off, CHUNK), CHUNK), :]
        prev_w = ref[pl.ds(pl.multiple_of(jnp.maximum(off - 16, 0), 16), 16), :]
        next_w = ref[pl.ds(pl.multiple_of(jnp.minimum(off + CHUNK, ns * nc * CHUNK - 16), 16), 16), :]
        prev_w = jnp.where(c > 0, prev_w, jnp.zeros_like(prev_w))
        next_w = jnp.where(c < nc - 1, next_w, jnp.zeros_like(next_w))
        win = jnp.concatenate([prev_w, cur, next_w], axis=0)
        xm1, xp1, xp2 = (_dot(sm, win) for sm in shifts)
        w = w_ref[...]
        y = w[0:1, :] * xm1 + w[1:2, :] * cur.astype(F32) + w[2:3, :] * xp1 + w[3:4, :] * xp2 + bias_ref[...]
        return _silu(y)

    def prep(c, carry, base):
        off = pl.multiple_of(c * CHUNK, CHUNK)
        rows = pl.ds(off, CHUNK)
        xs_scr[rows, :] = conv_silu(x_ref, wx_ref, bx_ref, c, base)
        bm = conv_silu(b_ref, wb_ref, bb_ref, c, base)
        bb_scr[rows, :] = bm.astype(BF16)
        bt_scr[:, rows] = bm.T.astype(BF16)
        cc_scr[rows, :] = conv_silu(c_ref, wc_ref, bc_ref, c, base).astype(BF16)
        return carry

    dirs = (0, 1)
    masks = (lower, upper)
    first_half = lax.broadcasted_iota(jnp.int32, (CHUNK, LANES), 1) < S_P

    def scan(i, carry, base):
        rows = [pl.ds(pl.multiple_of(c * CHUNK, CHUNK), CHUNK) for c in (i, nc - 1 - i)]
        grow = [pl.ds(pl.multiple_of(base + c * CHUNK, CHUNK), CHUNK) for c in (i, nc - 1 - i)]
        dt = [g_ref[r, :] for r in grow]
        a = [x * a_row for x in dt]
        cs = [_cumsum_rows(x, tri) for x in a]
        tot = [x[CHUNK - 1:CHUNK, :] for x in cs]
        bsum = [cs[0], tot[1] - cs[1] + a[1]]
        bsum_t = [x.T for x in bsum]
        dt_t = [x.T for x in dt]
        xs = [xs_scr[r, :] for r in rows]
        xb = [x.astype(BF16) for x in xs]
        bmb = [bb_scr[r, :] for r in rows]
        cmb = [cc_scr[r, :] for r in rows]
        btb = [bt_scr[:, r] for r in rows]
        cb = [_dot_nt(cmb[d], bmb[d]) for d in dirs]
        ht = [ht_scr[d] for d in dirs]
        inter = [_dot(cmb[d], ht[d].astype(BF16)) for d in dirs]
        ex = [_expand(jnp.exp(bsum[d]), expand[d], exact=False) for d in dirs]
        wexp = [_expand(jnp.exp(tot[d] - bsum[d]) * dt[d], expand[d], exact=False) for d in dirs]
        cd = [_expand(jnp.broadcast_to(jnp.exp(tot[d]), (8, LANES)), expand[d])[0:1, :] for d in dirs]
        u_t = [_dot(btb[d], (xs[d] * wexp[d]).astype(BF16)) for d in dirs]
        ys = [[], []]
        for p in range(S_HPG // 2):
            for d in dirs:
                ms = []
                for k in (2 * p, 2 * p + 1):
                    ln = d * S_HPG + k
                    decay = jnp.exp(jnp.where(masks[d], bsum[d][:, ln:ln + 1] - bsum_t[d][ln:ln + 1, :], -jnp.inf))
                    ms.append((cb[d] * decay * dt_t[d][ln:ln + 1, :]).astype(BF16))
                xp = xb[d][:, p * LANES:(p + 1) * LANES]
                zero = jnp.zeros_like(xp)
                rhs = jnp.concatenate([jnp.where(first_half, xp, zero), jnp.where(first_half, zero, xp)], axis=0)
                ys[d].append(_dot(jnp.concatenate(ms, axis=1), rhs))
        y_ref[grow[0], :] = jnp.concatenate(ys[0], axis=1) + ex[0] * inter[0]
        yb_scr[rows[1], :] = jnp.concatenate(ys[1], axis=1) + ex[1] * inter[1]
        for d in dirs:
            ht_scr[d] = ht[d] * cd[d] + u_t[d]
        return carry

    def finish(c, carry, base):
        rows = pl.ds(pl.multiple_of(c * CHUNK, CHUNK), CHUNK)
        grow = pl.ds(pl.multiple_of(base + c * CHUNK, CHUNK), CHUNK)
        yt = y_ref[grow, :] + yb_scr[rows, :] + sd_ref[...] * xs_scr[rows, :]
        y_ref[grow, :] = yt * _silu(z_ref[grow, :].astype(F32))
        return carry

    for sq in range(ns):
        base = sq * nc * CHUNK
        lax.fori_loop(0, nc, functools.partial(prep, base=base), 0, unroll=SCAN_UNROLL)
        for d in range(2):
            if has_state:
                ht_scr[d] = h0_ref[sq, d].T
            else:
                ht_scr[d] = jnp.zeros((S_N, S_GW), F32)
        lax.fori_loop(0, nc, functools.partial(scan, base=base), 0, unroll=SCAN_UNROLL)
        lax.fori_loop(0, nc, functools.partial(finish, base=base), 0, unroll=SCAN_UNROLL)
        if emit_state:
            for d in range(2):
                h_fin = ht_scr[d].T
                if alias_in:
                    hf_ref[sq, d] = h_fin
                else:
                    for slot in range(DEPTH):
                        hf_ref[sq, slot, d] = h_fin


def _ssd_call(proj, gates, conv_w, conv_b, alog, sd, nb, seq, ns, state=None, emit=None):
    emit_state = emit is not None
    assert nb % ns == 0
    t = proj.shape[0]
    nc = seq // CHUNK
    gw = S_GW
    n = S_N
    xw = S_HEADS * S_P
    in_specs = [pl.BlockSpec((ns * seq, gw), lambda b, g: (b, OFF_SX // gw + g)),
                pl.BlockSpec((ns * seq, n), lambda b, g: (b, OFF_SB // n + g)),
                pl.BlockSpec((ns * seq, n), lambda b, g: (b, OFF_SC // n + g)),
                pl.BlockSpec((ns * seq, gw), lambda b, g: (b, OFF_SZ // gw + g)),
                pl.BlockSpec((ns * seq, LANES), lambda b, g: (b, M_HEADS + g)),
                pl.BlockSpec((4, gw), lambda b, g: (0, g)),
                pl.BlockSpec((4, n), lambda b, g: (0, xw // n + g)),
                pl.BlockSpec((4, n), lambda b, g: (0, xw // n + S_GROUPS + g)),
                pl.BlockSpec((1, gw), lambda b, g: (0, g)),
                pl.BlockSpec((1, n), lambda b, g: (0, xw // n + g)),
                pl.BlockSpec((1, n), lambda b, g: (0, xw // n + S_GROUPS + g)),
                pl.BlockSpec((None, 1, LANES), lambda b, g: (g, 0, 0)),
                pl.BlockSpec((1, gw), lambda b, g: (0, g))]
    args = [proj, proj, proj, proj, gates, conv_w, conv_w, conv_w, conv_b, conv_b, conv_b, alog, sd]
    if state is not None:
        sl = state[0]
        in_specs.append(pl.BlockSpec((ns, None, 2, None, gw, n), lambda b, g: (b, sl, 0, g, 0, 0)))
        args.append(state[1])
    out_specs = [pl.BlockSpec((ns * seq, gw), lambda b, g: (b, g))]
    out_shape = [jax.ShapeDtypeStruct((t, xw), F32)]
    aliases = {}
    if emit_state:
        layer, prev = emit
        slot_blk, slot_idx = (None, layer) if prev is not None else (DEPTH, 0)
        out_specs.append(pl.BlockSpec((ns, slot_blk, 2, None, gw, n), lambda b, g: (b, slot_idx, 0, g, 0, 0)))
        out_shape.append(jax.ShapeDtypeStruct((nb, DEPTH, 2, S_GROUPS, gw, n), F32))
        if prev is not None:
            in_specs.append(pl.BlockSpec(memory_space=pl.ANY))
            args.append(prev)
            aliases = {len(args) - 1: 1}
    return pl.pallas_call(
        functools.partial(_ssd_kernel, nc=nc, ns=ns, has_state=state is not None, emit_state=emit_state,
                          alias_in=bool(aliases)),
        grid=(nb // ns, S_GROUPS),
        in_specs=in_specs, out_specs=out_specs, out_shape=out_shape, input_output_aliases=aliases,
        scratch_shapes=[pltpu.VMEM((seq, gw), F32), pltpu.VMEM((n, seq), BF16),
                        pltpu.VMEM((seq, n), BF16), pltpu.VMEM((seq, n), BF16),
                        pltpu.VMEM((2, n, gw), F32), pltpu.VMEM((seq, gw), F32)],
        compiler_params=_cparams(2),
        name="ssd",
    )(*args)


_IN_SIZES = (1024, 1024, 1024, 1024, 8, 8, 1024, 1024, 1024, 1024, 1024, 1536, 32, 6144)
_IN_OFFS = np.concatenate([[0], np.cumsum(_IN_SIZES)])


def _gate_lane_index():
    idx = -np.ones((LANES,), np.int64)
    for h in range(M_HEADS):
        base = GATE_LANE_OFFSETS[h]
        idx[base + 0] = 0 * M_HEADS + h
        idx[base + 1] = 1 * M_HEADS + h
        idx[base + 2] = 2 * M_HEADS + 0 * M_HEADS + h
        idx[base + 3] = 2 * M_HEADS + 1 * M_HEADS + h
    for g in range(S_GROUPS):
        base = GATE_LANE_OFFSETS[M_HEADS + g]
        for d in range(2):
            for k in range(S_HPG):
                idx[base + d * S_HPG + k] = 4 * M_HEADS + d * S_HEADS + g * S_HPG + k
    return idx


def _place(vals, idx):
    taken = jnp.take(vals, jnp.asarray(np.maximum(idx, 0)), axis=-1)
    return jnp.where(jnp.asarray(idx >= 0), taken, 0.0)


def _prep_params(w_in, m_igate_b, m_fgate_b, s_dt_bias, s_a_log, r_decay, s_d):
    o = _IN_OFFS
    w_big = jnp.concatenate([w_in[:, :, o[13]:o[14]], w_in[:, :, o[0]:o[4]], w_in[:, :, o[6]:o[12]]],
                            axis=2).astype(BF16)
    w_small = jnp.concatenate([w_in[:, :, o[4]:o[6]], w_in[:, :, o[12]:o[13]]], axis=2)
    gidx = _gate_lane_index()
    w_g = _place(w_small, gidx).astype(BF16)
    b_small = jnp.concatenate([m_igate_b.reshape(DEPTH, -1), m_fgate_b.reshape(DEPTH, -1),
                               s_dt_bias.reshape(DEPTH, -1)], axis=1)
    b_g = _place(b_small, gidx).reshape(DEPTH, 1, LANES)
    aidx = -np.ones((S_GROUPS, LANES), np.int64)
    for g in range(S_GROUPS):
        for d in range(2):
            for k in range(S_HPG):
                aidx[g, d * S_HPG + k] = d * S_HEADS + g * S_HPG + k
    a_flat = s_a_log.reshape(DEPTH, -1)
    alog = jnp.stack([_place(a_flat, aidx[g]) for g in range(S_GROUPS)], axis=1).reshape(DEPTH, S_GROUPS, 1, LANES)
    lgr = jnp.broadcast_to(jnp.swapaxes(r_decay, 1, 2)[..., None], (DEPTH, R_HEADS, 2, LANES))
    sd = jnp.repeat(s_d, S_P, axis=1).reshape(DEPTH, 1, S_HEADS * S_P)
    return w_big, w_g, b_g, alog, lgr, sd


def _rope_tables(seq):
    n_rows = seq // GRID_W
    rows = jnp.repeat(jnp.arange(n_rows, dtype=F32), GRID_W)
    cols = jnp.tile(jnp.arange(GRID_W, dtype=F32), n_rows)
    inv = ROPE_BASE ** (-jnp.arange(ROPE_FREQS, dtype=F32) / ROPE_FREQS)
    ang = jnp.concatenate([rows[:, None] * inv, cols[:, None] * inv], -1)
    cos, sin = jnp.cos(ang), jnp.sin(ang)
    return jnp.concatenate([cos, cos], -1), jnp.concatenate([-sin, sin], -1)


def _layer(x, l, mods, rows_per_mod, nb, seq, pw, rope, state, emit, final_g):
    sh_a, sc_a, g_a, sh_f, sc_f, g_f = mods
    tm_l, tm_s = min(TM_LARGE, rows_per_mod), min(TM_SMALL, rows_per_mod)
    proj, gates = _norm_mm_call(x, pw["norm_mix_g"], sc_a, sh_a, rows_per_mod, pw["w_big"], l, tm_l, TN_IN_PROJ,
                                BF16, False, gates=(pw["w_g"], pw["b_g"]), name="in_proj")
    st_m = st_r = st_s = None
    if state is not None:
        st_m, st_r, st_s = state
    em_m = em_r = em_s = None
    if emit is not None:
        em_m, em_r, em_s = ((l, prev) for prev in emit)
    ns = 1 if rope is not None else max(n for n in range(1, max(1, SEQ_ROWS_PER_STEP // seq) + 1) if nb % n == 0)
    om = _mlstm_call(proj, gates, pw["m_norm_g"][l], nb, seq, 1, state=st_m, emit=em_m)
    orr = _ret_call(proj, pw["lgr"][l], pw["r_norm_g"][l], nb, seq, ns, rope=rope, state=st_r, emit=em_r)
    os_ = _ssd_call(proj, gates, pw["s_conv_w"][l], pw["s_conv_b"][l], pw["alog"][l], pw["sd"][l], nb, seq, ns,
                    state=st_s, emit=em_s)
    merged = _merge_call(om[0], orr[0], os_[0], pw["s_norm_g"][l], pw["w_br_m"], pw["w_br_r"], pw["w_br_s"], l,
                         proj, tm_l, TN_MERGE)
    x = _mm_resid_call(merged, pw["w_out"], l, x, g_a, rows_per_mod, tm_s, D_MODEL, name="out_proj")
    (hid,) = _norm_mm_call(x, pw["norm_mlp_g"], sc_f, sh_f, rows_per_mod, pw["w_ff1"], l, tm_l, TN_MLP_UP,
                           BF16, True, name="mlp_up")
    x = _mm_resid_call(hid, pw["w_ff2"], l, x, g_f, rows_per_mod, tm_s, TN_MLP_DOWN, final_g=final_g, name="mlp_down")
    new_state = (om[1:], orr[1:], os_[1:]) if emit is not None else None
    return x, new_state


def kernel(x_prompt, x_sample, c, state_mlstm_C, state_mlstm_n, state_mlstm_m, state_ret, state_ssd, c_ctx, w_mod, b_mod, norm_mix_g, norm_mlp_g, w_in, m_igate_b, m_fgate_b, m_norm_g, r_decay, r_norm_g, s_conv_w, s_conv_b, s_dt_bias, s_a_log, s_d, s_norm_g, w_br_m, w_br_r, w_br_s, w_out, w_ff1, w_ff2, final_norm_g):
    bp, lp, d = x_prompt.shape
    bs, ls, _ = x_sample.shape
    xp = x_prompt.reshape(bp * lp, d)
    xs = x_sample.reshape(bs * ls, d)

    c_rows = jnp.zeros((8, d), F32).at[:bs].set(c).at[bs].set(c_ctx)
    mod = _mod_call(c_rows, w_mod, b_mod)
    rope = _rope_tables(ls)
    final_g = final_norm_g.reshape(1, d)

    w_big, w_g, b_g, alog, lgr, sd = _prep_params(w_in, m_igate_b, m_fgate_b, s_dt_bias, s_a_log, r_decay, s_d)
    pw = dict(w_big=w_big, w_g=w_g, b_g=b_g, alog=alog, lgr=lgr, sd=sd,
              norm_mix_g=norm_mix_g.reshape(DEPTH, 1, d), norm_mlp_g=norm_mlp_g.reshape(DEPTH, 1, d),
              m_norm_g=m_norm_g.reshape(DEPTH, 1, -1), r_norm_g=r_norm_g.reshape(DEPTH, 1, -1),
              s_norm_g=s_norm_g.reshape(DEPTH, 1, -1),
              s_conv_w=s_conv_w, s_conv_b=s_conv_b.reshape(DEPTH, 1, -1),
              w_br_m=w_br_m.astype(BF16), w_br_r=w_br_r.astype(BF16), w_br_s=w_br_s.astype(BF16),
              w_out=w_out.astype(BF16), w_ff1=w_ff1.astype(BF16), w_ff2=w_ff2.astype(BF16))
    cache_c = state_mlstm_C
    cache_r = state_ret
    cache_s = state_ssd.reshape(bs, DEPTH, 2, S_GROUPS, S_GW, S_N)

    big = (None, None, None)
    st_n, st_m = [], []
    for l in range(DEPTH):
        parts = mod[l].reshape(8, 6, 1, d)
        mods_ctx = tuple(parts[bs:bs + 1, i] for i in range(6))
        mods_lat = tuple(parts[:bs, i] for i in range(6))
        fg = final_g if l == DEPTH - 1 else None

        xp, st = _layer(xp, l, mods_ctx, bp * lp, bp, lp, pw, None, None, big, fg)
        (cf, nf, mf), (rf,), (hf,) = st
        big = (cf, rf, hf)
        st_n.append(jnp.transpose(nf, (0, 2, 1, 3)))
        st_m.append(jnp.transpose(mf[..., 0], (0, 2, 1)))

        cache = (
            (l, cache_c,
             jnp.transpose(state_mlstm_n[:, l], (0, 2, 1, 3)),
             jnp.broadcast_to(jnp.transpose(state_mlstm_m[:, l], (0, 2, 1))[..., None], (bs, M_HEADS, 2, LANES))),
            (l, cache_r),
            (l, cache_s),
        )
        xs, _ = _layer(xs, l, mods_lat, ls, bs, ls, pw, rope, cache, None, fg)

    return (xp.reshape(bp, lp, d), xs.reshape(bs, ls, d),
            big[0], jnp.stack(st_n, 1), jnp.stack(st_m, 1), big[1],
            big[2].reshape(bp, DEPTH, 2, S_HEADS, S_P, S_N))
```

```python
import functools

import numpy as np
import jax
import jax.numpy as jnp
from jax import lax
from jax.experimental import pallas as pl
from jax.experimental.pallas import tpu as pltpu

F32 = jnp.float32
BF16 = jnp.bfloat16

D_MODEL = 2048
DEPTH = 2
CHUNK = 128
EPS = 1e-6
M_HEADS, M_DK = 4, 256
R_HEADS, R_DK = 8, 128
S_HEADS, S_P, S_GROUPS, S_N = 16, 64, 2, 128
S_HPG = S_HEADS // S_GROUPS
S_GW = S_HPG * S_P
GRID_W = 64
ROPE_BASE = 10000.0
ROPE_FREQS = R_DK // 4
D_FF = 4 * D_MODEL
LANES = 128
VMEM_LIMIT = 56 * 1024 * 1024
SCAN_UNROLL = 4
SEQ_ROWS_PER_STEP = 1024
TM_LARGE, TM_SMALL = 1024, 512
TN_IN_PROJ, TN_MLP_UP, TN_MLP_DOWN, TN_MERGE = 1536, 2048, 512, 2048

N_GL = 3 * D_MODEL
OFF_MQ, OFF_MK, OFF_MV, OFF_MO = (N_GL + o for o in (0, 1024, 2048, 3072))
OFF_RQ, OFF_RK, OFF_RV, OFF_RG = (N_GL + o for o in (4096, 5120, 6144, 7168))
OFF_SZ, OFF_SX, OFF_SB, OFF_SC = (N_GL + o for o in (8192, 9216, 10240, 10496))
N_GATES = (M_HEADS + S_GROUPS) * LANES
GATE_LANE_OFFSETS = tuple(4 * h for h in range(M_HEADS)) + tuple(4 * M_HEADS + 2 * S_HPG * g for g in range(S_GROUPS))


def _cparams(n_axes):
    return pltpu.CompilerParams(dimension_semantics=("arbitrary",) * n_axes,
                                vmem_limit_bytes=VMEM_LIMIT)


def _dot(a, b):
    return jnp.dot(a, b, preferred_element_type=F32)


def _dot_nt(a, b):
    return lax.dot_general(a, b, (((1,), (1,)), ((), ())), preferred_element_type=F32)


def _split2(x):
    hi = x.astype(BF16)
    lo = (x - hi.astype(F32)).astype(BF16)
    return hi, lo


def _cumsum_rows(x, tri):
    hi, lo = _split2(x)
    return _dot(tri, hi) + _dot(tri, lo)


def _expand(x, e, exact=True):
    if not exact:
        return _dot(x.astype(BF16), e)
    hi, lo = _split2(x)
    return _dot(hi, e) + _dot(lo, e)


def _softplus(x):
    return jnp.maximum(x, 0.0) + jnp.log1p(jnp.exp(-jnp.abs(x)))


def _sigmoid(x):
    return 1.0 / (1.0 + jnp.exp(-x))


def _silu(x):
    return x * _sigmoid(x)


def _mod_kernel(c_ref, w_ref, b_ref, o_ref):
    a = _silu(c_ref[...]).astype(BF16)
    o_ref[...] = _dot(a, w_ref[...].astype(BF16)) + b_ref[...]


def _mod_call(c_rows, w_mod, b_mod):
    tn = 1024
    n = w_mod.shape[-1]
    return pl.pallas_call(
        _mod_kernel,
        grid=(DEPTH, n // tn),
        in_specs=[pl.BlockSpec((8, D_MODEL), lambda l, j: (0, 0)),
                  pl.BlockSpec((None, D_MODEL, tn), lambda l, j: (l, 0, j)),
                  pl.BlockSpec((None, 1, tn), lambda l, j: (l, 0, j))],
        out_specs=pl.BlockSpec((None, 8, tn), lambda l, j: (l, 0, j)),
        out_shape=jax.ShapeDtypeStruct((DEPTH, 8, n), F32),
        compiler_params=_cparams(2),
        name="mod_proj",
    )(c_rows, w_mod, b_mod.reshape(DEPTH, 1, n))


def _norm_mm_kernel(*refs, with_gates, relu2):
    if with_gates:
        x_ref, g_ref, sc_ref, sh_ref, w_ref, wg_ref, bg_ref, o_ref, og_ref, h_scr = refs
    else:
        x_ref, g_ref, sc_ref, sh_ref, w_ref, o_ref, h_scr = refs

    @pl.when(pl.program_id(1) == 0)
    def _():
        x = x_ref[...]
        ms = jnp.mean(x * x, axis=-1, keepdims=True)
        y = x * lax.rsqrt(ms + EPS) * g_ref[...]
        hb = (y * (1.0 + sc_ref[...]) + sh_ref[...]).astype(BF16)
        h_scr[...] = hb
        if with_gates:
            pre = _dot(hb, wg_ref[...]) + bg_ref[...]
            lane = lax.broadcasted_iota(jnp.int32, pre.shape, 1)
            act = jnp.where(lane < 4 * M_HEADS,
                            jnp.where(lane % 4 < 2, pre, -_softplus(-pre)),
                            _softplus(pre))
            for blk, off in enumerate(GATE_LANE_OFFSETS):
                own = 4 if blk < M_HEADS else 2 * S_HPG
                v = act if off == 0 else pltpu.roll(act, LANES - off, 1)
                og_ref[:, blk * LANES:(blk + 1) * LANES] = jnp.where(lane < own, v, 0.0)

    acc = _dot(h_scr[...], w_ref[...])
    if relu2:
        r = jnp.maximum(acc, 0.0)
        acc = r * r
    o_ref[...] = acc.astype(o_ref.dtype)


def _norm_mm_call(x, g, sc, sh, rows_per_mod, w, layer, tm, tn, out_dtype, relu2, gates=None, name="norm_mm"):
    t, d = x.shape
    n = w.shape[-1]
    assert t % tm == 0 and n % tn == 0 and rows_per_mod % tm == 0
    mod_idx = lambda i, j: ((i * tm) // rows_per_mod, 0, 0)
    in_specs = [pl.BlockSpec((tm, d), lambda i, j: (i, 0)),
                pl.BlockSpec((None, 1, d), lambda i, j: (layer, 0, 0)),
                pl.BlockSpec((None, 1, d), mod_idx),
                pl.BlockSpec((None, 1, d), mod_idx),
                pl.BlockSpec((None, d, tn), lambda i, j: (layer, 0, j))]
    args = [x, g, sc, sh, w]
    out_specs = [pl.BlockSpec((tm, tn), lambda i, j: (i, j))]
    out_shape = [jax.ShapeDtypeStruct((t, n), out_dtype)]
    if gates is not None:
        wg, bg = gates
        in_specs += [pl.BlockSpec((None, d, LANES), lambda i, j: (layer, 0, 0)),
                     pl.BlockSpec((None, 1, LANES), lambda i, j: (layer, 0, 0))]
        args += [wg, bg]
        out_specs.append(pl.BlockSpec((tm, N_GATES), lambda i, j: (i, 0)))
        out_shape.append(jax.ShapeDtypeStruct((t, N_GATES), F32))
    return pl.pallas_call(
        functools.partial(_norm_mm_kernel, with_gates=gates is not None, relu2=relu2),
        grid=(t // tm, n // tn),
        in_specs=in_specs, out_specs=out_specs, out_shape=out_shape,
        scratch_shapes=[pltpu.VMEM((tm, d), BF16)],
        compiler_params=_cparams(2),
        name=name,
    )(*args)


def _mm_resid_kernel(*refs, nj, tn, final_norm):
    if final_norm:
        a_ref, w_ref, r_ref, gate_ref, fg_ref, o_ref = refs
    else:
        a_ref, w_ref, r_ref, gate_ref, o_ref = refs
    j = pl.program_id(1)
    cols = pl.ds(pl.multiple_of(j * tn, tn), tn)
    o_ref[:, cols] = r_ref[...] + gate_ref[...] * _dot(a_ref[...], w_ref[...])
    if final_norm:
        @pl.when(j == nj - 1)
        def _():
            xn = o_ref[...]
            ms = jnp.mean(xn * xn, axis=-1, keepdims=True)
            o_ref[...] = xn * lax.rsqrt(ms + EPS) * fg_ref[...]


def _mm_resid_call(a, w, layer, resid, gate, rows_per_mod, tm, tn, final_g=None, name="mm_resid"):
    t, kdim = a.shape
    n = w.shape[-1]
    assert t % tm == 0 and n % tn == 0 and rows_per_mod % tm == 0
    nj = n // tn
    mod_idx = lambda i, j: ((i * tm) // rows_per_mod, 0, j)
    in_specs = [pl.BlockSpec((tm, kdim), lambda i, j: (i, 0)),
                pl.BlockSpec((None, kdim, tn), lambda i, j: (layer, 0, j)),
                pl.BlockSpec((tm, tn), lambda i, j: (i, j)),
                pl.BlockSpec((None, 1, tn), mod_idx)]
    args = [a, w, resid, gate]
    if final_g is not None:
        in_specs.append(pl.BlockSpec((1, n), lambda i, j: (0, 0)))
        args.append(final_g)
    return pl.pallas_call(
        functools.partial(_mm_resid_kernel, nj=nj, tn=tn, final_norm=final_g is not None),
        grid=(t // tm, nj),
        in_specs=in_specs,
        out_specs=pl.BlockSpec((tm, n), lambda i, j: (i, 0)),
        out_shape=jax.ShapeDtypeStruct((t, n), F32),
        compiler_params=_cparams(2),
        name=name,
    )(*args)


def _merge_kernel(ym_ref, yr_ref, ts_ref, sg_ref, wm_ref, wr_ref, ws_ref, g0_ref, g1_ref, g2_ref,
                  o_ref, ys_scr):
    @pl.when(pl.program_id(1) == 0)
    def _():
        t = ts_ref[...]
        ms = jnp.mean(t * t, axis=-1, keepdims=True)
        ys_scr[...] = (t * lax.rsqrt(ms + EPS) * sg_ref[...]).astype(BF16)

    acc = _sigmoid(g0_ref[...].astype(F32)) * _dot(ym_ref[...], wm_ref[...])
    acc += _sigmoid(g1_ref[...].astype(F32)) * _dot(yr_ref[...], wr_ref[...])
    acc += _sigmoid(g2_ref[...].astype(F32)) * _dot(ys_scr[...], ws_ref[...])
    o_ref[...] = acc.astype(o_ref.dtype)


def _merge_call(ym, yr, ts, sg, wm, wr, ws, layer, proj, tm, tn):
    t, w = ym.shape
    d = wm.shape[-1]
    assert t % tm == 0 and d % tn == 0
    per_br = d // tn
    yspec = pl.BlockSpec((tm, w), lambda i, j: (i, 0))
    wspec = pl.BlockSpec((None, w, tn), lambda i, j: (layer, 0, j))
    gspec = lambda br: pl.BlockSpec((tm, tn), lambda i, j: (i, br * per_br + j))
    return pl.pallas_call(
        _merge_kernel,
        grid=(t // tm, d // tn),
        in_specs=[yspec, yspec, yspec, pl.BlockSpec((1, w), lambda i, j: (0, 0)),
                  wspec, wspec, wspec, gspec(0), gspec(1), gspec(2)],
        out_specs=pl.BlockSpec((tm, tn), lambda i, j: (i, j)),
        out_shape=jax.ShapeDtypeStruct((t, d), BF16),
        scratch_shapes=[pltpu.VMEM((tm, w), BF16)],
        compiler_params=_cparams(2),
        name="merge",
    )(ym, yr, ts, sg, wm, wr, ws, proj, proj, proj)


def _chunk_masks():
    row = lax.broadcasted_iota(jnp.int32, (CHUNK, CHUNK), 0)
    col = lax.broadcasted_iota(jnp.int32, (CHUNK, CHUNK), 1)
    lower = col <= row
    upper = col >= row
    tri = jnp.where(lower, 1.0, 0.0).astype(BF16)
    return row, col, lower, upper, tri


def _mlstm_kernel(*refs, nc, ns, has_state, emit_state, alias_in):
    it = iter(refs)
    q_ref, k_ref, v_ref, o_ref, g_ref, ng_ref = (next(it) for _ in range(6))
    if has_state:
        c0_ref, n0_ref, m0_ref = (next(it) for _ in range(3))
    if alias_in:
        next(it)
    y_ref = next(it)
    if emit_state:
        cf_ref, nf_ref, mf_ref = (next(it) for _ in range(3))
    hacc = (next(it), next(it))
    c_scr = (next(it), next(it))
    n_scr = (next(it), next(it))
    m_scr = (next(it), next(it))
    p_scr = (next(it), next(it))
    ml_scr = (next(it), next(it))
    bc_scr = (next(it), next(it))
    wl_scr = (next(it), next(it))
    sc_scr = (next(it), next(it))

    _, _, lower, upper, tri = _chunk_masks()
    ones_bf = jnp.ones((CHUNK, LANES), BF16)
    dirs = (0, 1)
    masks = (lower, upper)
    sub8 = lax.broadcasted_iota(jnp.int32, (8, LANES), 0)

    def gates(i, carry, base):
        cidx = (2 * i, 2 * i + 1)
        two = (0, 1)
        rows = [pl.ds(pl.multiple_of(c * CHUNK, CHUNK), CHUNK) for c in cidx]
        grow = [pl.ds(pl.multiple_of(base + c * CHUNK, CHUNK), CHUNK) for c in cidx]
        g = [g_ref[r, :] for r in grow]
        cs = [_cumsum_rows(x, tri) for x in g]
        tot = [x[CHUNK - 1:CHUNK, :] for x in cs]
        g_t = [x.T for x in g]
        cs_t = [x.T for x in cs]
        tot_t = [x[:, CHUNK - 1:CHUNK] for x in cs_t]
        bsum = [[cs[j], tot[j] - cs[j] + g[j]] for j in two]
        bsum_t = [[cs_t[j], tot_t[j] - cs_t[j] + g_t[j]] for j in two]
        items = [(j, d) for j in two for d in dirs]
        b_col = {(j, d): bsum[j][d][:, 2 + d:3 + d] for (j, d) in items}
        i_col = {(j, d): g[j][:, d:d + 1] for (j, d) in items}
        t_row = {(j, d): g_t[j][d:d + 1, :] - bsum_t[j][d][2 + d:3 + d, :] for (j, d) in items}
        b_last = {(j, d): tot[j][:, 2 + d:3 + d] for (j, d) in items}
        logw = {(j, d): jnp.where(masks[d], b_col[j, d] + t_row[j, d], -jnp.inf) for (j, d) in items}
        m_loc = {it_: jnp.max(logw[it_], axis=1, keepdims=True) for it_ in items}
        m_chunk = {it_: jnp.max(b_last[it_] + t_row[it_], axis=1, keepdims=True) for it_ in items}
        for (j, d) in items:
            p_scr[d][rows[j], :] = jnp.exp(logw[j, d] - m_loc[j, d])
            ml_scr[d][rows[j], :] = jnp.broadcast_to(m_loc[j, d], (CHUNK, LANES))
            bc_scr[d][rows[j], :] = jnp.broadcast_to(b_col[j, d], (CHUNK, LANES))
            wl_scr[d][rows[j], :] = jnp.broadcast_to(b_last[j, d] - b_col[j, d] + i_col[j, d], (CHUNK, LANES))
            sc_scr[d][pl.ds(pl.multiple_of(cidx[j] * 8, 8), 8), :] = jnp.where(
                sub8 == 0, jnp.broadcast_to(m_chunk[j, d], (8, LANES)), jnp.broadcast_to(b_last[j, d], (8, LANES)))
        return carry

    def rep2(x):
        return jnp.concatenate([x, x], axis=1)

    def scan(i, carry, base):
        cidx = (i, nc - 1 - i)
        rows = [pl.ds(pl.multiple_of(c * CHUNK, CHUNK), CHUNK) for c in cidx]
        grow = [pl.ds(pl.multiple_of(base + c * CHUNK, CHUNK), CHUNK) for c in cidx]
        sc = [sc_scr[d][pl.ds(pl.multiple_of(cidx[d] * 8, 8), 8), :] for d in dirs]
        m_chunk = [sc[d][0:1, 0:1] for d in dirs]
        b_last = [sc[d][1:2, 0:1] for d in dirs]
        qb = [q_ref[r, :].astype(BF16) for r in grow]
        k = [k_ref[r, :].astype(F32) * (M_DK ** -0.5) for r in grow]
        kb = [x.astype(BF16) for x in k]
        vb = [v_ref[r, :].astype(BF16) for r in grow]
        qk = [_dot_nt(qb[d], kb[d]) for d in dirs]
        sb = [(qk[d] * p_scr[d][rows[d], :]).astype(BF16) for d in dirs]
        a_num = [_dot(sb[d], vb[d]) for d in dirs]
        a_den = [_dot(sb[d], ones_bf) for d in dirs]

        m_prev = [m_scr[d][0:1, 0:1] for d in dirs]
        c_prev = [c_scr[d][...] for d in dirs]
        n_prev = [n_scr[d][0:1, :] for d in dirs]
        m_new = [jnp.maximum(b_last[d] + m_prev[d], m_chunk[d]) for d in dirs]
        kw = [k[d] * rep2(jnp.exp(wl_scr[d][rows[d], :] - m_new[d])) for d in dirs]
        kw_t = [x.T.astype(BF16) for x in kw]
        q_c = [_dot(qb[d], c_prev[d].astype(BF16)) for d in dirs]
        u_c = [_dot(kw_t[d], vb[d]) for d in dirs]
        m_loc = [ml_scr[d][rows[d], :] for d in dirs]
        gg = [bc_scr[d][rows[d], :] + m_prev[d] for d in dirs]
        m_tot = [jnp.maximum(m_loc[d], gg[d]) for d in dirs]
        e_intra = [jnp.exp(m_loc[d] - m_tot[d]) for d in dirs]
        e_inter = [jnp.exp(gg[d] - m_tot[d]) for d in dirs]
        q_n = [_dot_nt(qb[d], jnp.broadcast_to(n_prev[d].astype(BF16), (LANES, M_DK))) for d in dirs]
        den = [e_intra[d] * a_den[d] + e_inter[d] * q_n[d] for d in dirs]
        inv = [1.0 / jnp.maximum(jnp.abs(den[d]), jnp.exp(-m_tot[d])) for d in dirs]
        for d in dirs:
            hacc[d][rows[d], :] = rep2(e_intra[d] * inv[d]) * a_num[d] + rep2(e_inter[d] * inv[d]) * q_c[d]

        u_n = [jnp.sum(x, axis=0, keepdims=True) for x in kw]
        a = [jnp.exp(b_last[d] + m_prev[d] - m_new[d]) for d in dirs]
        for d in dirs:
            c_scr[d][...] = a[d] * c_prev[d] + u_c[d]
            n_scr[d][0:1, :] = a[d] * n_prev[d] + u_n[d]
            m_scr[d][0:1, :] = jnp.broadcast_to(m_new[d], (1, LANES))
        return carry

    def finish(c, carry, base):
        rows = pl.ds(pl.multiple_of(c * CHUNK, CHUNK), CHUNK)
        grow = pl.ds(pl.multiple_of(base + c * CHUNK, CHUNK), CHUNK)
        ht = hacc[0][rows, :] + hacc[1][rows, :]
        ms = jnp.mean(ht * ht, axis=1, keepdims=True)
        y = ht * lax.rsqrt(ms + EPS) * ng_ref[...] * _sigmoid(o_ref[grow, :].astype(F32))
        y_ref[grow, :] = y.astype(y_ref.dtype)
        return carry

    for sq in range(ns):
        base = sq * nc * CHUNK
        for d in range(2):
            if has_state:
                c_scr[d][...] = c0_ref[sq, d]
                n_scr[d][0:1, :] = n0_ref[sq, d:d + 1, :]
                m_scr[d][0:1, :] = m0_ref[sq, d:d + 1, :]
            else:
                c_scr[d][...] = jnp.zeros_like(c_scr[d])
                n_scr[d][...] = jnp.zeros_like(n_scr[d])
                m_scr[d][...] = jnp.zeros_like(m_scr[d])
        lax.fori_loop(0, nc // 2, functools.partial(gates, base=base), 0)
        lax.fori_loop(0, nc, functools.partial(scan, base=base), 0, unroll=SCAN_UNROLL)
        lax.fori_loop(0, nc, functools.partial(finish, base=base), 0, unroll=SCAN_UNROLL)
        if emit_state:
            for d in range(2):
                if alias_in:
                    cf_ref[sq, d] = c_scr[d][...]
                else:
                    for slot in range(DEPTH):
                        cf_ref[sq, slot, d] = c_scr[d][...]
                nf_ref[sq, d:d + 1, :] = n_scr[d][0:1, :]
                mf_ref[sq, d:d + 1, :] = m_scr[d][0:1, :]


def _mlstm_call(proj, gates, ng, nb, seq, ns, state=None, emit=None):
    emit_state = emit is not None
    assert nb % ns == 0
    t = proj.shape[0]
    nc = seq // CHUNK
    dk = M_DK
    cspec = lambda base: pl.BlockSpec((ns * seq, dk), lambda b, h: (b, base // dk + h))
    in_specs = [cspec(OFF_MQ), cspec(OFF_MK), cspec(OFF_MV), cspec(OFF_MO),
                pl.BlockSpec((ns * seq, LANES), lambda b, h: (b, h)),
                pl.BlockSpec((1, dk), lambda b, h: (0, h))]
    args = [proj, proj, proj, proj, gates, ng]
    if state is not None:
        sl = state[0]
        in_specs += [pl.BlockSpec((ns, None, 2, None, dk, dk), lambda b, h: (b, sl, 0, h, 0, 0)),
                     pl.BlockSpec((ns, None, 2, dk), lambda b, h: (b, h, 0, 0)),
                     pl.BlockSpec((ns, None, 2, LANES), lambda b, h: (b, h, 0, 0))]
        args += list(state[1:])
    out_specs = [pl.BlockSpec((ns * seq, dk), lambda b, h: (b, h))]
    out_shape = [jax.ShapeDtypeStruct((t, M_HEADS * dk), BF16)]
    aliases = {}
    if emit_state:
        layer, prev = emit
        slot_blk, slot_idx = (None, layer) if prev is not None else (DEPTH, 0)
        out_specs += [pl.BlockSpec((ns, slot_blk, 2, None, dk, dk), lambda b, h: (b, slot_idx, 0, h, 0, 0)),
                      pl.BlockSpec((ns, None, 2, dk), lambda b, h: (b, h, 0, 0)),
                      pl.BlockSpec((ns, None, 2, LANES), lambda b, h: (b, h, 0, 0))]
        out_shape += [jax.ShapeDtypeStruct((nb, DEPTH, 2, M_HEADS, dk, dk), F32),
                      jax.ShapeDtypeStruct((nb, M_HEADS, 2, dk), F32),
                      jax.ShapeDtypeStruct((nb, M_HEADS, 2, LANES), F32)]
        if prev is not None:
            in_specs.append(pl.BlockSpec(memory_space=pl.ANY))
            args.append(prev)
            aliases = {len(args) - 1: 1}
    return pl.pallas_call(
        functools.partial(_mlstm_kernel, nc=nc, ns=ns, has_state=state is not None, emit_state=emit_state,
                          alias_in=bool(aliases)),
        grid=(nb // ns, M_HEADS),
        in_specs=in_specs, out_specs=out_specs, out_shape=out_shape, input_output_aliases=aliases,
        scratch_shapes=([pltpu.VMEM((seq, dk), F32)] * 2 + [pltpu.VMEM((dk, dk), F32)] * 2
                        + [pltpu.VMEM((8, dk), F32)] * 2 + [pltpu.VMEM((8, LANES), F32)] * 2
                        + [pltpu.VMEM((seq, LANES), F32)] * 8 + [pltpu.VMEM((nc * 8, LANES), F32)] * 2),
        compiler_params=_cparams(2),
        name="mlstm",
    )(*args)


R_HPS = 2


def _ret_kernel(*refs, nc, ns, has_state, emit_state, rope, alias_in):
    it = iter(refs)
    q_ref, k_ref, v_ref, g_ref, lg_ref, ng_ref = (next(it) for _ in range(6))
    if rope:
        cos_ref, sin_ref = next(it), next(it)
    if has_state:
        s0_ref = next(it)
    if alias_in:
        next(it)
    y_ref = next(it)
    if emit_state:
        sf_ref = next(it)
    chains = [(h, d) for h in range(R_HPS) for d in range(2)]
    oacc = {c: next(it) for c in chains}
    s_scr = {c: next(it) for c in chains}
    qs_scr, ks_scr, kt_scr = (next(it) for _ in range(3))

    row, col, lower, upper, _ = _chunk_masks()
    rel = (row - col).astype(F32)
    pos = row.astype(F32)
    decay, w_q, w_k, chunk_decay = {}, {}, {}, {}
    for (h, d) in chains:
        lgd = -jnp.exp(lg_ref[h, d:d + 1, :])
        if d == 0:
            decay[h, d] = jnp.where(lower, jnp.exp(rel * lgd), 0.0)
            w_q[h, d] = jnp.exp((pos + 1.0) * lgd)
            w_k[h, d] = jnp.exp((CHUNK - 1.0 - pos) * lgd)
        else:
            decay[h, d] = jnp.where(upper, jnp.exp(-rel * lgd), 0.0)
            w_q[h, d] = jnp.exp((CHUNK - pos) * lgd)
            w_k[h, d] = jnp.exp(pos * lgd)
        chunk_decay[h, d] = jnp.exp(CHUNK * lgd)

    def hcols(h):
        return slice(h * R_DK, (h + 1) * R_DK)

    def prep(c, carry, base):
        rows = pl.ds(pl.multiple_of(c * CHUNK, CHUNK), CHUNK)
        grow = pl.ds(pl.multiple_of(base + c * CHUNK, CHUNK), CHUNK)
        for h in range(R_HPS):
            q = q_ref[grow, hcols(h)].astype(F32)
            k = k_ref[grow, hcols(h)].astype(F32) * (R_DK ** -0.5)
            if rope:
                cs, sn = cos_ref[grow, :], sin_ref[grow, :]
                q = q * cs + pltpu.roll(q, R_DK // 2, 1) * sn
                k = k * cs + pltpu.roll(k, R_DK // 2, 1) * sn
            qs_scr[rows, hcols(h)] = q.astype(BF16)
            ks_scr[rows, hcols(h)] = k.astype(BF16)
            kt_scr[hcols(h), rows] = k.T.astype(BF16)
        return carry

    def scan(i, carry, base):
        rows = {0: pl.ds(pl.multiple_of(i * CHUNK, CHUNK), CHUNK),
                1: pl.ds(pl.multiple_of((nc - 1 - i) * CHUNK, CHUNK), CHUNK)}
        grow = {0: pl.ds(pl.multiple_of(base + i * CHUNK, CHUNK), CHUNK),
                1: pl.ds(pl.multiple_of(base + (nc - 1 - i) * CHUNK, CHUNK), CHUNK)}
        qb = {(h, d): qs_scr[rows[d], hcols(h)] for (h, d) in chains}
        kb = {(h, d): ks_scr[rows[d], hcols(h)] for (h, d) in chains}
        kt = {(h, d): kt_scr[hcols(h), rows[d]] for (h, d) in chains}
        vb = {(h, d): v_ref[grow[d], hcols(h)].astype(BF16) for (h, d) in chains}
        vw = {c: (vb[c].astype(F32) * w_k[c]).astype(BF16) for c in chains}
        s_prev = {c: s_scr[c][...] for c in chains}
        sc = {c: (_dot_nt(qb[c], kb[c]) * decay[c]).astype(BF16) for c in chains}
        inter = {c: _dot(qb[c], s_prev[c].astype(BF16)) for c in chains}
        u = {c: _dot(kt[c], vw[c]) for c in chains}
        intra = {c: _dot(sc[c], vb[c]) for c in chains}
        for (h, d) in chains:
            oacc[h, d][rows[d], :] = intra[h, d] + w_q[h, d] * inter[h, d]
            s_scr[h, d][...] = chunk_decay[h, d] * s_prev[h, d] + u[h, d]
        return carry

    def finish(c, carry, base):
        rows = pl.ds(pl.multiple_of(c * CHUNK, CHUNK), CHUNK)
        grow = pl.ds(pl.multiple_of(base + c * CHUNK, CHUNK), CHUNK)
        for h in range(R_HPS):
            ot = oacc[h, 0][rows, :] + oacc[h, 1][rows, :]
            ms = jnp.mean(ot * ot, axis=1, keepdims=True)
            y = ot * lax.rsqrt(ms + EPS) * ng_ref[:, hcols(h)] * _silu(g_ref[grow, hcols(h)].astype(F32))
            y_ref[grow, hcols(h)] = y.astype(y_ref.dtype)
        return carry

    for sq in range(ns):
        base = sq * nc * CHUNK
        for (h, d) in chains:
            if has_state:
                s_scr[h, d][...] = s0_ref[sq, d, h]
            else:
                s_scr[h, d][...] = jnp.zeros_like(s_scr[h, d])
        lax.fori_loop(0, nc, functools.partial(prep, base=base), 0, unroll=SCAN_UNROLL)
        lax.fori_loop(0, nc, functools.partial(scan, base=base), 0, unroll=SCAN_UNROLL)
        lax.fori_loop(0, nc, functools.partial(finish, base=base), 0, unroll=SCAN_UNROLL)
        if emit_state:
            for (h, d) in chains:
                if alias_in:
                    sf_ref[sq, d, h] = s_scr[h, d][...]
                else:
                    for slot in range(DEPTH):
                        sf_ref[sq, slot, d, h] = s_scr[h, d][...]


def _ret_call(proj, lgr, ng, nb, seq, ns, rope=None, state=None, emit=None):
    emit_state = emit is not None
    assert nb % ns == 0 and (rope is None or ns == 1)
    t = proj.shape[0]
    nc = seq // CHUNK
    dk = R_DK
    bw = R_HPS * dk
    cspec = lambda base: pl.BlockSpec((ns * seq, bw), lambda b, h: (b, base // bw + h))
    in_specs = [cspec(OFF_RQ), cspec(OFF_RK), cspec(OFF_RV), cspec(OFF_RG),
                pl.BlockSpec((R_HPS, 2, LANES), lambda b, h: (h, 0, 0)),
                pl.BlockSpec((1, bw), lambda b, h: (0, h))]
    args = [proj, proj, proj, proj, lgr, ng]
    if rope is not None:
        in_specs += [pl.BlockSpec((seq, dk), lambda b, h: (0, 0))] * 2
        args += list(rope)
    if state is not None:
        sl = state[0]
        in_specs.append(pl.BlockSpec((ns, None, 2, R_HPS, dk, dk), lambda b, h: (b, sl, 0, h, 0, 0)))
        args.append(state[1])
    out_specs = [pl.BlockSpec((ns * seq, bw), lambda b, h: (b, h))]
    out_shape = [jax.ShapeDtypeStruct((t, R_HEADS * dk), BF16)]
    aliases = {}
    if emit_state:
        layer, prev = emit
        slot_blk, slot_idx = (None, layer) if prev is not None else (DEPTH, 0)
        out_specs.append(pl.BlockSpec((ns, slot_blk, 2, R_HPS, dk, dk), lambda b, h: (b, slot_idx, 0, h, 0, 0)))
        out_shape.append(jax.ShapeDtypeStruct((nb, DEPTH, 2, R_HEADS, dk, dk), F32))
        if prev is not None:
            in_specs.append(pl.BlockSpec(memory_space=pl.ANY))
            args.append(prev)
            aliases = {len(args) - 1: 1}
    n_chain = 2 * R_HPS
    return pl.pallas_call(
        functools.partial(_ret_kernel, nc=nc, ns=ns, has_state=state is not None, emit_state=emit_state,
                          rope=rope is not None, alias_in=bool(aliases)),
        grid=(nb // ns, R_HEADS // R_HPS),
        in_specs=in_specs, out_specs=out_specs, out_shape=out_shape, input_output_aliases=aliases,
        scratch_shapes=([pltpu.VMEM((seq, dk), F32)] * n_chain + [pltpu.VMEM((dk, dk), F32)] * n_chain
                        + [pltpu.VMEM((seq, bw), BF16)] * 2 + [pltpu.VMEM((bw, seq), BF16)]),
        compiler_params=_cparams(2),
        name="retention",
    )(*args)


def _ssd_kernel(*refs, nc, ns, has_state, emit_state, alias_in):
    it = iter(refs)
    (x_ref, b_ref, c_ref, z_ref, g_ref, wx_ref, wb_ref, wc_ref, bx_ref, bb_ref, bc_ref,
     al_ref, sd_ref) = (next(it) for _ in range(13))
    if has_state:
        h0_ref = next(it)
    if alias_in:
        next(it)
    y_ref = next(it)
    if emit_state:
        hf_ref = next(it)
    xs_scr, bt_scr, bb_scr, cc_scr, ht_scr, yb_scr = (next(it) for _ in range(6))

    _, _, lower, upper, tri = _chunk_masks()
    er = lax.broadcasted_iota(jnp.int32, (LANES, S_GW), 0)
    ec = lax.broadcasted_iota(jnp.int32, (LANES, S_GW), 1) // S_P
    expand = [jnp.where(er == ec + d * S_HPG, 1.0, 0.0).astype(BF16) for d in range(2)]
    a_row = -jnp.exp(al_ref[...])

    wr = lax.broadcasted_iota(jnp.int32, (CHUNK, CHUNK + 32), 0)
    wc = lax.broadcasted_iota(jnp.int32, (CHUNK, CHUNK + 32), 1)
    shifts = [jnp.where(wc == wr + 16 + sft, 1.0, 0.0).astype(BF16) for sft in (-1, 1, 2)]

    def conv_silu(ref, w_ref, bias_ref, c, base):
        off = base + c * CHUNK
        cur = ref[pl.ds(pl.multiple_of(off, CHUNK), CHUNK), :]
        prev_w = ref[pl.ds(pl.multiple_of(jnp.maximum(off - 16, 0), 16), 16), :]
        next_w = ref[pl.ds(pl.multiple_of(jnp.minimum(off + CHUNK, ns * nc * CHUNK - 16), 16), 16), :]
        prev_w = jnp.where(c > 0, prev_w, jnp.zeros_like(prev_w))
        next_w = jnp.where(c < nc - 1, next_w, jnp.zeros_like(next_w))
        win = jnp.concatenate([prev_w, cur, next_w], axis=0)
        xm1, xp1, xp2 = (_dot(sm, win) for sm in shifts)
        w = w_ref[...]
        y = w[0:1, :] * xm1 + w[1:2, :] * cur.astype(F32) + w[2:3, :] * xp1 + w[3:4, :] * xp2 + bias_ref[...]
        return _silu(y)

    def prep(c, carry, base):
        off = pl.multiple_of(c * CHUNK, CHUNK)
        rows = pl.ds(off, CHUNK)
        xs_scr[rows, :] = conv_silu(x_ref, wx_ref, bx_ref, c, base)
        bm = conv_silu(b_ref, wb_ref, bb_ref, c, base)
        bb_scr[rows, :] = bm.astype(BF16)
        bt_scr[:, rows] = bm.T.astype(BF16)
        cc_scr[rows, :] = conv_silu(c_ref, wc_ref, bc_ref, c, base).astype(BF16)
        return carry

    dirs = (0, 1)
    masks = (lower, upper)
    first_half = lax.broadcasted_iota(jnp.int32, (CHUNK, LANES), 1) < S_P

    def scan(i, carry, base):
        rows = [pl.ds(pl.multiple_of(c * CHUNK, CHUNK), CHUNK) for c in (i, nc - 1 - i)]
        grow = [pl.ds(pl.multiple_of(base + c * CHUNK, CHUNK), CHUNK) for c in (i, nc - 1 - i)]
        dt = [g_ref[r, :] for r in grow]
        a = [x * a_row for x in dt]
        cs = [_cumsum_rows(x, tri) for x in a]
        tot = [x[CHUNK - 1:CHUNK, :] for x in cs]
        bsum = [cs[0], tot[1] - cs[1] + a[1]]
        bsum_t = [x.T for x in bsum]
        dt_t = [x.T for x in dt]
        xs = [xs_scr[r, :] for r in rows]
        xb = [x.astype(BF16) for x in xs]
        bmb = [bb_scr[r, :] for r in rows]
        cmb = [cc_scr[r, :] for r in rows]
        btb = [bt_scr[:, r] for r in rows]
        cb = [_dot_nt(cmb[d], bmb[d]) for d in dirs]
        ht = [ht_scr[d] for d in dirs]
        inter = [_dot(cmb[d], ht[d].astype(BF16)) for d in dirs]
        ex = [_expand(jnp.exp(bsum[d]), expand[d], exact=False) for d in dirs]
        wexp = [_expand(jnp.exp(tot[d] - bsum[d]) * dt[d], expand[d], exact=False) for d in dirs]
        cd = [_expand(jnp.broadcast_to(jnp.exp(tot[d]), (8, LANES)), expand[d])[0:1, :] for d in dirs]
        u_t = [_dot(btb[d], (xs[d] * wexp[d]).astype(BF16)) for d in dirs]
        ys = [[], []]
        for p in range(S_HPG // 2):
            for d in dirs:
                ms = []
                for k in (2 * p, 2 * p + 1):
                    ln = d * S_HPG + k
                    decay = jnp.exp(jnp.where(masks[d], bsum[d][:, ln:ln + 1] - bsum_t[d][ln:ln + 1, :], -jnp.inf))
                    ms.append((cb[d] * decay * dt_t[d][ln:ln + 1, :]).astype(BF16))
                xp = xb[d][:, p * LANES:(p + 1) * LANES]
                zero = jnp.zeros_like(xp)
                rhs = jnp.concatenate([jnp.where(first_half, xp, zero), jnp.where(first_half, zero, xp)], axis=0)
                ys[d].append(_dot(jnp.concatenate(ms, axis=1), rhs))
        y_ref[grow[0], :] = jnp.concatenate(ys[0], axis=1) + ex[0] * inter[0]
        yb_scr[rows[1], :] = jnp.concatenate(ys[1], axis=1) + ex[1] * inter[1]
        for d in dirs:
            ht_scr[d] = ht[d] * cd[d] + u_t[d]
        return carry

    def finish(c, carry, base):
        rows = pl.ds(pl.multiple_of(c * CHUNK, CHUNK), CHUNK)
        grow = pl.ds(pl.multiple_of(base + c * CHUNK, CHUNK), CHUNK)
        yt = y_ref[grow, :] + yb_scr[rows, :] + sd_ref[...] * xs_scr[rows, :]
        y_ref[grow, :] = yt * _silu(z_ref[grow, :].astype(F32))
        return carry

    for sq in range(ns):
        base = sq * nc * CHUNK
        lax.fori_loop(0, nc, functools.partial(prep, base=base), 0, unroll=SCAN_UNROLL)
        for d in range(2):
            if has_state:
                ht_scr[d] = h0_ref[sq, d].T
            else:
                ht_scr[d] = jnp.zeros((S_N, S_GW), F32)
        lax.fori_loop(0, nc, functools.partial(scan, base=base), 0, unroll=SCAN_UNROLL)
        lax.fori_loop(0, nc, functools.partial(finish, base=base), 0, unroll=SCAN_UNROLL)
        if emit_state:
            for d in range(2):
                h_fin = ht_scr[d].T
                if alias_in:
                    hf_ref[sq, d] = h_fin
                else:
                    for slot in range(DEPTH):
                        hf_ref[sq, slot, d] = h_fin


def _ssd_call(proj, gates, conv_w, conv_b, alog, sd, nb, seq, ns, state=None, emit=None):
    emit_state = emit is not None
    assert nb % ns == 0
    t = proj.shape[0]
    nc = seq // CHUNK
    gw = S_GW
    n = S_N
    xw = S_HEADS * S_P
    in_specs = [pl.BlockSpec((ns * seq, gw), lambda b, g: (b, OFF_SX // gw + g)),
                pl.BlockSpec((ns * seq, n), lambda b, g: (b, OFF_SB // n + g)),
                pl.BlockSpec((ns * seq, n), lambda b, g: (b, OFF_SC // n + g)),
                pl.BlockSpec((ns * seq, gw), lambda b, g: (b, OFF_SZ // gw + g)),
                pl.BlockSpec((ns * seq, LANES), lambda b, g: (b, M_HEADS + g)),
                pl.BlockSpec((4, gw), lambda b, g: (0, g)),
                pl.BlockSpec((4, n), lambda b, g: (0, xw // n + g)),
                pl.BlockSpec((4, n), lambda b, g: (0, xw // n + S_GROUPS + g)),
                pl.BlockSpec((1, gw), lambda b, g: (0, g)),
                pl.BlockSpec((1, n), lambda b, g: (0, xw // n + g)),
                pl.BlockSpec((1, n), lambda b, g: (0, xw // n + S_GROUPS + g)),
                pl.BlockSpec((None, 1, LANES), lambda b, g: (g, 0, 0)),
                pl.BlockSpec((1, gw), lambda b, g: (0, g))]
    args = [proj, proj, proj, proj, gates, conv_w, conv_w, conv_w, conv_b, conv_b, conv_b, alog, sd]
    if state is not None:
        sl = state[0]
        in_specs.append(pl.BlockSpec((ns, None, 2, None, gw, n), lambda b, g: (b, sl, 0, g, 0, 0)))
        args.append(state[1])
    out_specs = [pl.BlockSpec((ns * seq, gw), lambda b, g: (b, g))]
    out_shape = [jax.ShapeDtypeStruct((t, xw), F32)]
    aliases = {}
    if emit_state:
        layer, prev = emit
        slot_blk, slot_idx = (None, layer) if prev is not None else (DEPTH, 0)
        out_specs.append(pl.BlockSpec((ns, slot_blk, 2, None, gw, n), lambda b, g: (b, slot_idx, 0, g, 0, 0)))
        out_shape.append(jax.ShapeDtypeStruct((nb, DEPTH, 2, S_GROUPS, gw, n), F32))
        if prev is not None:
            in_specs.append(pl.BlockSpec(memory_space=pl.ANY))
            args.append(prev)
            aliases = {len(args) - 1: 1}
    return pl.pallas_call(
        functools.partial(_ssd_kernel, nc=nc, ns=ns, has_state=state is not None, emit_state=emit_state,
                          alias_in=bool(aliases)),
        grid=(nb // ns, S_GROUPS),
        in_specs=in_specs, out_specs=out_specs, out_shape=out_shape, input_output_aliases=aliases,
        scratch_shapes=[pltpu.VMEM((seq, gw), F32), pltpu.VMEM((n, seq), BF16),
                        pltpu.VMEM((seq, n), BF16), pltpu.VMEM((seq, n), BF16),
                        pltpu.VMEM((2, n, gw), F32), pltpu.VMEM((seq, gw), F32)],
        compiler_params=_cparams(2),
        name="ssd",
    )(*args)


_IN_SIZES = (1024, 1024, 1024, 1024, 8, 8, 1024, 1024, 1024, 1024, 1024, 1536, 32, 6144)
_IN_OFFS = np.concatenate([[0], np.cumsum(_IN_SIZES)])


def _gate_lane_index():
    idx = -np.ones((LANES,), np.int64)
    for h in range(M_HEADS):
        base = GATE_LANE_OFFSETS[h]
        idx[base + 0] = 0 * M_HEADS + h
        idx[base + 1] = 1 * M_HEADS + h
        idx[base + 2] = 2 * M_HEADS + 0 * M_HEADS + h
        idx[base + 3] = 2 * M_HEADS + 1 * M_HEADS + h
    for g in range(S_GROUPS):
        base = GATE_LANE_OFFSETS[M_HEADS + g]
        for d in range(2):
            for k in range(S_HPG):
                idx[base + d * S_HPG + k] = 4 * M_HEADS + d * S_HEADS + g * S_HPG + k
    return idx


def _place(vals, idx):
    taken = jnp.take(vals, jnp.asarray(np.maximum(idx, 0)), axis=-1)
    return jnp.where(jnp.asarray(idx >= 0), taken, 0.0)


def _prep_params(w_in, m_igate_b, m_fgate_b, s_dt_bias, s_a_log, r_decay, s_d):
    o = _IN_OFFS
    w_big = jnp.concatenate([w_in[:, :, o[13]:o[14]], w_in[:, :, o[0]:o[4]], w_in[:, :, o[6]:o[12]]],
                            axis=2).astype(BF16)
    w_small = jnp.concatenate([w_in[:, :, o[4]:o[6]], w_in[:, :, o[12]:o[13]]], axis=2)
    gidx = _gate_lane_index()
    w_g = _place(w_small, gidx).astype(BF16)
    b_small = jnp.concatenate([m_igate_b.reshape(DEPTH, -1), m_fgate_b.reshape(DEPTH, -1),
                               s_dt_bias.reshape(DEPTH, -1)], axis=1)
    b_g = _place(b_small, gidx).reshape(DEPTH, 1, LANES)
    aidx = -np.ones((S_GROUPS, LANES), np.int64)
    for g in range(S_GROUPS):
        for d in range(2):
            for k in range(S_HPG):
                aidx[g, d * S_HPG + k] = d * S_HEADS + g * S_HPG + k
    a_flat = s_a_log.reshape(DEPTH, -1)
    alog = jnp.stack([_place(a_flat, aidx[g]) for g in range(S_GROUPS)], axis=1).reshape(DEPTH, S_GROUPS, 1, LANES)
    lgr = jnp.broadcast_to(jnp.swapaxes(r_decay, 1, 2)[..., None], (DEPTH, R_HEADS, 2, LANES))
    sd = jnp.repeat(s_d, S_P, axis=1).reshape(DEPTH, 1, S_HEADS * S_P)
    return w_big, w_g, b_g, alog, lgr, sd


def _rope_tables(seq):
    n_rows = seq // GRID_W
    rows = jnp.repeat(jnp.arange(n_rows, dtype=F32), GRID_W)
    cols = jnp.tile(jnp.arange(GRID_W, dtype=F32), n_rows)
    inv = ROPE_BASE ** (-jnp.arange(ROPE_FREQS, dtype=F32) / ROPE_FREQS)
    ang = jnp.concatenate([rows[:, None] * inv, cols[:, None] * inv], -1)
    cos, sin = jnp.cos(ang), jnp.sin(ang)
    return jnp.concatenate([cos, cos], -1), jnp.concatenate([-sin, sin], -1)


def _layer(x, l, mods, rows_per_mod, nb, seq, pw, rope, state, emit, final_g):
    sh_a, sc_a, g_a, sh_f, sc_f, g_f = mods
    tm_l, tm_s = min(TM_LARGE, rows_per_mod), min(TM_SMALL, rows_per_mod)
    proj, gates = _norm_mm_call(x, pw["norm_mix_g"], sc_a, sh_a, rows_per_mod, pw["w_big"], l, tm_l, TN_IN_PROJ,
                                BF16, False, gates=(pw["w_g"], pw["b_g"]), name="in_proj")
    st_m = st_r = st_s = None
    if state is not None:
        st_m, st_r, st_s = state
    em_m = em_r = em_s = None
    if emit is not None:
        em_m, em_r, em_s = ((l, prev) for prev in emit)
    ns = 1 if rope is not None else max(n for n in range(1, max(1, SEQ_ROWS_PER_STEP // seq) + 1) if nb % n == 0)
    om = _mlstm_call(proj, gates, pw["m_norm_g"][l], nb, seq, 1, state=st_m, emit=em_m)
    orr = _ret_call(proj, pw["lgr"][l], pw["r_norm_g"][l], nb, seq, ns, rope=rope, state=st_r, emit=em_r)
    os_ = _ssd_call(proj, gates, pw["s_conv_w"][l], pw["s_conv_b"][l], pw["alog"][l], pw["sd"][l], nb, seq, ns,
                    state=st_s, emit=em_s)
    merged = _merge_call(om[0], orr[0], os_[0], pw["s_norm_g"][l], pw["w_br_m"], pw["w_br_r"], pw["w_br_s"], l,
                         proj, tm_s, TN_MERGE)
    x = _mm_resid_call(merged, pw["w_out"], l, x, g_a, rows_per_mod, tm_s, D_MODEL, name="out_proj")
    (hid,) = _norm_mm_call(x, pw["norm_mlp_g"], sc_f, sh_f, rows_per_mod, pw["w_ff1"], l, tm_l, TN_MLP_UP,
                           BF16, True, name="mlp_up")
    x = _mm_resid_call(hid, pw["w_ff2"], l, x, g_f, rows_per_mod, tm_s, TN_MLP_DOWN, final_g=final_g, name="mlp_down")
    new_state = (om[1:], orr[1:], os_[1:]) if emit is not None else None
    return x, new_state


def kernel(x_prompt, x_sample, c, state_mlstm_C, state_mlstm_n, state_mlstm_m, state_ret, state_ssd, c_ctx, w_mod, b_mod, norm_mix_g, norm_mlp_g, w_in, m_igate_b, m_fgate_b, m_norm_g, r_decay, r_norm_g, s_conv_w, s_conv_b, s_dt_bias, s_a_log, s_d, s_norm_g, w_br_m, w_br_r, w_br_s, w_out, w_ff1, w_ff2, final_norm_g):
    bp, lp, d = x_prompt.shape
    bs, ls, _ = x_sample.shape
    xp = x_prompt.reshape(bp * lp, d)
    xs = x_sample.reshape(bs * ls, d)

    c_rows = jnp.zeros((8, d), F32).at[:bs].set(c).at[bs].set(c_ctx)
    mod = _mod_call(c_rows, w_mod, b_mod)
    rope = _rope_tables(ls)
    final_g = final_norm_g.reshape(1, d)

    w_big, w_g, b_g, alog, lgr, sd = _prep_params(w_in, m_igate_b, m_fgate_b, s_dt_bias, s_a_log, r_decay, s_d)
    pw = dict(w_big=w_big, w_g=w_g, b_g=b_g, alog=alog, lgr=lgr, sd=sd,
              norm_mix_g=norm_mix_g.reshape(DEPTH, 1, d), norm_mlp_g=norm_mlp_g.reshape(DEPTH, 1, d),
              m_norm_g=m_norm_g.reshape(DEPTH, 1, -1), r_norm_g=r_norm_g.reshape(DEPTH, 1, -1),
              s_norm_g=s_norm_g.reshape(DEPTH, 1, -1),
              s_conv_w=s_conv_w, s_conv_b=s_conv_b.reshape(DEPTH, 1, -1),
              w_br_m=w_br_m.astype(BF16), w_br_r=w_br_r.astype(BF16), w_br_s=w_br_s.astype(BF16),
              w_out=w_out.astype(BF16), w_ff1=w_ff1.astype(BF16), w_ff2=w_ff2.astype(BF16))
    cache_c = state_mlstm_C
    cache_r = state_ret
    cache_s = state_ssd.reshape(bs, DEPTH, 2, S_GROUPS, S_GW, S_N)

    big = (None, None, None)
    st_n, st_m = [], []
    for l in range(DEPTH):
        parts = mod[l].reshape(8, 6, 1, d)
        mods_ctx = tuple(parts[bs:bs + 1, i] for i in range(6))
        mods_lat = tuple(parts[:bs, i] for i in range(6))
        fg = final_g if l == DEPTH - 1 else None

        xp, st = _layer(xp, l, mods_ctx, bp * lp, bp, lp, pw, None, None, big, fg)
        (cf, nf, mf), (rf,), (hf,) = st
        big = (cf, rf, hf)
        st_n.append(jnp.transpose(nf, (0, 2, 1, 3)))
        st_m.append(jnp.transpose(mf[..., 0], (0, 2, 1)))

        cache = (
            (l, cache_c,
             jnp.transpose(state_mlstm_n[:, l], (0, 2, 1, 3)),
             jnp.broadcast_to(jnp.transpose(state_mlstm_m[:, l], (0, 2, 1))[..., None], (bs, M_HEADS, 2, LANES))),
            (l, cache_r),
            (l, cache_s),
        )
        xs, _ = _layer(xs, l, mods_lat, ls, bs, ls, pw, rope, cache, None, fg)

    return (xp.reshape(bp, lp, d), xs.reshape(bs, ls, d),
            big[0], jnp.stack(st_n, 1), jnp.stack(st_m, 1), big[1],
            big[2].reshape(bp, DEPTH, 2, S_HEADS, S_P, S_N))
```

```python
import functools

import numpy as np
import jax
import jax.numpy as jnp
from jax import lax
from jax.experimental import pallas as pl
from jax.experimental.pallas import tpu as pltpu

F32 = jnp.float32
BF16 = jnp.bfloat16

D_MODEL = 2048
DEPTH = 2
CHUNK = 128
EPS = 1e-6
M_HEADS, M_DK = 4, 256
R_HEADS, R_DK = 8, 128
S_HEADS, S_P, S_GROUPS, S_N = 16, 64, 2, 128
S_HPG = S_HEADS // S_GROUPS
S_GW = S_HPG * S_P
GRID_W = 64
ROPE_BASE = 10000.0
ROPE_FREQS = R_DK // 4
D_FF = 4 * D_MODEL
LANES = 128
VMEM_LIMIT = 56 * 1024 * 1024
SCAN_UNROLL = 8
SEQ_ROWS_PER_STEP = 1024
TM_LARGE, TM_SMALL = 1024, 512
TN_IN_PROJ, TN_MLP_UP, TN_MLP_DOWN, TN_MERGE = 1536, 2048, 512, 2048

N_GL = 3 * D_MODEL
OFF_MQ, OFF_MK, OFF_MV, OFF_MO = (N_GL + o for o in (0, 1024, 2048, 3072))
OFF_RQ, OFF_RK, OFF_RV, OFF_RG = (N_GL + o for o in (4096, 5120, 6144, 7168))
OFF_SZ, OFF_SX, OFF_SB, OFF_SC = (N_GL + o for o in (8192, 9216, 10240, 10496))
N_GATES = (M_HEADS + S_GROUPS) * LANES
GATE_LANE_OFFSETS = tuple(4 * h for h in range(M_HEADS)) + tuple(4 * M_HEADS + 2 * S_HPG * g for g in range(S_GROUPS))


def _cparams(n_axes):
    return pltpu.CompilerParams(dimension_semantics=("arbitrary",) * n_axes,
                                vmem_limit_bytes=VMEM_LIMIT)


def _dot(a, b):
    return jnp.dot(a, b, preferred_element_type=F32)


def _dot_nt(a, b):
    return lax.dot_general(a, b, (((1,), (1,)), ((), ())), preferred_element_type=F32)


def _split2(x):
    hi = x.astype(BF16)
    lo = (x - hi.astype(F32)).astype(BF16)
    return hi, lo


def _cumsum_rows(x, tri):
    hi, lo = _split2(x)
    return _dot(tri, hi) + _dot(tri, lo)


def _expand(x, e, exact=True):
    if not exact:
        return _dot(x.astype(BF16), e)
    hi, lo = _split2(x)
    return _dot(hi, e) + _dot(lo, e)


def _softplus(x):
    return jnp.maximum(x, 0.0) + jnp.log1p(jnp.exp(-jnp.abs(x)))


def _sigmoid(x):
    return 1.0 / (1.0 + jnp.exp(-x))


def _silu(x):
    return x * _sigmoid(x)


def _mod_kernel(c_ref, w_ref, b_ref, o_ref):
    a = _silu(c_ref[...]).astype(BF16)
    o_ref[...] = _dot(a, w_ref[...].astype(BF16)) + b_ref[...]


def _mod_call(c_rows, w_mod, b_mod):
    tn = 1024
    n = w_mod.shape[-1]
    return pl.pallas_call(
        _mod_kernel,
        grid=(DEPTH, n // tn),
        in_specs=[pl.BlockSpec((8, D_MODEL), lambda l, j: (0, 0)),
                  pl.BlockSpec((None, D_MODEL, tn), lambda l, j: (l, 0, j)),
                  pl.BlockSpec((None, 1, tn), lambda l, j: (l, 0, j))],
        out_specs=pl.BlockSpec((None, 8, tn), lambda l, j: (l, 0, j)),
        out_shape=jax.ShapeDtypeStruct((DEPTH, 8, n), F32),
        compiler_params=_cparams(2),
        name="mod_proj",
    )(c_rows, w_mod, b_mod.reshape(DEPTH, 1, n))


def _norm_mm_kernel(*refs, with_gates, relu2):
    if with_gates:
        x_ref, g_ref, sc_ref, sh_ref, w_ref, wg_ref, bg_ref, o_ref, og_ref, h_scr = refs
    else:
        x_ref, g_ref, sc_ref, sh_ref, w_ref, o_ref, h_scr = refs

    @pl.when(pl.program_id(1) == 0)
    def _():
        x = x_ref[...]
        ms = jnp.mean(x * x, axis=-1, keepdims=True)
        y = x * lax.rsqrt(ms + EPS) * g_ref[...]
        hb = (y * (1.0 + sc_ref[...]) + sh_ref[...]).astype(BF16)
        h_scr[...] = hb
        if with_gates:
            pre = _dot(hb, wg_ref[...]) + bg_ref[...]
            lane = lax.broadcasted_iota(jnp.int32, pre.shape, 1)
            act = jnp.where(lane < 4 * M_HEADS,
                            jnp.where(lane % 4 < 2, pre, -_softplus(-pre)),
                            _softplus(pre))
            for blk, off in enumerate(GATE_LANE_OFFSETS):
                own = 4 if blk < M_HEADS else 2 * S_HPG
                v = act if off == 0 else pltpu.roll(act, LANES - off, 1)
                og_ref[:, blk * LANES:(blk + 1) * LANES] = jnp.where(lane < own, v, 0.0)

    acc = _dot(h_scr[...], w_ref[...])
    if relu2:
        r = jnp.maximum(acc, 0.0)
        acc = r * r
    o_ref[...] = acc.astype(o_ref.dtype)


def _norm_mm_call(x, g, sc, sh, rows_per_mod, w, layer, tm, tn, out_dtype, relu2, gates=None, name="norm_mm"):
    t, d = x.shape
    n = w.shape[-1]
    assert t % tm == 0 and n % tn == 0 and rows_per_mod % tm == 0
    mod_idx = lambda i, j: ((i * tm) // rows_per_mod, 0, 0)
    in_specs = [pl.BlockSpec((tm, d), lambda i, j: (i, 0)),
                pl.BlockSpec((None, 1, d), lambda i, j: (layer, 0, 0)),
                pl.BlockSpec((None, 1, d), mod_idx),
                pl.BlockSpec((None, 1, d), mod_idx),
                pl.BlockSpec((None, d, tn), lambda i, j: (layer, 0, j))]
    args = [x, g, sc, sh, w]
    out_specs = [pl.BlockSpec((tm, tn), lambda i, j: (i, j))]
    out_shape = [jax.ShapeDtypeStruct((t, n), out_dtype)]
    if gates is not None:
        wg, bg = gates
        in_specs += [pl.BlockSpec((None, d, LANES), lambda i, j: (layer, 0, 0)),
                     pl.BlockSpec((None, 1, LANES), lambda i, j: (layer, 0, 0))]
        args += [wg, bg]
        out_specs.append(pl.BlockSpec((tm, N_GATES), lambda i, j: (i, 0)))
        out_shape.append(jax.ShapeDtypeStruct((t, N_GATES), F32))
    return pl.pallas_call(
        functools.partial(_norm_mm_kernel, with_gates=gates is not None, relu2=relu2),
        grid=(t // tm, n // tn),
        in_specs=in_specs, out_specs=out_specs, out_shape=out_shape,
        scratch_shapes=[pltpu.VMEM((tm, d), BF16)],
        compiler_params=_cparams(2),
        name=name,
    )(*args)


def _mm_resid_kernel(*refs, nj, tn, final_norm):
    if final_norm:
        a_ref, w_ref, r_ref, gate_ref, fg_ref, o_ref = refs
    else:
        a_ref, w_ref, r_ref, gate_ref, o_ref = refs
    j = pl.program_id(1)
    cols = pl.ds(pl.multiple_of(j * tn, tn), tn)
    o_ref[:, cols] = r_ref[...] + gate_ref[...] * _dot(a_ref[...], w_ref[...])
    if final_norm:
        @pl.when(j == nj - 1)
        def _():
            xn = o_ref[...]
            ms = jnp.mean(xn * xn, axis=-1, keepdims=True)
            o_ref[...] = xn * lax.rsqrt(ms + EPS) * fg_ref[...]


def _mm_resid_call(a, w, layer, resid, gate, rows_per_mod, tm, tn, final_g=None, name="mm_resid"):
    t, kdim = a.shape
    n = w.shape[-1]
    assert t % tm == 0 and n % tn == 0 and rows_per_mod % tm == 0
    nj = n // tn
    mod_idx = lambda i, j: ((i * tm) // rows_per_mod, 0, j)
    in_specs = [pl.BlockSpec((tm, kdim), lambda i, j: (i, 0)),
                pl.BlockSpec((None, kdim, tn), lambda i, j: (layer, 0, j)),
                pl.BlockSpec((tm, tn), lambda i, j: (i, j)),
                pl.BlockSpec((None, 1, tn), mod_idx)]
    args = [a, w, resid, gate]
    if final_g is not None:
        in_specs.append(pl.BlockSpec((1, n), lambda i, j: (0, 0)))
        args.append(final_g)
    return pl.pallas_call(
        functools.partial(_mm_resid_kernel, nj=nj, tn=tn, final_norm=final_g is not None),
        grid=(t // tm, nj),
        in_specs=in_specs,
        out_specs=pl.BlockSpec((tm, n), lambda i, j: (i, 0)),
        out_shape=jax.ShapeDtypeStruct((t, n), F32),
        compiler_params=_cparams(2),
        name=name,
    )(*args)


def _merge_kernel(ym_ref, yr_ref, ts_ref, sg_ref, wm_ref, wr_ref, ws_ref, g0_ref, g1_ref, g2_ref,
                  o_ref, ys_scr):
    @pl.when(pl.program_id(1) == 0)
    def _():
        t = ts_ref[...]
        ms = jnp.mean(t * t, axis=-1, keepdims=True)
        ys_scr[...] = (t * lax.rsqrt(ms + EPS) * sg_ref[...]).astype(BF16)

    acc = _sigmoid(g0_ref[...].astype(F32)) * _dot(ym_ref[...], wm_ref[...])
    acc += _sigmoid(g1_ref[...].astype(F32)) * _dot(yr_ref[...], wr_ref[...])
    acc += _sigmoid(g2_ref[...].astype(F32)) * _dot(ys_scr[...], ws_ref[...])
    o_ref[...] = acc.astype(o_ref.dtype)


def _merge_call(ym, yr, ts, sg, wm, wr, ws, layer, proj, tm, tn):
    t, w = ym.shape
    d = wm.shape[-1]
    assert t % tm == 0 and d % tn == 0
    per_br = d // tn
    yspec = pl.BlockSpec((tm, w), lambda i, j: (i, 0))
    wspec = pl.BlockSpec((None, w, tn), lambda i, j: (layer, 0, j))
    gspec = lambda br: pl.BlockSpec((tm, tn), lambda i, j: (i, br * per_br + j))
    return pl.pallas_call(
        _merge_kernel,
        grid=(t // tm, d // tn),
        in_specs=[yspec, yspec, yspec, pl.BlockSpec((1, w), lambda i, j: (0, 0)),
                  wspec, wspec, wspec, gspec(0), gspec(1), gspec(2)],
        out_specs=pl.BlockSpec((tm, tn), lambda i, j: (i, j)),
        out_shape=jax.ShapeDtypeStruct((t, d), BF16),
        scratch_shapes=[pltpu.VMEM((tm, w), BF16)],
        compiler_params=_cparams(2),
        name="merge",
    )(ym, yr, ts, sg, wm, wr, ws, proj, proj, proj)


def _chunk_masks():
    row = lax.broadcasted_iota(jnp.int32, (CHUNK, CHUNK), 0)
    col = lax.broadcasted_iota(jnp.int32, (CHUNK, CHUNK), 1)
    lower = col <= row
    upper = col >= row
    tri = jnp.where(lower, 1.0, 0.0).astype(BF16)
    return row, col, lower, upper, tri


def _mlstm_kernel(*refs, nc, ns, has_state, emit_state, alias_in):
    it = iter(refs)
    q_ref, k_ref, v_ref, o_ref, g_ref, ng_ref = (next(it) for _ in range(6))
    if has_state:
        c0_ref, n0_ref, m0_ref = (next(it) for _ in range(3))
    if alias_in:
        next(it)
    y_ref = next(it)
    if emit_state:
        cf_ref, nf_ref, mf_ref = (next(it) for _ in range(3))
    hacc = (next(it), next(it))
    c_scr = (next(it), next(it))
    n_scr = (next(it), next(it))
    m_scr = (next(it), next(it))
    p_scr = (next(it), next(it))
    ml_scr = (next(it), next(it))
    bc_scr = (next(it), next(it))
    wl_scr = (next(it), next(it))
    sc_scr = (next(it), next(it))

    _, _, lower, upper, tri = _chunk_masks()
    ones_bf = jnp.ones((CHUNK, LANES), BF16)
    dirs = (0, 1)
    masks = (lower, upper)
    sub8 = lax.broadcasted_iota(jnp.int32, (8, LANES), 0)

    def gates(i, carry, base):
        cidx = (2 * i, 2 * i + 1)
        two = (0, 1)
        rows = [pl.ds(pl.multiple_of(c * CHUNK, CHUNK), CHUNK) for c in cidx]
        grow = [pl.ds(pl.multiple_of(base + c * CHUNK, CHUNK), CHUNK) for c in cidx]
        g = [g_ref[r, :] for r in grow]
        cs = [_cumsum_rows(x, tri) for x in g]
        tot = [x[CHUNK - 1:CHUNK, :] for x in cs]
        g_t = [x.T for x in g]
        cs_t = [x.T for x in cs]
        tot_t = [x[:, CHUNK - 1:CHUNK] for x in cs_t]
        bsum = [[cs[j], tot[j] - cs[j] + g[j]] for j in two]
        bsum_t = [[cs_t[j], tot_t[j] - cs_t[j] + g_t[j]] for j in two]
        items = [(j, d) for j in two for d in dirs]
        b_col = {(j, d): bsum[j][d][:, 2 + d:3 + d] for (j, d) in items}
        i_col = {(j, d): g[j][:, d:d + 1] for (j, d) in items}
        t_row = {(j, d): g_t[j][d:d + 1, :] - bsum_t[j][d][2 + d:3 + d, :] for (j, d) in items}
        b_last = {(j, d): tot[j][:, 2 + d:3 + d] for (j, d) in items}
        logw = {(j, d): jnp.where(masks[d], b_col[j, d] + t_row[j, d], -jnp.inf) for (j, d) in items}
        m_loc = {it_: jnp.max(logw[it_], axis=1, keepdims=True) for it_ in items}
        m_chunk = {it_: jnp.max(b_last[it_] + t_row[it_], axis=1, keepdims=True) for it_ in items}
        for (j, d) in items:
            p_scr[d][rows[j], :] = jnp.exp(logw[j, d] - m_loc[j, d])
            ml_scr[d][rows[j], :] = jnp.broadcast_to(m_loc[j, d], (CHUNK, LANES))
            bc_scr[d][rows[j], :] = jnp.broadcast_to(b_col[j, d], (CHUNK, LANES))
            wl_scr[d][rows[j], :] = jnp.broadcast_to(b_last[j, d] - b_col[j, d] + i_col[j, d], (CHUNK, LANES))
            sc_scr[d][pl.ds(pl.multiple_of(cidx[j] * 8, 8), 8), :] = jnp.where(
                sub8 == 0, jnp.broadcast_to(m_chunk[j, d], (8, LANES)), jnp.broadcast_to(b_last[j, d], (8, LANES)))
        return carry

    def rep2(x):
        return jnp.concatenate([x, x], axis=1)

    def scan(i, carry, base):
        cidx = (i, nc - 1 - i)
        rows = [pl.ds(pl.multiple_of(c * CHUNK, CHUNK), CHUNK) for c in cidx]
        grow = [pl.ds(pl.multiple_of(base + c * CHUNK, CHUNK), CHUNK) for c in cidx]
        sc = [sc_scr[d][pl.ds(pl.multiple_of(cidx[d] * 8, 8), 8), :] for d in dirs]
        m_chunk = [sc[d][0:1, 0:1] for d in dirs]
        b_last = [sc[d][1:2, 0:1] for d in dirs]
        qb = [q_ref[r, :].astype(BF16) for r in grow]
        k = [k_ref[r, :].astype(F32) * (M_DK ** -0.5) for r in grow]
        kb = [x.astype(BF16) for x in k]
        vb = [v_ref[r, :].astype(BF16) for r in grow]
        qk = [_dot_nt(qb[d], kb[d]) for d in dirs]
        sb = [(qk[d] * p_scr[d][rows[d], :]).astype(BF16) for d in dirs]
        a_num = [_dot(sb[d], vb[d]) for d in dirs]
        a_den = [_dot(sb[d], ones_bf) for d in dirs]

        m_prev = [m_scr[d][0:1, 0:1] for d in dirs]
        c_prev = [c_scr[d][...] for d in dirs]
        n_prev = [n_scr[d][0:1, :] for d in dirs]
        m_new = [jnp.maximum(b_last[d] + m_prev[d], m_chunk[d]) for d in dirs]
        kw = [k[d] * rep2(jnp.exp(wl_scr[d][rows[d], :] - m_new[d])) for d in dirs]
        kw_t = [x.T.astype(BF16) for x in kw]
        q_c = [_dot(qb[d], c_prev[d].astype(BF16)) for d in dirs]
        u_c = [_dot(kw_t[d], vb[d]) for d in dirs]
        m_loc = [ml_scr[d][rows[d], :] for d in dirs]
        gg = [bc_scr[d][rows[d], :] + m_prev[d] for d in dirs]
        m_tot = [jnp.maximum(m_loc[d], gg[d]) for d in dirs]
        e_intra = [jnp.exp(m_loc[d] - m_tot[d]) for d in dirs]
        e_inter = [jnp.exp(gg[d] - m_tot[d]) for d in dirs]
        q_n = [_dot_nt(qb[d], jnp.broadcast_to(n_prev[d].astype(BF16), (LANES, M_DK))) for d in dirs]
        den = [e_intra[d] * a_den[d] + e_inter[d] * q_n[d] for d in dirs]
        inv = [1.0 / jnp.maximum(jnp.abs(den[d]), jnp.exp(-m_tot[d])) for d in dirs]
        for d in dirs:
            hacc[d][rows[d], :] = rep2(e_intra[d] * inv[d]) * a_num[d] + rep2(e_inter[d] * inv[d]) * q_c[d]

        u_n = [jnp.sum(x, axis=0, keepdims=True) for x in kw]
        a = [jnp.exp(b_last[d] + m_prev[d] - m_new[d]) for d in dirs]
        for d in dirs:
            c_scr[d][...] = a[d] * c_prev[d] + u_c[d]
            n_scr[d][0:1, :] = a[d] * n_prev[d] + u_n[d]
            m_scr[d][0:1, :] = jnp.broadcast_to(m_new[d], (1, LANES))
        return carry

    def finish(c, carry, base):
        rows = pl.ds(pl.multiple_of(c * CHUNK, CHUNK), CHUNK)
        grow = pl.ds(pl.multiple_of(base + c * CHUNK, CHUNK), CHUNK)
        ht = hacc[0][rows, :] + hacc[1][rows, :]
        ms = jnp.mean(ht * ht, axis=1, keepdims=True)
        y = ht * lax.rsqrt(ms + EPS) * ng_ref[...] * _sigmoid(o_ref[grow, :].astype(F32))
        y_ref[grow, :] = y.astype(y_ref.dtype)
        return carry

    for sq in range(ns):
        base = sq * nc * CHUNK
        for d in range(2):
            if has_state:
                c_scr[d][...] = c0_ref[sq, d]
                n_scr[d][0:1, :] = n0_ref[sq, d:d + 1, :]
                m_scr[d][0:1, :] = m0_ref[sq, d:d + 1, :]
            else:
                c_scr[d][...] = jnp.zeros_like(c_scr[d])
                n_scr[d][...] = jnp.zeros_like(n_scr[d])
                m_scr[d][...] = jnp.zeros_like(m_scr[d])
        lax.fori_loop(0, nc // 2, functools.partial(gates, base=base), 0, unroll=2)
        lax.fori_loop(0, nc, functools.partial(scan, base=base), 0, unroll=SCAN_UNROLL)
        lax.fori_loop(0, nc, functools.partial(finish, base=base), 0, unroll=SCAN_UNROLL)
        if emit_state:
            for d in range(2):
                if alias_in:
                    cf_ref[sq, d] = c_scr[d][...]
                else:
                    for slot in range(DEPTH):
                        cf_ref[sq, slot, d] = c_scr[d][...]
                nf_ref[sq, d:d + 1, :] = n_scr[d][0:1, :]
                mf_ref[sq, d:d + 1, :] = m_scr[d][0:1, :]


def _mlstm_call(proj, gates, ng, nb, seq, ns, state=None, emit=None):
    emit_state = emit is not None
    assert nb % ns == 0
    t = proj.shape[0]
    nc = seq // CHUNK
    dk = M_DK
    cspec = lambda base: pl.BlockSpec((ns * seq, dk), lambda b, h: (b, base // dk + h))
    in_specs = [cspec(OFF_MQ), cspec(OFF_MK), cspec(OFF_MV), cspec(OFF_MO),
                pl.BlockSpec((ns * seq, LANES), lambda b, h: (b, h)),
                pl.BlockSpec((1, dk), lambda b, h: (0, h))]
    args = [proj, proj, proj, proj, gates, ng]
    if state is not None:
        sl = state[0]
        in_specs += [pl.BlockSpec((ns, None, 2, None, dk, dk), lambda b, h: (b, sl, 0, h, 0, 0)),
                     pl.BlockSpec((ns, None, 2, dk), lambda b, h: (b, h, 0, 0)),
                     pl.BlockSpec((ns, None, 2, LANES), lambda b, h: (b, h, 0, 0))]
        args += list(state[1:])
    out_specs = [pl.BlockSpec((ns * seq, dk), lambda b, h: (b, h))]
    out_shape = [jax.ShapeDtypeStruct((t, M_HEADS * dk), BF16)]
    aliases = {}
    if emit_state:
        layer, prev = emit
        slot_blk, slot_idx = (None, layer) if prev is not None else (DEPTH, 0)
        out_specs += [pl.BlockSpec((ns, slot_blk, 2, None, dk, dk), lambda b, h: (b, slot_idx, 0, h, 0, 0)),
                      pl.BlockSpec((ns, None, 2, dk), lambda b, h: (b, h, 0, 0)),
                      pl.BlockSpec((ns, None, 2, LANES), lambda b, h: (b, h, 0, 0))]
        out_shape += [jax.ShapeDtypeStruct((nb, DEPTH, 2, M_HEADS, dk, dk), F32),
                      jax.ShapeDtypeStruct((nb, M_HEADS, 2, dk), F32),
                      jax.ShapeDtypeStruct((nb, M_HEADS, 2, LANES), F32)]
        if prev is not None:
            in_specs.append(pl.BlockSpec(memory_space=pl.ANY))
            args.append(prev)
            aliases = {len(args) - 1: 1}
    return pl.pallas_call(
        functools.partial(_mlstm_kernel, nc=nc, ns=ns, has_state=state is not None, emit_state=emit_state,
                          alias_in=bool(aliases)),
        grid=(nb // ns, M_HEADS),
        in_specs=in_specs, out_specs=out_specs, out_shape=out_shape, input_output_aliases=aliases,
        scratch_shapes=([pltpu.VMEM((seq, dk), F32)] * 2 + [pltpu.VMEM((dk, dk), F32)] * 2
                        + [pltpu.VMEM((8, dk), F32)] * 2 + [pltpu.VMEM((8, LANES), F32)] * 2
                        + [pltpu.VMEM((seq, LANES), F32)] * 8 + [pltpu.VMEM((nc * 8, LANES), F32)] * 2),
        compiler_params=_cparams(2),
        name="mlstm",
    )(*args)


R_HPS = 2


def _ret_kernel(*refs, nc, ns, has_state, emit_state, rope, alias_in):
    it = iter(refs)
    q_ref, k_ref, v_ref, g_ref, lg_ref, ng_ref = (next(it) for _ in range(6))
    if rope:
        cos_ref, sin_ref = next(it), next(it)
    if has_state:
        s0_ref = next(it)
    if alias_in:
        next(it)
    y_ref = next(it)
    if emit_state:
        sf_ref = next(it)
    chains = [(h, d) for h in range(R_HPS) for d in range(2)]
    oacc = {c: next(it) for c in chains}
    s_scr = {c: next(it) for c in chains}
    qs_scr, ks_scr, kt_scr = (next(it) for _ in range(3))

    row, col, lower, upper, _ = _chunk_masks()
    rel = (row - col).astype(F32)
    pos = row.astype(F32)
    decay, w_q, w_k, chunk_decay = {}, {}, {}, {}
    for (h, d) in chains:
        lgd = -jnp.exp(lg_ref[h, d:d + 1, :])
        if d == 0:
            decay[h, d] = jnp.where(lower, jnp.exp(rel * lgd), 0.0)
            w_q[h, d] = jnp.exp((pos + 1.0) * lgd)
            w_k[h, d] = jnp.exp((CHUNK - 1.0 - pos) * lgd)
        else:
            decay[h, d] = jnp.where(upper, jnp.exp(-rel * lgd), 0.0)
            w_q[h, d] = jnp.exp((CHUNK - pos) * lgd)
            w_k[h, d] = jnp.exp(pos * lgd)
        chunk_decay[h, d] = jnp.exp(CHUNK * lgd)

    def hcols(h):
        return slice(h * R_DK, (h + 1) * R_DK)

    def prep(c, carry, base):
        rows = pl.ds(pl.multiple_of(c * CHUNK, CHUNK), CHUNK)
        grow = pl.ds(pl.multiple_of(base + c * CHUNK, CHUNK), CHUNK)
        for h in range(R_HPS):
            q = q_ref[grow, hcols(h)].astype(F32)
            k = k_ref[grow, hcols(h)].astype(F32) * (R_DK ** -0.5)
            if rope:
                cs, sn = cos_ref[grow, :], sin_ref[grow, :]
                q = q * cs + pltpu.roll(q, R_DK // 2, 1) * sn
                k = k * cs + pltpu.roll(k, R_DK // 2, 1) * sn
            qs_scr[rows, hcols(h)] = q.astype(BF16)
            ks_scr[rows, hcols(h)] = k.astype(BF16)
            kt_scr[hcols(h), rows] = k.T.astype(BF16)
        return carry

    def scan(i, carry, base):
        rows = {0: pl.ds(pl.multiple_of(i * CHUNK, CHUNK), CHUNK),
                1: pl.ds(pl.multiple_of((nc - 1 - i) * CHUNK, CHUNK), CHUNK)}
        grow = {0: pl.ds(pl.multiple_of(base + i * CHUNK, CHUNK), CHUNK),
                1: pl.ds(pl.multiple_of(base + (nc - 1 - i) * CHUNK, CHUNK), CHUNK)}
        qb = {(h, d): qs_scr[rows[d], hcols(h)] for (h, d) in chains}
        kb = {(h, d): ks_scr[rows[d], hcols(h)] for (h, d) in chains}
        kt = {(h, d): kt_scr[hcols(h), rows[d]] for (h, d) in chains}
        vb = {(h, d): v_ref[grow[d], hcols(h)].astype(BF16) for (h, d) in chains}
        vw = {c: (vb[c].astype(F32) * w_k[c]).astype(BF16) for c in chains}
        s_prev = {c: s_scr[c][...] for c in chains}
        sc = {c: (_dot_nt(qb[c], kb[c]) * decay[c]).astype(BF16) for c in chains}
        inter = {c: _dot(qb[c], s_prev[c].astype(BF16)) for c in chains}
        u = {c: _dot(kt[c], vw[c]) for c in chains}
        intra = {c: _dot(sc[c], vb[c]) for c in chains}
        for (h, d) in chains:
            oacc[h, d][rows[d], :] = intra[h, d] + w_q[h, d] * inter[h, d]
            s_scr[h, d][...] = chunk_decay[h, d] * s_prev[h, d] + u[h, d]
        return carry

    def finish(c, carry, base):
        rows = pl.ds(pl.multiple_of(c * CHUNK, CHUNK), CHUNK)
        grow = pl.ds(pl.multiple_of(base + c * CHUNK, CHUNK), CHUNK)
        for h in range(R_HPS):
            ot = oacc[h, 0][rows, :] + oacc[h, 1][rows, :]
            ms = jnp.mean(ot * ot, axis=1, keepdims=True)
            y = ot * lax.rsqrt(ms + EPS) * ng_ref[:, hcols(h)] * _silu(g_ref[grow, hcols(h)].astype(F32))
            y_ref[grow, hcols(h)] = y.astype(y_ref.dtype)
        return carry

    for sq in range(ns):
        base = sq * nc * CHUNK
        for (h, d) in chains:
            if has_state:
                s_scr[h, d][...] = s0_ref[sq, d, h]
            else:
                s_scr[h, d][...] = jnp.zeros_like(s_scr[h, d])
        lax.fori_loop(0, nc, functools.partial(prep, base=base), 0, unroll=SCAN_UNROLL)
        lax.fori_loop(0, nc, functools.partial(scan, base=base), 0, unroll=SCAN_UNROLL)
        lax.fori_loop(0, nc, functools.partial(finish, base=base), 0, unroll=SCAN_UNROLL)
        if emit_state:
            for (h, d) in chains:
                if alias_in:
                    sf_ref[sq, d, h] = s_scr[h, d][...]
                else:
                    for slot in range(DEPTH):
                        sf_ref[sq, slot, d, h] = s_scr[h, d][...]


def _ret_call(proj, lgr, ng, nb, seq, ns, rope=None, state=None, emit=None):
    emit_state = emit is not None
    assert nb % ns == 0 and (rope is None or ns == 1)
    t = proj.shape[0]
    nc = seq // CHUNK
    dk = R_DK
    bw = R_HPS * dk
    cspec = lambda base: pl.BlockSpec((ns * seq, bw), lambda b, h: (b, base // bw + h))
    in_specs = [cspec(OFF_RQ), cspec(OFF_RK), cspec(OFF_RV), cspec(OFF_RG),
                pl.BlockSpec((R_HPS, 2, LANES), lambda b, h: (h, 0, 0)),
                pl.BlockSpec((1, bw), lambda b, h: (0, h))]
    args = [proj, proj, proj, proj, lgr, ng]
    if rope is not None:
        in_specs += [pl.BlockSpec((seq, dk), lambda b, h: (0, 0))] * 2
        args += list(rope)
    if state is not None:
        sl = state[0]
        in_specs.append(pl.BlockSpec((ns, None, 2, R_HPS, dk, dk), lambda b, h: (b, sl, 0, h, 0, 0)))
        args.append(state[1])
    out_specs = [pl.BlockSpec((ns * seq, bw), lambda b, h: (b, h))]
    out_shape = [jax.ShapeDtypeStruct((t, R_HEADS * dk), BF16)]
    aliases = {}
    if emit_state:
        layer, prev = emit
        slot_blk, slot_idx = (None, layer) if prev is not None else (DEPTH, 0)
        out_specs.append(pl.BlockSpec((ns, slot_blk, 2, R_HPS, dk, dk), lambda b, h: (b, slot_idx, 0, h, 0, 0)))
        out_shape.append(jax.ShapeDtypeStruct((nb, DEPTH, 2, R_HEADS, dk, dk), F32))
        if prev is not None:
            in_specs.append(pl.BlockSpec(memory_space=pl.ANY))
            args.append(prev)
            aliases = {len(args) - 1: 1}
    n_chain = 2 * R_HPS
    return pl.pallas_call(
        functools.partial(_ret_kernel, nc=nc, ns=ns, has_state=state is not None, emit_state=emit_state,
                          rope=rope is not None, alias_in=bool(aliases)),
        grid=(nb // ns, R_HEADS // R_HPS),
        in_specs=in_specs, out_specs=out_specs, out_shape=out_shape, input_output_aliases=aliases,
        scratch_shapes=([pltpu.VMEM((seq, dk), F32)] * n_chain + [pltpu.VMEM((dk, dk), F32)] * n_chain
                        + [pltpu.VMEM((seq, bw), BF16)] * 2 + [pltpu.VMEM((bw, seq), BF16)]),
        compiler_params=_cparams(2),
        name="retention",
    )(*args)


def _ssd_kernel(*refs, nc, ns, has_state, emit_state, alias_in):
    it = iter(refs)
    (x_ref, b_ref, c_ref, z_ref, g_ref, wx_ref, wb_ref, wc_ref, bx_ref, bb_ref, bc_ref,
     al_ref, sd_ref) = (next(it) for _ in range(13))
    if has_state:
        h0_ref = next(it)
    if alias_in:
        next(it)
    y_ref = next(it)
    if emit_state:
        hf_ref = next(it)
    xs_scr, bt_scr, bb_scr, cc_scr, ht_scr, yb_scr = (next(it) for _ in range(6))

    _, _, lower, upper, tri = _chunk_masks()
    er = lax.broadcasted_iota(jnp.int32, (LANES, S_GW), 0)
    ec = lax.broadcasted_iota(jnp.int32, (LANES, S_GW), 1) // S_P
    expand = [jnp.where(er == ec + d * S_HPG, 1.0, 0.0).astype(BF16) for d in range(2)]
    a_row = -jnp.exp(al_ref[...])

    wr = lax.broadcasted_iota(jnp.int32, (CHUNK, CHUNK + 32), 0)
    wc = lax.broadcasted_iota(jnp.int32, (CHUNK, CHUNK + 32), 1)
    shifts = [jnp.where(wc == wr + 16 + sft, 1.0, 0.0).astype(BF16) for sft in (-1, 1, 2)]

    def conv_silu(ref, w_ref, bias_ref, c, base):
        off = base + c * CHUNK
        cur = ref[pl.ds(pl.multiple_of(off, CHUNK), CHUNK), :]
        prev_w = ref[pl.ds(pl.multiple_of(jnp.maximum(off - 16, 0), 16), 16), :]
        next_w = ref[pl.ds(pl.multiple_of(jnp.minimum(off + CHUNK, ns * nc * CHUNK - 16), 16), 16), :]
        prev_w = jnp.where(c > 0, prev_w, jnp.zeros_like(prev_w))
        next_w = jnp.where(c < nc - 1, next_w, jnp.zeros_like(next_w))
        win = jnp.concatenate([prev_w, cur, next_w], axis=0)
        xm1, xp1, xp2 = (_dot(sm, win) for sm in shifts)
        w = w_ref[...]
        y = w[0:1, :] * xm1 + w[1:2, :] * cur.astype(F32) + w[2:3, :] * xp1 + w[3:4, :] * xp2 + bias_ref[...]
        return _silu(y)

    def prep(c, carry, base):
        off = pl.multiple_of(c * CHUNK, CHUNK)
        rows = pl.ds(off, CHUNK)
        xs_scr[rows, :] = conv_silu(x_ref, wx_ref, bx_ref, c, base)
        bm = conv_silu(b_ref, wb_ref, bb_ref, c, base)
        bb_scr[rows, :] = bm.astype(BF16)
        bt_scr[:, rows] = bm.T.astype(BF16)
        cc_scr[rows, :] = conv_silu(c_ref, wc_ref, bc_ref, c, base).astype(BF16)
        return carry

    dirs = (0, 1)
    masks = (lower, upper)
    first_half = lax.broadcasted_iota(jnp.int32, (CHUNK, LANES), 1) < S_P

    def scan(i, carry, base):
        rows = [pl.ds(pl.multiple_of(c * CHUNK, CHUNK), CHUNK) for c in (i, nc - 1 - i)]
        grow = [pl.ds(pl.multiple_of(base + c * CHUNK, CHUNK), CHUNK) for c in (i, nc - 1 - i)]
        dt = [g_ref[r, :] for r in grow]
        a = [x * a_row for x in dt]
        cs = [_cumsum_rows(x, tri) for x in a]
        tot = [x[CHUNK - 1:CHUNK, :] for x in cs]
        bsum = [cs[0], tot[1] - cs[1] + a[1]]
        bsum_t = [x.T for x in bsum]
        dt_t = [x.T for x in dt]
        xs = [xs_scr[r, :] for r in rows]
        xb = [x.astype(BF16) for x in xs]
        bmb = [bb_scr[r, :] for r in rows]
        cmb = [cc_scr[r, :] for r in rows]
        btb = [bt_scr[:, r] for r in rows]
        cb = [_dot_nt(cmb[d], bmb[d]) for d in dirs]
        ht = [ht_scr[d] for d in dirs]
        inter = [_dot(cmb[d], ht[d].astype(BF16)) for d in dirs]
        ex = [_expand(jnp.exp(bsum[d]), expand[d], exact=False) for d in dirs]
        wexp = [_expand(jnp.exp(tot[d] - bsum[d]) * dt[d], expand[d], exact=False) for d in dirs]
        cd = [_expand(jnp.broadcast_to(jnp.exp(tot[d]), (8, LANES)), expand[d])[0:1, :] for d in dirs]
        u_t = [_dot(btb[d], (xs[d] * wexp[d]).astype(BF16)) for d in dirs]
        ys = [[], []]
        for p in range(S_HPG // 2):
            for d in dirs:
                ms = []
                for k in (2 * p, 2 * p + 1):
                    ln = d * S_HPG + k
                    decay = jnp.exp(jnp.where(masks[d], bsum[d][:, ln:ln + 1] - bsum_t[d][ln:ln + 1, :], -jnp.inf))
                    ms.append((cb[d] * decay * dt_t[d][ln:ln + 1, :]).astype(BF16))
                xp = xb[d][:, p * LANES:(p + 1) * LANES]
                zero = jnp.zeros_like(xp)
                rhs = jnp.concatenate([jnp.where(first_half, xp, zero), jnp.where(first_half, zero, xp)], axis=0)
                ys[d].append(_dot(jnp.concatenate(ms, axis=1), rhs))
        y_ref[grow[0], :] = jnp.concatenate(ys[0], axis=1) + ex[0] * inter[0]
        yb_scr[rows[1], :] = jnp.concatenate(ys[1], axis=1) + ex[1] * inter[1]
        for d in dirs:
            ht_scr[d] = ht[d] * cd[d] + u_t[d]
        return carry

    def finish(c, carry, base):
        rows = pl.ds(pl.multiple_of(c * CHUNK, CHUNK), CHUNK)
        grow = pl.ds(pl.multiple_of(base + c * CHUNK, CHUNK), CHUNK)
        yt = y_ref[grow, :] + yb_scr[rows, :] + sd_ref[...] * xs_scr[rows, :]
        y_ref[grow, :] = yt * _silu(z_ref[grow, :].astype(F32))
        return carry

    for sq in range(ns):
        base = sq * nc * CHUNK
        lax.fori_loop(0, nc, functools.partial(prep, base=base), 0, unroll=SCAN_UNROLL)
        for d in range(2):
            if has_state:
                ht_scr[d] = h0_ref[sq, d].T
            else:
                ht_scr[d] = jnp.zeros((S_N, S_GW), F32)
        lax.fori_loop(0, nc, functools.partial(scan, base=base), 0, unroll=SCAN_UNROLL)
        lax.fori_loop(0, nc, functools.partial(finish, base=base), 0, unroll=SCAN_UNROLL)
        if emit_state:
            for d in range(2):
                h_fin = ht_scr[d].T
                if alias_in:
                    hf_ref[sq, d] = h_fin
                else:
                    for slot in range(DEPTH):
                        hf_ref[sq, slot, d] = h_fin


def _ssd_call(proj, gates, conv_w, conv_b, alog, sd, nb, seq, ns, state=None, emit=None):
    emit_state = emit is not None
    assert nb % ns == 0
    t = proj.shape[0]
    nc = seq // CHUNK
    gw = S_GW
    n = S_N
    xw = S_HEADS * S_P
    in_specs = [pl.BlockSpec((ns * seq, gw), lambda b, g: (b, OFF_SX // gw + g)),
                pl.BlockSpec((ns * seq, n), lambda b, g: (b, OFF_SB // n + g)),
                pl.BlockSpec((ns * seq, n), lambda b, g: (b, OFF_SC // n + g)),
                pl.BlockSpec((ns * seq, gw), lambda b, g: (b, OFF_SZ // gw + g)),
                pl.BlockSpec((ns * seq, LANES), lambda b, g: (b, M_HEADS + g)),
                pl.BlockSpec((4, gw), lambda b, g: (0, g)),
                pl.BlockSpec((4, n), lambda b, g: (0, xw // n + g)),
                pl.BlockSpec((4, n), lambda b, g: (0, xw // n + S_GROUPS + g)),
                pl.BlockSpec((1, gw), lambda b, g: (0, g)),
                pl.BlockSpec((1, n), lambda b, g: (0, xw // n + g)),
                pl.BlockSpec((1, n), lambda b, g: (0, xw // n + S_GROUPS + g)),
                pl.BlockSpec((None, 1, LANES), lambda b, g: (g, 0, 0)),
                pl.BlockSpec((1, gw), lambda b, g: (0, g))]
    args = [proj, proj, proj, proj, gates, conv_w, conv_w, conv_w, conv_b, conv_b, conv_b, alog, sd]
    if state is not None:
        sl = state[0]
        in_specs.append(pl.BlockSpec((ns, None, 2, None, gw, n), lambda b, g: (b, sl, 0, g, 0, 0)))
        args.append(state[1])
    out_specs = [pl.BlockSpec((ns * seq, gw), lambda b, g: (b, g))]
    out_shape = [jax.ShapeDtypeStruct((t, xw), F32)]
    aliases = {}
    if emit_state:
        layer, prev = emit
        slot_blk, slot_idx = (None, layer) if prev is not None else (DEPTH, 0)
        out_specs.append(pl.BlockSpec((ns, slot_blk, 2, None, gw, n), lambda b, g: (b, slot_idx, 0, g, 0, 0)))
        out_shape.append(jax.ShapeDtypeStruct((nb, DEPTH, 2, S_GROUPS, gw, n), F32))
        if prev is not None:
            in_specs.append(pl.BlockSpec(memory_space=pl.ANY))
            args.append(prev)
            aliases = {len(args) - 1: 1}
    return pl.pallas_call(
        functools.partial(_ssd_kernel, nc=nc, ns=ns, has_state=state is not None, emit_state=emit_state,
                          alias_in=bool(aliases)),
        grid=(nb // ns, S_GROUPS),
        in_specs=in_specs, out_specs=out_specs, out_shape=out_shape, input_output_aliases=aliases,
        scratch_shapes=[pltpu.VMEM((seq, gw), F32), pltpu.VMEM((n, seq), BF16),
                        pltpu.VMEM((seq, n), BF16), pltpu.VMEM((seq, n), BF16),
                        pltpu.VMEM((2, n, gw), F32), pltpu.VMEM((seq, gw), F32)],
        compiler_params=_cparams(2),
        name="ssd",
    )(*args)


_IN_SIZES = (1024, 1024, 1024, 1024, 8, 8, 1024, 1024, 1024, 1024, 1024, 1536, 32, 6144)
_IN_OFFS = np.concatenate([[0], np.cumsum(_IN_SIZES)])


def _gate_lane_index():
    idx = -np.ones((LANES,), np.int64)
    for h in range(M_HEADS):
        base = GATE_LANE_OFFSETS[h]
        idx[base + 0] = 0 * M_HEADS + h
        idx[base + 1] = 1 * M_HEADS + h
        idx[base + 2] = 2 * M_HEADS + 0 * M_HEADS + h
        idx[base + 3] = 2 * M_HEADS + 1 * M_HEADS + h
    for g in range(S_GROUPS):
        base = GATE_LANE_OFFSETS[M_HEADS + g]
        for d in range(2):
            for k in range(S_HPG):
                idx[base + d * S_HPG + k] = 4 * M_HEADS + d * S_HEADS + g * S_HPG + k
    return idx


def _place(vals, idx):
    taken = jnp.take(vals, jnp.asarray(np.maximum(idx, 0)), axis=-1)
    return jnp.where(jnp.asarray(idx >= 0), taken, 0.0)


def _prep_params(w_in, m_igate_b, m_fgate_b, s_dt_bias, s_a_log, r_decay, s_d):
    o = _IN_OFFS
    w_big = jnp.concatenate([w_in[:, :, o[13]:o[14]], w_in[:, :, o[0]:o[4]], w_in[:, :, o[6]:o[12]]],
                            axis=2).astype(BF16)
    w_small = jnp.concatenate([w_in[:, :, o[4]:o[6]], w_in[:, :, o[12]:o[13]]], axis=2)
    gidx = _gate_lane_index()
    w_g = _place(w_small, gidx).astype(BF16)
    b_small = jnp.concatenate([m_igate_b.reshape(DEPTH, -1), m_fgate_b.reshape(DEPTH, -1),
                               s_dt_bias.reshape(DEPTH, -1)], axis=1)
    b_g = _place(b_small, gidx).reshape(DEPTH, 1, LANES)
    aidx = -np.ones((S_GROUPS, LANES), np.int64)
    for g in range(S_GROUPS):
        for d in range(2):
            for k in range(S_HPG):
                aidx[g, d * S_HPG + k] = d * S_HEADS + g * S_HPG + k
    a_flat = s_a_log.reshape(DEPTH, -1)
    alog = jnp.stack([_place(a_flat, aidx[g]) for g in range(S_GROUPS)], axis=1).reshape(DEPTH, S_GROUPS, 1, LANES)
    lgr = jnp.broadcast_to(jnp.swapaxes(r_decay, 1, 2)[..., None], (DEPTH, R_HEADS, 2, LANES))
    sd = jnp.repeat(s_d, S_P, axis=1).reshape(DEPTH, 1, S_HEADS * S_P)
    return w_big, w_g, b_g, alog, lgr, sd


def _rope_tables(seq):
    n_rows = seq // GRID_W
    rows = jnp.repeat(jnp.arange(n_rows, dtype=F32), GRID_W)
    cols = jnp.tile(jnp.arange(GRID_W, dtype=F32), n_rows)
    inv = ROPE_BASE ** (-jnp.arange(ROPE_FREQS, dtype=F32) / ROPE_FREQS)
    ang = jnp.concatenate([rows[:, None] * inv, cols[:, None] * inv], -1)
    cos, sin = jnp.cos(ang), jnp.sin(ang)
    return jnp.concatenate([cos, cos], -1), jnp.concatenate([-sin, sin], -1)


def _layer(x, l, mods, rows_per_mod, nb, seq, pw, rope, state, emit, final_g):
    sh_a, sc_a, g_a, sh_f, sc_f, g_f = mods
    tm_l, tm_s = min(TM_LARGE, rows_per_mod), min(TM_SMALL, rows_per_mod)
    proj, gates = _norm_mm_call(x, pw["norm_mix_g"], sc_a, sh_a, rows_per_mod, pw["w_big"], l, tm_l, TN_IN_PROJ,
                                BF16, False, gates=(pw["w_g"], pw["b_g"]), name="in_proj")
    st_m = st_r = st_s = None
    if state is not None:
        st_m, st_r, st_s = state
    em_m = em_r = em_s = None
    if emit is not None:
        em_m, em_r, em_s = ((l, prev) for prev in emit)
    ns = 1 if rope is not None else max(n for n in range(1, max(1, SEQ_ROWS_PER_STEP // seq) + 1) if nb % n == 0)
    om = _mlstm_call(proj, gates, pw["m_norm_g"][l], nb, seq, 1, state=st_m, emit=em_m)
    orr = _ret_call(proj, pw["lgr"][l], pw["r_norm_g"][l], nb, seq, ns, rope=rope, state=st_r, emit=em_r)
    os_ = _ssd_call(proj, gates, pw["s_conv_w"][l], pw["s_conv_b"][l], pw["alog"][l], pw["sd"][l], nb, seq, ns,
                    state=st_s, emit=em_s)
    merged = _merge_call(om[0], orr[0], os_[0], pw["s_norm_g"][l], pw["w_br_m"], pw["w_br_r"], pw["w_br_s"], l,
                         proj, tm_s, TN_MERGE)
    x = _mm_resid_call(merged, pw["w_out"], l, x, g_a, rows_per_mod, tm_s, D_MODEL, name="out_proj")
    (hid,) = _norm_mm_call(x, pw["norm_mlp_g"], sc_f, sh_f, rows_per_mod, pw["w_ff1"], l, tm_l, TN_MLP_UP,
                           BF16, True, name="mlp_up")
    x = _mm_resid_call(hid, pw["w_ff2"], l, x, g_f, rows_per_mod, tm_s, TN_MLP_DOWN, final_g=final_g, name="mlp_down")
    new_state = (om[1:], orr[1:], os_[1:]) if emit is not None else None
    return x, new_state


def kernel(x_prompt, x_sample, c, state_mlstm_C, state_mlstm_n, state_mlstm_m, state_ret, state_ssd, c_ctx, w_mod, b_mod, norm_mix_g, norm_mlp_g, w_in, m_igate_b, m_fgate_b, m_norm_g, r_decay, r_norm_g, s_conv_w, s_conv_b, s_dt_bias, s_a_log, s_d, s_norm_g, w_br_m, w_br_r, w_br_s, w_out, w_ff1, w_ff2, final_norm_g):
    bp, lp, d = x_prompt.shape
    bs, ls, _ = x_sample.shape
    xp = x_prompt.reshape(bp * lp, d)
    xs = x_sample.reshape(bs * ls, d)

    c_rows = jnp.zeros((8, d), F32).at[:bs].set(c).at[bs].set(c_ctx)
    mod = _mod_call(c_rows, w_mod, b_mod)
    rope = _rope_tables(ls)
    final_g = final_norm_g.reshape(1, d)

    w_big, w_g, b_g, alog, lgr, sd = _prep_params(w_in, m_igate_b, m_fgate_b, s_dt_bias, s_a_log, r_decay, s_d)
    pw = dict(w_big=w_big, w_g=w_g, b_g=b_g, alog=alog, lgr=lgr, sd=sd,
              norm_mix_g=norm_mix_g.reshape(DEPTH, 1, d), norm_mlp_g=norm_mlp_g.reshape(DEPTH, 1, d),
              m_norm_g=m_norm_g.reshape(DEPTH, 1, -1), r_norm_g=r_norm_g.reshape(DEPTH, 1, -1),
              s_norm_g=s_norm_g.reshape(DEPTH, 1, -1),
              s_conv_w=s_conv_w, s_conv_b=s_conv_b.reshape(DEPTH, 1, -1),
              w_br_m=w_br_m.astype(BF16), w_br_r=w_br_r.astype(BF16), w_br_s=w_br_s.astype(BF16),
              w_out=w_out.astype(BF16), w_ff1=w_ff1.astype(BF16), w_ff2=w_ff2.astype(BF16))
    cache_c = state_mlstm_C
    cache_r = state_ret
    cache_s = state_ssd.reshape(bs, DEPTH, 2, S_GROUPS, S_GW, S_N)

    big = (None, None, None)
    st_n, st_m = [], []
    for l in range(DEPTH):
        parts = mod[l].reshape(8, 6, 1, d)
        mods_ctx = tuple(parts[bs:bs + 1, i] for i in range(6))
        mods_lat = tuple(parts[:bs, i] for i in range(6))
        fg = final_g if l == DEPTH - 1 else None

        xp, st = _layer(xp, l, mods_ctx, bp * lp, bp, lp, pw, None, None, big, fg)
        (cf, nf, mf), (rf,), (hf,) = st
        big = (cf, rf, hf)
        st_n.append(jnp.transpose(nf, (0, 2, 1, 3)))
        st_m.append(jnp.transpose(mf[..., 0], (0, 2, 1)))

        cache = (
            (l, cache_c,
             jnp.transpose(state_mlstm_n[:, l], (0, 2, 1, 3)),
             jnp.broadcast_to(jnp.transpose(state_mlstm_m[:, l], (0, 2, 1))[..., None], (bs, M_HEADS, 2, LANES))),
            (l, cache_r),
            (l, cache_s),
        )
        xs, _ = _layer(xs, l, mods_lat, ls, bs, ls, pw, rope, cache, None, fg)

    return (xp.reshape(bp, lp, d), xs.reshape(bs, ls, d),
            big[0], jnp.stack(st_n, 1), jnp.stack(st_m, 1), big[1],
            big[2].reshape(bp, DEPTH, 2, S_HEADS, S_P, S_N))
```

```python
import functools

import numpy as np
import jax
import jax.numpy as jnp
from jax import lax
from jax.experimental import pallas as pl
from jax.experimental.pallas import tpu as pltpu

F32 = jnp.float32
BF16 = jnp.bfloat16

D_MODEL = 2048
DEPTH = 2
CHUNK = 128
EPS = 1e-6
M_HEADS, M_DK = 4, 256
R_HEADS, R_DK = 8, 128
S_HEADS, S_P, S_GROUPS, S_N = 16, 64, 2, 128
S_HPG = S_HEADS // S_GROUPS
S_GW = S_HPG * S_P
GRID_W = 64
ROPE_BASE = 10000.0
ROPE_FREQS = R_DK // 4
D_FF = 4 * D_MODEL
LANES = 128
VMEM_LIMIT = 56 * 1024 * 1024
SCAN_UNROLL = 8
SEQ_ROWS_PER_STEP = 1024
TM_LARGE, TM_SMALL = 1024, 512
TN_IN_PROJ, TN_MLP_UP, TN_MLP_DOWN, TN_MERGE = 1536, 2048, 512, 2048

N_GL = 3 * D_MODEL
OFF_MQ, OFF_MK, OFF_MV, OFF_MO = (N_GL + o for o in (0, 1024, 2048, 3072))
OFF_RQ, OFF_RK, OFF_RV, OFF_RG = (N_GL + o for o in (4096, 5120, 6144, 7168))
OFF_SZ, OFF_SX, OFF_SB, OFF_SC = (N_GL + o for o in (8192, 9216, 10240, 10496))
N_GATES = (M_HEADS + S_GROUPS) * LANES
GATE_LANE_OFFSETS = tuple(4 * h for h in range(M_HEADS)) + tuple(4 * M_HEADS + 2 * S_HPG * g for g in range(S_GROUPS))


def _cparams(n_axes):
    return pltpu.CompilerParams(dimension_semantics=("arbitrary",) * n_axes,
                                vmem_limit_bytes=VMEM_LIMIT)


def _dot(a, b):
    return jnp.dot(a, b, preferred_element_type=F32)


def _dot_nt(a, b):
    return lax.dot_general(a, b, (((1,), (1,)), ((), ())), preferred_element_type=F32)


def _split2(x):
    hi = x.astype(BF16)
    lo = (x - hi.astype(F32)).astype(BF16)
    return hi, lo


def _cumsum_rows(x, tri):
    hi, lo = _split2(x)
    return _dot(tri, hi) + _dot(tri, lo)


def _expand(x, e, exact=True):
    if not exact:
        return _dot(x.astype(BF16), e)
    hi, lo = _split2(x)
    return _dot(hi, e) + _dot(lo, e)


def _softplus(x):
    return jnp.maximum(x, 0.0) + jnp.log1p(jnp.exp(-jnp.abs(x)))


def _sigmoid(x):
    return 1.0 / (1.0 + jnp.exp(-x))


def _silu(x):
    return x * _sigmoid(x)


def _mod_kernel(c_ref, w_ref, b_ref, o_ref):
    a = _silu(c_ref[...]).astype(BF16)
    o_ref[...] = _dot(a, w_ref[...].astype(BF16)) + b_ref[...]


def _mod_call(c_rows, w_mod, b_mod):
    tn = 1024
    n = w_mod.shape[-1]
    return pl.pallas_call(
        _mod_kernel,
        grid=(DEPTH, n // tn),
        in_specs=[pl.BlockSpec((8, D_MODEL), lambda l, j: (0, 0)),
                  pl.BlockSpec((None, D_MODEL, tn), lambda l, j: (l, 0, j)),
                  pl.BlockSpec((None, 1, tn), lambda l, j: (l, 0, j))],
        out_specs=pl.BlockSpec((None, 8, tn), lambda l, j: (l, 0, j)),
        out_shape=jax.ShapeDtypeStruct((DEPTH, 8, n), F32),
        compiler_params=_cparams(2),
        name="mod_proj",
    )(c_rows, w_mod, b_mod.reshape(DEPTH, 1, n))


def _norm_mm_kernel(*refs, with_gates, relu2):
    if with_gates:
        x_ref, g_ref, sc_ref, sh_ref, w_ref, wg_ref, bg_ref, o_ref, og_ref, h_scr = refs
    else:
        x_ref, g_ref, sc_ref, sh_ref, w_ref, o_ref, h_scr = refs

    @pl.when(pl.program_id(1) == 0)
    def _():
        x = x_ref[...]
        ms = jnp.mean(x * x, axis=-1, keepdims=True)
        y = x * lax.rsqrt(ms + EPS) * g_ref[...]
        hb = (y * (1.0 + sc_ref[...]) + sh_ref[...]).astype(BF16)
        h_scr[...] = hb
        if with_gates:
            pre = _dot(hb, wg_ref[...]) + bg_ref[...]
            lane = lax.broadcasted_iota(jnp.int32, pre.shape, 1)
            act = jnp.where(lane < 4 * M_HEADS,
                            jnp.where(lane % 4 < 2, pre, -_softplus(-pre)),
                            _softplus(pre))
            for blk, off in enumerate(GATE_LANE_OFFSETS):
                own = 4 if blk < M_HEADS else 2 * S_HPG
                v = act if off == 0 else pltpu.roll(act, LANES - off, 1)
                og_ref[:, blk * LANES:(blk + 1) * LANES] = jnp.where(lane < own, v, 0.0)

    acc = _dot(h_scr[...], w_ref[...])
    if relu2:
        r = jnp.maximum(acc, 0.0)
        acc = r * r
    o_ref[...] = acc.astype(o_ref.dtype)


def _norm_mm_call(x, g, sc, sh, rows_per_mod, w, layer, tm, tn, out_dtype, relu2, gates=None, name="norm_mm"):
    t, d = x.shape
    n = w.shape[-1]
    assert t % tm == 0 and n % tn == 0 and rows_per_mod % tm == 0
    mod_idx = lambda i, j: ((i * tm) // rows_per_mod, 0, 0)
    in_specs = [pl.BlockSpec((tm, d), lambda i, j: (i, 0)),
                pl.BlockSpec((None, 1, d), lambda i, j: (layer, 0, 0)),
                pl.BlockSpec((None, 1, d), mod_idx),
                pl.BlockSpec((None, 1, d), mod_idx),
                pl.BlockSpec((None, d, tn), lambda i, j: (layer, 0, j))]
    args = [x, g, sc, sh, w]
    out_specs = [pl.BlockSpec((tm, tn), lambda i, j: (i, j))]
    out_shape = [jax.ShapeDtypeStruct((t, n), out_dtype)]
    if gates is not None:
        wg, bg = gates
        in_specs += [pl.BlockSpec((None, d, LANES), lambda i, j: (layer, 0, 0)),
                     pl.BlockSpec((None, 1, LANES), lambda i, j: (layer, 0, 0))]
        args += [wg, bg]
        out_specs.append(pl.BlockSpec((tm, N_GATES), lambda i, j: (i, 0)))
        out_shape.append(jax.ShapeDtypeStruct((t, N_GATES), F32))
    return pl.pallas_call(
        functools.partial(_norm_mm_kernel, with_gates=gates is not None, relu2=relu2),
        grid=(t // tm, n // tn),
        in_specs=in_specs, out_specs=out_specs, out_shape=out_shape,
        scratch_shapes=[pltpu.VMEM((tm, d), BF16)],
        compiler_params=_cparams(2),
        name=name,
    )(*args)


def _mm_resid_kernel(*refs, nj, tn, final_norm):
    if final_norm:
        a_ref, w_ref, r_ref, gate_ref, fg_ref, o_ref = refs
    else:
        a_ref, w_ref, r_ref, gate_ref, o_ref = refs
    j = pl.program_id(1)
    cols = pl.ds(pl.multiple_of(j * tn, tn), tn)
    o_ref[:, cols] = r_ref[...] + gate_ref[...] * _dot(a_ref[...], w_ref[...])
    if final_norm:
        @pl.when(j == nj - 1)
        def _():
            xn = o_ref[...]
            ms = jnp.mean(xn * xn, axis=-1, keepdims=True)
            o_ref[...] = xn * lax.rsqrt(ms + EPS) * fg_ref[...]


def _mm_resid_call(a, w, layer, resid, gate, rows_per_mod, tm, tn, final_g=None, name="mm_resid"):
    t, kdim = a.shape
    n = w.shape[-1]
    assert t % tm == 0 and n % tn == 0 and rows_per_mod % tm == 0
    nj = n // tn
    mod_idx = lambda i, j: ((i * tm) // rows_per_mod, 0, j)
    in_specs = [pl.BlockSpec((tm, kdim), lambda i, j: (i, 0)),
                pl.BlockSpec((None, kdim, tn), lambda i, j: (layer, 0, j)),
                pl.BlockSpec((tm, tn), lambda i, j: (i, j)),
                pl.BlockSpec((None, 1, tn), mod_idx)]
    args = [a, w, resid, gate]
    if final_g is not None:
        in_specs.append(pl.BlockSpec((1, n), lambda i, j: (0, 0)))
        args.append(final_g)
    return pl.pallas_call(
        functools.partial(_mm_resid_kernel, nj=nj, tn=tn, final_norm=final_g is not None),
        grid=(t // tm, nj),
        in_specs=in_specs,
        out_specs=pl.BlockSpec((tm, n), lambda i, j: (i, 0)),
        out_shape=jax.ShapeDtypeStruct((t, n), F32),
        compiler_params=_cparams(2),
        name=name,
    )(*args)


def _merge_kernel(ym_ref, yr_ref, ts_ref, sg_ref, wm_ref, wr_ref, ws_ref, g0_ref, g1_ref, g2_ref,
                  o_ref, ys_scr):
    @pl.when(pl.program_id(1) == 0)
    def _():
        t = ts_ref[...]
        ms = jnp.mean(t * t, axis=-1, keepdims=True)
        ys_scr[...] = (t * lax.rsqrt(ms + EPS) * sg_ref[...]).astype(BF16)

    acc = _sigmoid(g0_ref[...].astype(F32)) * _dot(ym_ref[...], wm_ref[...])
    acc += _sigmoid(g1_ref[...].astype(F32)) * _dot(yr_ref[...], wr_ref[...])
    acc += _sigmoid(g2_ref[...].astype(F32)) * _dot(ys_scr[...], ws_ref[...])
    o_ref[...] = acc.astype(o_ref.dtype)


def _merge_call(ym, yr, ts, sg, wm, wr, ws, layer, proj, tm, tn):
    t, w = ym.shape
    d = wm.shape[-1]
    assert t % tm == 0 and d % tn == 0
    per_br = d // tn
    yspec = pl.BlockSpec((tm, w), lambda i, j: (i, 0))
    wspec = pl.BlockSpec((None, w, tn), lambda i, j: (layer, 0, j))
    gspec = lambda br: pl.BlockSpec((tm, tn), lambda i, j: (i, br * per_br + j))
    return pl.pallas_call(
        _merge_kernel,
        grid=(t // tm, d // tn),
        in_specs=[yspec, yspec, yspec, pl.BlockSpec((1, w), lambda i, j: (0, 0)),
                  wspec, wspec, wspec, gspec(0), gspec(1), gspec(2)],
        out_specs=pl.BlockSpec((tm, tn), lambda i, j: (i, j)),
        out_shape=jax.ShapeDtypeStruct((t, d), BF16),
        scratch_shapes=[pltpu.VMEM((tm, w), BF16)],
        compiler_params=_cparams(2),
        name="merge",
    )(ym, yr, ts, sg, wm, wr, ws, proj, proj, proj)


def _chunk_masks():
    row = lax.broadcasted_iota(jnp.int32, (CHUNK, CHUNK), 0)
    col = lax.broadcasted_iota(jnp.int32, (CHUNK, CHUNK), 1)
    lower = col <= row
    upper = col >= row
    tri = jnp.where(lower, 1.0, 0.0).astype(BF16)
    return row, col, lower, upper, tri


def _mlstm_kernel(*refs, nc, ns, has_state, emit_state, alias_in):
    it = iter(refs)
    q_ref, k_ref, v_ref, o_ref, g_ref, ng_ref = (next(it) for _ in range(6))
    if has_state:
        c0_ref, n0_ref, m0_ref = (next(it) for _ in range(3))
    if alias_in:
        next(it)
    y_ref = next(it)
    if emit_state:
        cf_ref, nf_ref, mf_ref = (next(it) for _ in range(3))
    hacc = (next(it), next(it))
    c_scr = (next(it), next(it))
    n_scr = (next(it), next(it))
    m_scr = (next(it), next(it))
    p_scr = (next(it), next(it))
    ml_scr = (next(it), next(it))
    bc_scr = (next(it), next(it))
    wl_scr = (next(it), next(it))
    sc_scr = (next(it), next(it))

    _, _, lower, upper, tri = _chunk_masks()
    ones_bf = jnp.ones((CHUNK, LANES), BF16)
    dirs = (0, 1)
    masks = (lower, upper)
    sub8 = lax.broadcasted_iota(jnp.int32, (8, LANES), 0)

    def gates(i, carry, base):
        cidx = (2 * i, 2 * i + 1)
        two = (0, 1)
        rows = [pl.ds(pl.multiple_of(c * CHUNK, CHUNK), CHUNK) for c in cidx]
        grow = [pl.ds(pl.multiple_of(base + c * CHUNK, CHUNK), CHUNK) for c in cidx]
        g = [g_ref[r, :] for r in grow]
        cs = [_cumsum_rows(x, tri) for x in g]
        tot = [x[CHUNK - 1:CHUNK, :] for x in cs]
        g_t = [x.T for x in g]
        cs_t = [x.T for x in cs]
        tot_t = [x[:, CHUNK - 1:CHUNK] for x in cs_t]
        bsum = [[cs[j], tot[j] - cs[j] + g[j]] for j in two]
        bsum_t = [[cs_t[j], tot_t[j] - cs_t[j] + g_t[j]] for j in two]
        items = [(j, d) for j in two for d in dirs]
        b_col = {(j, d): bsum[j][d][:, 2 + d:3 + d] for (j, d) in items}
        i_col = {(j, d): g[j][:, d:d + 1] for (j, d) in items}
        t_row = {(j, d): g_t[j][d:d + 1, :] - bsum_t[j][d][2 + d:3 + d, :] for (j, d) in items}
        b_last = {(j, d): tot[j][:, 2 + d:3 + d] for (j, d) in items}
        logw = {(j, d): jnp.where(masks[d], b_col[j, d] + t_row[j, d], -jnp.inf) for (j, d) in items}
        m_loc = {it_: jnp.max(logw[it_], axis=1, keepdims=True) for it_ in items}
        m_chunk = {it_: jnp.max(b_last[it_] + t_row[it_], axis=1, keepdims=True) for it_ in items}
        for (j, d) in items:
            p_scr[d][rows[j], :] = jnp.exp(logw[j, d] - m_loc[j, d])
            ml_scr[d][rows[j], :] = jnp.broadcast_to(m_loc[j, d], (CHUNK, LANES))
            bc_scr[d][rows[j], :] = jnp.broadcast_to(b_col[j, d], (CHUNK, LANES))
            wl_scr[d][rows[j], :] = jnp.broadcast_to(b_last[j, d] - b_col[j, d] + i_col[j, d], (CHUNK, LANES))
            sc_scr[d][pl.ds(pl.multiple_of(cidx[j] * 8, 8), 8), :] = jnp.where(
                sub8 == 0, jnp.broadcast_to(m_chunk[j, d], (8, LANES)), jnp.broadcast_to(b_last[j, d], (8, LANES)))
        return carry

    def rep2(x):
        return jnp.concatenate([x, x], axis=1)

    def scan(i, carry, base):
        cidx = (i, nc - 1 - i)
        rows = [pl.ds(pl.multiple_of(c * CHUNK, CHUNK), CHUNK) for c in cidx]
        grow = [pl.ds(pl.multiple_of(base + c * CHUNK, CHUNK), CHUNK) for c in cidx]
        sc = [sc_scr[d][pl.ds(pl.multiple_of(cidx[d] * 8, 8), 8), :] for d in dirs]
        m_chunk = [sc[d][0:1, 0:1] for d in dirs]
        b_last = [sc[d][1:2, 0:1] for d in dirs]
        qb = [q_ref[r, :].astype(BF16) for r in grow]
        k = [k_ref[r, :].astype(F32) * (M_DK ** -0.5) for r in grow]
        kb = [x.astype(BF16) for x in k]
        vb = [v_ref[r, :].astype(BF16) for r in grow]
        qk = [_dot_nt(qb[d], kb[d]) for d in dirs]
        sb = [(qk[d] * p_scr[d][rows[d], :]).astype(BF16) for d in dirs]
        a_num = [_dot(sb[d], vb[d]) for d in dirs]
        a_den = [_dot(sb[d], ones_bf) for d in dirs]

        m_prev = [m_scr[d][0:1, 0:1] for d in dirs]
        c_prev = [c_scr[d][...] for d in dirs]
        n_prev = [n_scr[d][0:1, :] for d in dirs]
        m_new = [jnp.maximum(b_last[d] + m_prev[d], m_chunk[d]) for d in dirs]
        kw = [k[d] * rep2(jnp.exp(wl_scr[d][rows[d], :] - m_new[d])) for d in dirs]
        kw_t = [x.T.astype(BF16) for x in kw]
        q_c = [_dot(qb[d], c_prev[d].astype(BF16)) for d in dirs]
        u_c = [_dot(kw_t[d], vb[d]) for d in dirs]
        m_loc = [ml_scr[d][rows[d], :] for d in dirs]
        gg = [bc_scr[d][rows[d], :] + m_prev[d] for d in dirs]
        m_tot = [jnp.maximum(m_loc[d], gg[d]) for d in dirs]
        e_intra = [jnp.exp(m_loc[d] - m_tot[d]) for d in dirs]
        e_inter = [jnp.exp(gg[d] - m_tot[d]) for d in dirs]
        q_n = [_dot_nt(qb[d], jnp.broadcast_to(n_prev[d].astype(BF16), (LANES, M_DK))) for d in dirs]
        den = [e_intra[d] * a_den[d] + e_inter[d] * q_n[d] for d in dirs]
        inv = [1.0 / jnp.maximum(jnp.abs(den[d]), jnp.exp(-m_tot[d])) for d in dirs]
        for d in dirs:
            hacc[d][rows[d], :] = rep2(e_intra[d] * inv[d]) * a_num[d] + rep2(e_inter[d] * inv[d]) * q_c[d]

        u_n = [jnp.sum(x, axis=0, keepdims=True) for x in kw]
        a = [jnp.exp(b_last[d] + m_prev[d] - m_new[d]) for d in dirs]
        for d in dirs:
            c_scr[d][...] = a[d] * c_prev[d] + u_c[d]
            n_scr[d][0:1, :] = a[d] * n_prev[d] + u_n[d]
            m_scr[d][0:1, :] = jnp.broadcast_to(m_new[d], (1, LANES))
        return carry

    def finish(c, carry, base):
        rows = pl.ds(pl.multiple_of(c * CHUNK, CHUNK), CHUNK)
        grow = pl.ds(pl.multiple_of(base + c * CHUNK, CHUNK), CHUNK)
        ht = hacc[0][rows, :] + hacc[1][rows, :]
        ms = jnp.mean(ht * ht, axis=1, keepdims=True)
        y = ht * lax.rsqrt(ms + EPS) * ng_ref[...] * _sigmoid(o_ref[grow, :].astype(F32))
        y_ref[grow, :] = y.astype(y_ref.dtype)
        return carry

    for sq in range(ns):
        base = sq * nc * CHUNK
        for d in range(2):
            if has_state:
                c_scr[d][...] = c0_ref[sq, d]
                n_scr[d][0:1, :] = n0_ref[sq, d:d + 1, :]
                m_scr[d][0:1, :] = m0_ref[sq, d:d + 1, :]
            else:
                c_scr[d][...] = jnp.zeros_like(c_scr[d])
                n_scr[d][...] = jnp.zeros_like(n_scr[d])
                m_scr[d][...] = jnp.zeros_like(m_scr[d])
        lax.fori_loop(0, nc // 2, functools.partial(gates, base=base), 0, unroll=2)
        lax.fori_loop(0, nc, functools.partial(scan, base=base), 0, unroll=SCAN_UNROLL)
        lax.fori_loop(0, nc, functools.partial(finish, base=base), 0, unroll=SCAN_UNROLL)
        if emit_state:
            for d in range(2):
                if alias_in:
                    cf_ref[sq, d] = c_scr[d][...]
                else:
                    for slot in range(DEPTH):
                        cf_ref[sq, slot, d] = c_scr[d][...]
                nf_ref[sq, d:d + 1, :] = n_scr[d][0:1, :]
                mf_ref[sq, d:d + 1, :] = m_scr[d][0:1, :]


def _mlstm_call(proj, gates, ng, nb, seq, ns, state=None, emit=None):
    emit_state = emit is not None
    assert nb % ns == 0
    t = proj.shape[0]
    nc = seq // CHUNK
    dk = M_DK
    cspec = lambda base: pl.BlockSpec((ns * seq, dk), lambda b, h: (b, base // dk + h))
    in_specs = [cspec(OFF_MQ), cspec(OFF_MK), cspec(OFF_MV), cspec(OFF_MO),
                pl.BlockSpec((ns * seq, LANES), lambda b, h: (b, h)),
                pl.BlockSpec((1, dk), lambda b, h: (0, h))]
    args = [proj, proj, proj, proj, gates, ng]
    if state is not None:
        sl = state[0]
        in_specs += [pl.BlockSpec((ns, None, 2, None, dk, dk), lambda b, h: (b, sl, 0, h, 0, 0)),
                     pl.BlockSpec((ns, None, 2, dk), lambda b, h: (b, h, 0, 0)),
                     pl.BlockSpec((ns, None, 2, LANES), lambda b, h: (b, h, 0, 0))]
        args += list(state[1:])
    out_specs = [pl.BlockSpec((ns * seq, dk), lambda b, h: (b, h))]
    out_shape = [jax.ShapeDtypeStruct((t, M_HEADS * dk), BF16)]
    aliases = {}
    if emit_state:
        layer, prev = emit
        slot_blk, slot_idx = (None, layer) if prev is not None else (DEPTH, 0)
        out_specs += [pl.BlockSpec((ns, slot_blk, 2, None, dk, dk), lambda b, h: (b, slot_idx, 0, h, 0, 0)),
                      pl.BlockSpec((ns, None, 2, dk), lambda b, h: (b, h, 0, 0)),
                      pl.BlockSpec((ns, None, 2, LANES), lambda b, h: (b, h, 0, 0))]
        out_shape += [jax.ShapeDtypeStruct((nb, DEPTH, 2, M_HEADS, dk, dk), F32),
                      jax.ShapeDtypeStruct((nb, M_HEADS, 2, dk), F32),
                      jax.ShapeDtypeStruct((nb, M_HEADS, 2, LANES), F32)]
        if prev is not None:
            in_specs.append(pl.BlockSpec(memory_space=pl.ANY))
            args.append(prev)
            aliases = {len(args) - 1: 1}
    return pl.pallas_call(
        functools.partial(_mlstm_kernel, nc=nc, ns=ns, has_state=state is not None, emit_state=emit_state,
                          alias_in=bool(aliases)),
        grid=(nb // ns, M_HEADS),
        in_specs=in_specs, out_specs=out_specs, out_shape=out_shape, input_output_aliases=aliases,
        scratch_shapes=([pltpu.VMEM((seq, dk), F32)] * 2 + [pltpu.VMEM((dk, dk), F32)] * 2
                        + [pltpu.VMEM((8, dk), F32)] * 2 + [pltpu.VMEM((8, LANES), F32)] * 2
                        + [pltpu.VMEM((seq, LANES), F32)] * 8 + [pltpu.VMEM((nc * 8, LANES), F32)] * 2),
        compiler_params=_cparams(2),
        name="mlstm",
    )(*args)


R_HPS = 2


def _ret_kernel(*refs, nc, ns, has_state, emit_state, rope, alias_in):
    it = iter(refs)
    q_ref, k_ref, v_ref, g_ref, lg_ref, ng_ref = (next(it) for _ in range(6))
    if rope:
        cos_ref, sin_ref = next(it), next(it)
    if has_state:
        s0_ref = next(it)
    if alias_in:
        next(it)
    y_ref = next(it)
    if emit_state:
        sf_ref = next(it)
    chains = [(h, d) for h in range(R_HPS) for d in range(2)]
    oacc = {c: next(it) for c in chains}
    s_scr = {c: next(it) for c in chains}
    qs_scr, ks_scr, kt_scr = (next(it) for _ in range(3))

    row, col, lower, upper, _ = _chunk_masks()
    rel = (row - col).astype(F32)
    pos = row.astype(F32)
    decay, w_q, w_k, chunk_decay = {}, {}, {}, {}
    for (h, d) in chains:
        lgd = -jnp.exp(lg_ref[h, d:d + 1, :])
        if d == 0:
            decay[h, d] = jnp.where(lower, jnp.exp(rel * lgd), 0.0)
            w_q[h, d] = jnp.exp((pos + 1.0) * lgd)
            w_k[h, d] = jnp.exp((CHUNK - 1.0 - pos) * lgd)
        else:
            decay[h, d] = jnp.where(upper, jnp.exp(-rel * lgd), 0.0)
            w_q[h, d] = jnp.exp((CHUNK - pos) * lgd)
            w_k[h, d] = jnp.exp(pos * lgd)
        chunk_decay[h, d] = jnp.exp(CHUNK * lgd)

    def hcols(h):
        return slice(h * R_DK, (h + 1) * R_DK)

    def prep(c, carry, base):
        rows = pl.ds(pl.multiple_of(c * CHUNK, CHUNK), CHUNK)
        grow = pl.ds(pl.multiple_of(base + c * CHUNK, CHUNK), CHUNK)
        for h in range(R_HPS):
            q = q_ref[grow, hcols(h)].astype(F32)
            k = k_ref[grow, hcols(h)].astype(F32) * (R_DK ** -0.5)
            if rope:
                cs, sn = cos_ref[grow, :], sin_ref[grow, :]
                q = q * cs + pltpu.roll(q, R_DK // 2, 1) * sn
                k = k * cs + pltpu.roll(k, R_DK // 2, 1) * sn
            qs_scr[rows, hcols(h)] = q.astype(BF16)
            ks_scr[rows, hcols(h)] = k.astype(BF16)
            kt_scr[hcols(h), rows] = k.T.astype(BF16)
        return carry

    def scan(i, carry, base):
        rows = {0: pl.ds(pl.multiple_of(i * CHUNK, CHUNK), CHUNK),
                1: pl.ds(pl.multiple_of((nc - 1 - i) * CHUNK, CHUNK), CHUNK)}
        grow = {0: pl.ds(pl.multiple_of(base + i * CHUNK, CHUNK), CHUNK),
                1: pl.ds(pl.multiple_of(base + (nc - 1 - i) * CHUNK, CHUNK), CHUNK)}
        qb = {(h, d): qs_scr[rows[d], hcols(h)] for (h, d) in chains}
        kb = {(h, d): ks_scr[rows[d], hcols(h)] for (h, d) in chains}
        kt = {(h, d): kt_scr[hcols(h), rows[d]] for (h, d) in chains}
        vb = {(h, d): v_ref[grow[d], hcols(h)].astype(BF16) for (h, d) in chains}
        vw = {c: (vb[c].astype(F32) * w_k[c]).astype(BF16) for c in chains}
        s_prev = {c: s_scr[c][...] for c in chains}
        sc = {c: (_dot_nt(qb[c], kb[c]) * decay[c]).astype(BF16) for c in chains}
        inter = {c: _dot(qb[c], s_prev[c].astype(BF16)) for c in chains}
        u = {c: _dot(kt[c], vw[c]) for c in chains}
        intra = {c: _dot(sc[c], vb[c]) for c in chains}
        for (h, d) in chains:
            oacc[h, d][rows[d], :] = intra[h, d] + w_q[h, d] * inter[h, d]
            s_scr[h, d][...] = chunk_decay[h, d] * s_prev[h, d] + u[h, d]
        return carry

    def finish(c, carry, base):
        rows = pl.ds(pl.multiple_of(c * CHUNK, CHUNK), CHUNK)
        grow = pl.ds(pl.multiple_of(base + c * CHUNK, CHUNK), CHUNK)
        for h in range(R_HPS):
            ot = oacc[h, 0][rows, :] + oacc[h, 1][rows, :]
            ms = jnp.mean(ot * ot, axis=1, keepdims=True)
            y = ot * lax.rsqrt(ms + EPS) * ng_ref[:, hcols(h)] * _silu(g_ref[grow, hcols(h)].astype(F32))
            y_ref[grow, hcols(h)] = y.astype(y_ref.dtype)
        return carry

    for sq in range(ns):
        base = sq * nc * CHUNK
        for (h, d) in chains:
            if has_state:
                s_scr[h, d][...] = s0_ref[sq, d, h]
            else:
                s_scr[h, d][...] = jnp.zeros_like(s_scr[h, d])
        lax.fori_loop(0, nc, functools.partial(prep, base=base), 0, unroll=SCAN_UNROLL)
        lax.fori_loop(0, nc, functools.partial(scan, base=base), 0, unroll=SCAN_UNROLL)
        lax.fori_loop(0, nc, functools.partial(finish, base=base), 0, unroll=SCAN_UNROLL)
        if emit_state:
            for (h, d) in chains:
                if alias_in:
                    sf_ref[sq, d, h] = s_scr[h, d][...]
                else:
                    for slot in range(DEPTH):
                        sf_ref[sq, slot, d, h] = s_scr[h, d][...]


def _ret_call(proj, lgr, ng, nb, seq, ns, rope=None, state=None, emit=None):
    emit_state = emit is not None
    assert nb % ns == 0 and (rope is None or ns == 1)
    t = proj.shape[0]
    nc = seq // CHUNK
    dk = R_DK
    bw = R_HPS * dk
    cspec = lambda base: pl.BlockSpec((ns * seq, bw), lambda b, h: (b, base // bw + h))
    in_specs = [cspec(OFF_RQ), cspec(OFF_RK), cspec(OFF_RV), cspec(OFF_RG),
                pl.BlockSpec((R_HPS, 2, LANES), lambda b, h: (h, 0, 0)),
                pl.BlockSpec((1, bw), lambda b, h: (0, h))]
    args = [proj, proj, proj, proj, lgr, ng]
    if rope is not None:
        in_specs += [pl.BlockSpec((seq, dk), lambda b, h: (0, 0))] * 2
        args += list(rope)
    if state is not None:
        sl = state[0]
        in_specs.append(pl.BlockSpec((ns, None, 2, R_HPS, dk, dk), lambda b, h: (b, sl, 0, h, 0, 0)))
        args.append(state[1])
    out_specs = [pl.BlockSpec((ns * seq, bw), lambda b, h: (b, h))]
    out_shape = [jax.ShapeDtypeStruct((t, R_HEADS * dk), BF16)]
    aliases = {}
    if emit_state:
        layer, prev = emit
        slot_blk, slot_idx = (None, layer) if prev is not None else (DEPTH, 0)
        out_specs.append(pl.BlockSpec((ns, slot_blk, 2, R_HPS, dk, dk), lambda b, h: (b, slot_idx, 0, h, 0, 0)))
        out_shape.append(jax.ShapeDtypeStruct((nb, DEPTH, 2, R_HEADS, dk, dk), F32))
        if prev is not None:
            in_specs.append(pl.BlockSpec(memory_space=pl.ANY))
            args.append(prev)
            aliases = {len(args) - 1: 1}
    n_chain = 2 * R_HPS
    return pl.pallas_call(
        functools.partial(_ret_kernel, nc=nc, ns=ns, has_state=state is not None, emit_state=emit_state,
                          rope=rope is not None, alias_in=bool(aliases)),
        grid=(nb // ns, R_HEADS // R_HPS),
        in_specs=in_specs, out_specs=out_specs, out_shape=out_shape, input_output_aliases=aliases,
        scratch_shapes=([pltpu.VMEM((seq, dk), F32)] * n_chain + [pltpu.VMEM((dk, dk), F32)] * n_chain
                        + [pltpu.VMEM((seq, bw), BF16)] * 2 + [pltpu.VMEM((bw, seq), BF16)]),
        compiler_params=_cparams(2),
        name="retention",
    )(*args)


def _ssd_kernel(*refs, nc, ns, has_state, emit_state, alias_in):
    it = iter(refs)
    (x_ref, b_ref, c_ref, z_ref, g_ref, wx_ref, wb_ref, wc_ref, bx_ref, bb_ref, bc_ref,
     al_ref, sd_ref) = (next(it) for _ in range(13))
    if has_state:
        h0_ref = next(it)
    if alias_in:
        next(it)
    y_ref = next(it)
    if emit_state:
        hf_ref = next(it)
    xs_scr, bt_scr, bb_scr, cc_scr, ht_scr, yb_scr = (next(it) for _ in range(6))

    _, _, lower, upper, tri = _chunk_masks()
    er = lax.broadcasted_iota(jnp.int32, (LANES, S_GW), 0)
    ec = lax.broadcasted_iota(jnp.int32, (LANES, S_GW), 1) // S_P
    expand = [jnp.where(er == ec + d * S_HPG, 1.0, 0.0).astype(BF16) for d in range(2)]
    a_row = -jnp.exp(al_ref[...])

    wr = lax.broadcasted_iota(jnp.int32, (CHUNK, CHUNK + 32), 0)
    wc = lax.broadcasted_iota(jnp.int32, (CHUNK, CHUNK + 32), 1)
    shifts = [jnp.where(wc == wr + 16 + sft, 1.0, 0.0).astype(BF16) for sft in (-1, 1, 2)]

    def conv_silu(ref, w_ref, bias_ref, c, base):
        off = base + c * CHUNK
        cur = ref[pl.ds(pl.multiple_of(off, CHUNK), CHUNK), :]
        prev_w = ref[pl.ds(pl.multiple_of(jnp.maximum(off - 16, 0), 16), 16), :]
        next_w = ref[pl.ds(pl.multiple_of(jnp.minimum(off + CHUNK, ns * nc * CHUNK - 16), 16), 16), :]
        prev_w = jnp.where(c > 0, prev_w, jnp.zeros_like(prev_w))
        next_w = jnp.where(c < nc - 1, next_w, jnp.zeros_like(next_w))
        win = jnp.concatenate([prev_w, cur, next_w], axis=0)
        xm1, xp1, xp2 = (_dot(sm, win) for sm in shifts)
        w = w_ref[...]
        y = w[0:1, :] * xm1 + w[1:2, :] * cur.astype(F32) + w[2:3, :] * xp1 + w[3:4, :] * xp2 + bias_ref[...]
        return _silu(y)

    def prep(c, carry, base):
        off = pl.multiple_of(c * CHUNK, CHUNK)
        rows = pl.ds(off, CHUNK)
        xs_scr[rows, :] = conv_silu(x_ref, wx_ref, bx_ref, c, base)
        bm = conv_silu(b_ref, wb_ref, bb_ref, c, base)
        bb_scr[rows, :] = bm.astype(BF16)
        bt_scr[:, rows] = bm.T.astype(BF16)
        cc_scr[rows, :] = conv_silu(c_ref, wc_ref, bc_ref, c, base).astype(BF16)
        return carry

    dirs = (0, 1)
    masks = (lower, upper)
    first_half = lax.broadcasted_iota(jnp.int32, (CHUNK, LANES), 1) < S_P

    def scan(i, carry, base):
        rows = [pl.ds(pl.multiple_of(c * CHUNK, CHUNK), CHUNK) for c in (i, nc - 1 - i)]
        grow = [pl.ds(pl.multiple_of(base + c * CHUNK, CHUNK), CHUNK) for c in (i, nc - 1 - i)]
        dt = [g_ref[r, :] for r in grow]
        a = [x * a_row for x in dt]
        cs = [_cumsum_rows(x, tri) for x in a]
        tot = [x[CHUNK - 1:CHUNK, :] for x in cs]
        bsum = [cs[0], tot[1] - cs[1] + a[1]]
        bsum_t = [x.T for x in bsum]
        dt_t = [x.T for x in dt]
        xs = [xs_scr[r, :] for r in rows]
        xb = [x.astype(BF16) for x in xs]
        bmb = [bb_scr[r, :] for r in rows]
        cmb = [cc_scr[r, :] for r in rows]
        btb = [bt_scr[:, r] for r in rows]
        cb = [_dot_nt(cmb[d], bmb[d]) for d in dirs]
        ht = [ht_scr[d] for d in dirs]
        inter = [_dot(cmb[d], ht[d].astype(BF16)) for d in dirs]
        ex = [_expand(jnp.exp(bsum[d]), expand[d], exact=False) for d in dirs]
        wexp = [_expand(jnp.exp(tot[d] - bsum[d]) * dt[d], expand[d], exact=False) for d in dirs]
        cd = [_expand(jnp.broadcast_to(jnp.exp(tot[d]), (8, LANES)), expand[d])[0:1, :] for d in dirs]
        u_t = [_dot(btb[d], (xs[d] * wexp[d]).astype(BF16)) for d in dirs]
        cb_bf = [x.astype(BF16) for x in cb]
        cbdt = [[cb_bf[d] * dt_t[d][d * S_HPG + k:d * S_HPG + k + 1, :].astype(BF16) for k in range(S_HPG)] for d in dirs]
        ys = [[], []]
        for p in range(S_HPG // 2):
            for d in dirs:
                ms = []
                for k in (2 * p, 2 * p + 1):
                    ln = d * S_HPG + k
                    decay = jnp.exp(jnp.where(masks[d], bsum[d][:, ln:ln + 1] - bsum_t[d][ln:ln + 1, :], -jnp.inf))
                    ms.append(cbdt[d][k] * decay.astype(BF16))
                xp = xb[d][:, p * LANES:(p + 1) * LANES]
                zero = jnp.zeros_like(xp)
                rhs = jnp.concatenate([jnp.where(first_half, xp, zero), jnp.where(first_half, zero, xp)], axis=0)
                ys[d].append(_dot(jnp.concatenate(ms, axis=1), rhs))
        y_ref[grow[0], :] = jnp.concatenate(ys[0], axis=1) + ex[0] * inter[0]
        yb_scr[rows[1], :] = jnp.concatenate(ys[1], axis=1) + ex[1] * inter[1]
        for d in dirs:
            ht_scr[d] = ht[d] * cd[d] + u_t[d]
        return carry

    def finish(c, carry, base):
        rows = pl.ds(pl.multiple_of(c * CHUNK, CHUNK), CHUNK)
        grow = pl.ds(pl.multiple_of(base + c * CHUNK, CHUNK), CHUNK)
        yt = y_ref[grow, :] + yb_scr[rows, :] + sd_ref[...] * xs_scr[rows, :]
        y_ref[grow, :] = yt * _silu(z_ref[grow, :].astype(F32))
        return carry

    for sq in range(ns):
        base = sq * nc * CHUNK
        lax.fori_loop(0, nc, functools.partial(prep, base=base), 0, unroll=SCAN_UNROLL)
        for d in range(2):
            if has_state:
                ht_scr[d] = h0_ref[sq, d].T
            else:
                ht_scr[d] = jnp.zeros((S_N, S_GW), F32)
        lax.fori_loop(0, nc, functools.partial(scan, base=base), 0, unroll=SCAN_UNROLL)
        lax.fori_loop(0, nc, functools.partial(finish, base=base), 0, unroll=SCAN_UNROLL)
        if emit_state:
            for d in range(2):
                h_fin = ht_scr[d].T
                if alias_in:
                    hf_ref[sq, d] = h_fin
                else:
                    for slot in range(DEPTH):
                        hf_ref[sq, slot, d] = h_fin


def _ssd_call(proj, gates, conv_w, conv_b, alog, sd, nb, seq, ns, state=None, emit=None):
    emit_state = emit is not None
    assert nb % ns == 0
    t = proj.shape[0]
    nc = seq // CHUNK
    gw = S_GW
    n = S_N
    xw = S_HEADS * S_P
    in_specs = [pl.BlockSpec((ns * seq, gw), lambda b, g: (b, OFF_SX // gw + g)),
                pl.BlockSpec((ns * seq, n), lambda b, g: (b, OFF_SB // n + g)),
                pl.BlockSpec((ns * seq, n), lambda b, g: (b, OFF_SC // n + g)),
                pl.BlockSpec((ns * seq, gw), lambda b, g: (b, OFF_SZ // gw + g)),
                pl.BlockSpec((ns * seq, LANES), lambda b, g: (b, M_HEADS + g)),
                pl.BlockSpec((4, gw), lambda b, g: (0, g)),
                pl.BlockSpec((4, n), lambda b, g: (0, xw // n + g)),
                pl.BlockSpec((4, n), lambda b, g: (0, xw // n + S_GROUPS + g)),
                pl.BlockSpec((1, gw), lambda b, g: (0, g)),
                pl.BlockSpec((1, n), lambda b, g: (0, xw // n + g)),
                pl.BlockSpec((1, n), lambda b, g: (0, xw // n + S_GROUPS + g)),
                pl.BlockSpec((None, 1, LANES), lambda b, g: (g, 0, 0)),
                pl.BlockSpec((1, gw), lambda b, g: (0, g))]
    args = [proj, proj, proj, proj, gates, conv_w, conv_w, conv_w, conv_b, conv_b, conv_b, alog, sd]
    if state is not None:
        sl = state[0]
        in_specs.append(pl.BlockSpec((ns, None, 2, None, gw, n), lambda b, g: (b, sl, 0, g, 0, 0)))
        args.append(state[1])
    out_specs = [pl.BlockSpec((ns * seq, gw), lambda b, g: (b, g))]
    out_shape = [jax.ShapeDtypeStruct((t, xw), F32)]
    aliases = {}
    if emit_state:
        layer, prev = emit
        slot_blk, slot_idx = (None, layer) if prev is not None else (DEPTH, 0)
        out_specs.append(pl.BlockSpec((ns, slot_blk, 2, None, gw, n), lambda b, g: (b, slot_idx, 0, g, 0, 0)))
        out_shape.append(jax.ShapeDtypeStruct((nb, DEPTH, 2, S_GROUPS, gw, n), F32))
        if prev is not None:
            in_specs.append(pl.BlockSpec(memory_space=pl.ANY))
            args.append(prev)
            aliases = {len(args) - 1: 1}
    return pl.pallas_call(
        functools.partial(_ssd_kernel, nc=nc, ns=ns, has_state=state is not None, emit_state=emit_state,
                          alias_in=bool(aliases)),
        grid=(nb // ns, S_GROUPS),
        in_specs=in_specs, out_specs=out_specs, out_shape=out_shape, input_output_aliases=aliases,
        scratch_shapes=[pltpu.VMEM((seq, gw), F32), pltpu.VMEM((n, seq), BF16),
                        pltpu.VMEM((seq, n), BF16), pltpu.VMEM((seq, n), BF16),
                        pltpu.VMEM((2, n, gw), F32), pltpu.VMEM((seq, gw), F32)],
        compiler_params=_cparams(2),
        name="ssd",
    )(*args)


_IN_SIZES = (1024, 1024, 1024, 1024, 8, 8, 1024, 1024, 1024, 1024, 1024, 1536, 32, 6144)
_IN_OFFS = np.concatenate([[0], np.cumsum(_IN_SIZES)])


def _gate_lane_index():
    idx = -np.ones((LANES,), np.int64)
    for h in range(M_HEADS):
        base = GATE_LANE_OFFSETS[h]
        idx[base + 0] = 0 * M_HEADS + h
        idx[base + 1] = 1 * M_HEADS + h
        idx[base + 2] = 2 * M_HEADS + 0 * M_HEADS + h
        idx[base + 3] = 2 * M_HEADS + 1 * M_HEADS + h
    for g in range(S_GROUPS):
        base = GATE_LANE_OFFSETS[M_HEADS + g]
        for d in range(2):
            for k in range(S_HPG):
                idx[base + d * S_HPG + k] = 4 * M_HEADS + d * S_HEADS + g * S_HPG + k
    return idx


def _place(vals, idx):
    taken = jnp.take(vals, jnp.asarray(np.maximum(idx, 0)), axis=-1)
    return jnp.where(jnp.asarray(idx >= 0), taken, 0.0)


def _prep_params(w_in, m_igate_b, m_fgate_b, s_dt_bias, s_a_log, r_decay, s_d):
    o = _IN_OFFS
    w_big = jnp.concatenate([w_in[:, :, o[13]:o[14]], w_in[:, :, o[0]:o[4]], w_in[:, :, o[6]:o[12]]],
                            axis=2).astype(BF16)
    w_small = jnp.concatenate([w_in[:, :, o[4]:o[6]], w_in[:, :, o[12]:o[13]]], axis=2)
    gidx = _gate_lane_index()
    w_g = _place(w_small, gidx).astype(BF16)
    b_small = jnp.concatenate([m_igate_b.reshape(DEPTH, -1), m_fgate_b.reshape(DEPTH, -1),
                               s_dt_bias.reshape(DEPTH, -1)], axis=1)
    b_g = _place(b_small, gidx).reshape(DEPTH, 1, LANES)
    aidx = -np.ones((S_GROUPS, LANES), np.int64)
    for g in range(S_GROUPS):
        for d in range(2):
            for k in range(S_HPG):
                aidx[g, d * S_HPG + k] = d * S_HEADS + g * S_HPG + k
    a_flat = s_a_log.reshape(DEPTH, -1)
    alog = jnp.stack([_place(a_flat, aidx[g]) for g in range(S_GROUPS)], axis=1).reshape(DEPTH, S_GROUPS, 1, LANES)
    lgr = jnp.broadcast_to(jnp.swapaxes(r_decay, 1, 2)[..., None], (DEPTH, R_HEADS, 2, LANES))
    sd = jnp.repeat(s_d, S_P, axis=1).reshape(DEPTH, 1, S_HEADS * S_P)
    return w_big, w_g, b_g, alog, lgr, sd


def _rope_tables(seq):
    n_rows = seq // GRID_W
    rows = jnp.repeat(jnp.arange(n_rows, dtype=F32), GRID_W)
    cols = jnp.tile(jnp.arange(GRID_W, dtype=F32), n_rows)
    inv = ROPE_BASE ** (-jnp.arange(ROPE_FREQS, dtype=F32) / ROPE_FREQS)
    ang = jnp.concatenate([rows[:, None] * inv, cols[:, None] * inv], -1)
    cos, sin = jnp.cos(ang), jnp.sin(ang)
    return jnp.concatenate([cos, cos], -1), jnp.concatenate([-sin, sin], -1)


def _layer(x, l, mods, rows_per_mod, nb, seq, pw, rope, state, emit, final_g):
    sh_a, sc_a, g_a, sh_f, sc_f, g_f = mods
    tm_l, tm_s = min(TM_LARGE, rows_per_mod), min(TM_SMALL, rows_per_mod)
    proj, gates = _norm_mm_call(x, pw["norm_mix_g"], sc_a, sh_a, rows_per_mod, pw["w_big"], l, tm_l, TN_IN_PROJ,
                                BF16, False, gates=(pw["w_g"], pw["b_g"]), name="in_proj")
    st_m = st_r = st_s = None
    if state is not None:
        st_m, st_r, st_s = state
    em_m = em_r = em_s = None
    if emit is not None:
        em_m, em_r, em_s = ((l, prev) for prev in emit)
    ns = 1 if rope is not None else max(n for n in range(1, max(1, SEQ_ROWS_PER_STEP // seq) + 1) if nb % n == 0)
    om = _mlstm_call(proj, gates, pw["m_norm_g"][l], nb, seq, 1, state=st_m, emit=em_m)
    orr = _ret_call(proj, pw["lgr"][l], pw["r_norm_g"][l], nb, seq, ns, rope=rope, state=st_r, emit=em_r)
    os_ = _ssd_call(proj, gates, pw["s_conv_w"][l], pw["s_conv_b"][l], pw["alog"][l], pw["sd"][l], nb, seq, ns,
                    state=st_s, emit=em_s)
    merged = _merge_call(om[0], orr[0], os_[0], pw["s_norm_g"][l], pw["w_br_m"], pw["w_br_r"], pw["w_br_s"], l,
                         proj, tm_s, TN_MERGE)
    x = _mm_resid_call(merged, pw["w_out"], l, x, g_a, rows_per_mod, tm_s, D_MODEL, name="out_proj")
    (hid,) = _norm_mm_call(x, pw["norm_mlp_g"], sc_f, sh_f, rows_per_mod, pw["w_ff1"], l, tm_l, TN_MLP_UP,
                           BF16, True, name="mlp_up")
    x = _mm_resid_call(hid, pw["w_ff2"], l, x, g_f, rows_per_mod, tm_s, TN_MLP_DOWN, final_g=final_g, name="mlp_down")
    new_state = (om[1:], orr[1:], os_[1:]) if emit is not None else None
    return x, new_state


def kernel(x_prompt, x_sample, c, state_mlstm_C, state_mlstm_n, state_mlstm_m, state_ret, state_ssd, c_ctx, w_mod, b_mod, norm_mix_g, norm_mlp_g, w_in, m_igate_b, m_fgate_b, m_norm_g, r_decay, r_norm_g, s_conv_w, s_conv_b, s_dt_bias, s_a_log, s_d, s_norm_g, w_br_m, w_br_r, w_br_s, w_out, w_ff1, w_ff2, final_norm_g):
    bp, lp, d = x_prompt.shape
    bs, ls, _ = x_sample.shape
    xp = x_prompt.reshape(bp * lp, d)
    xs = x_sample.reshape(bs * ls, d)

    c_rows = jnp.zeros((8, d), F32).at[:bs].set(c).at[bs].set(c_ctx)
    mod = _mod_call(c_rows, w_mod, b_mod)
    rope = _rope_tables(ls)
    final_g = final_norm_g.reshape(1, d)

    w_big, w_g, b_g, alog, lgr, sd = _prep_params(w_in, m_igate_b, m_fgate_b, s_dt_bias, s_a_log, r_decay, s_d)
    pw = dict(w_big=w_big, w_g=w_g, b_g=b_g, alog=alog, lgr=lgr, sd=sd,
              norm_mix_g=norm_mix_g.reshape(DEPTH, 1, d), norm_mlp_g=norm_mlp_g.reshape(DEPTH, 1, d),
              m_norm_g=m_norm_g.reshape(DEPTH, 1, -1), r_norm_g=r_norm_g.reshape(DEPTH, 1, -1),
              s_norm_g=s_norm_g.reshape(DEPTH, 1, -1),
              s_conv_w=s_conv_w, s_conv_b=s_conv_b.reshape(DEPTH, 1, -1),
              w_br_m=w_br_m.astype(BF16), w_br_r=w_br_r.astype(BF16), w_br_s=w_br_s.astype(BF16),
              w_out=w_out.astype(BF16), w_ff1=w_ff1.astype(BF16), w_ff2=w_ff2.astype(BF16))
    cache_c = state_mlstm_C
    cache_r = state_ret
    cache_s = state_ssd.reshape(bs, DEPTH, 2, S_GROUPS, S_GW, S_N)

    big = (None, None, None)
    st_n, st_m = [], []
    for l in range(DEPTH):
        parts = mod[l].reshape(8, 6, 1, d)
        mods_ctx = tuple(parts[bs:bs + 1, i] for i in range(6))
        mods_lat = tuple(parts[:bs, i] for i in range(6))
        fg = final_g if l == DEPTH - 1 else None

        xp, st = _layer(xp, l, mods_ctx, bp * lp, bp, lp, pw, None, None, big, fg)
        (cf, nf, mf), (rf,), (hf,) = st
        big = (cf, rf, hf)
        st_n.append(jnp.transpose(nf, (0, 2, 1, 3)))
        st_m.append(jnp.transpose(mf[..., 0], (0, 2, 1)))

        cache = (
            (l, cache_c,
             jnp.transpose(state_mlstm_n[:, l], (0, 2, 1, 3)),
             jnp.broadcast_to(jnp.transpose(state_mlstm_m[:, l], (0, 2, 1))[..., None], (bs, M_HEADS, 2, LANES))),
            (l, cache_r),
            (l, cache_s),
        )
        xs, _ = _layer(xs, l, mods_lat, ls, bs, ls, pw, rope, cache, None, fg)

    return (xp.reshape(bp, lp, d), xs.reshape(bs, ls, d),
            big[0], jnp.stack(st_n, 1), jnp.stack(st_m, 1), big[1],
            big[2].reshape(bp, DEPTH, 2, S_HEADS, S_P, S_N))
```
